```python
import jax
import jax.numpy as jnp
from jax import lax
import numpy as np

D_MODEL = 1024
BATCH = 4
SEQ = 4096
DEPTH = 1
DEC_BATCH = 128
DEC_SEQ = 4
PAST_LEN = 8192
PAGE_SIZE = 128

N_MEM = 256
EPS = 1e-6
Q_BLOCK = 128
D_FF = 2816
MLA_HEADS = 8
MLA_Q_LORA = 384
MLA_KV_LORA = 256
MLA_NOPE = 64
MLA_ROPE = 32
MLA_V = 64
MLA_SCALE = (MLA_NOPE + MLA_ROPE) ** -0.5
ROPE_BASE = 10000.0
NSA_HEADS = 8
NSA_KV_HEADS = 2
NSA_HPG = NSA_HEADS // NSA_KV_HEADS
NSA_DH = 64
NSA_SCALE = NSA_DH ** -0.5
CMP_BLOCK = 32
SEL_BLOCK = 64
SEL_TOPK = 16
WINDOW = 512
FORCE_BONUS = 4.0 * NSA_HPG
X_HEADS = 4
X_DH = 128
X_SCALE = X_DH ** -0.5
W_CQ = MLA_Q_LORA
W_CKV = MLA_KV_LORA + MLA_ROPE
W_NQ = NSA_HEADS * NSA_DH
W_NKV = 3 * NSA_KV_HEADS * 2 * NSA_DH
W_NG = NSA_HEADS * 3
W_MG = 2 * D_MODEL
IN_COLS = W_CQ + W_CKV + W_NQ + W_NKV + W_NG + W_MG
IN_OFFSETS = (W_CQ, W_CQ + W_CKV, W_CQ + W_CKV + W_NQ, W_CQ + W_CKV + W_NQ + W_NKV, W_CQ + W_CKV + W_NQ + W_NKV + W_NG)

kernel_name = 'hybrid_mla_nsa_macaron_step'


def rmsnorm(x, g):
    xf = x.astype(jnp.float32)
    y = xf * lax.rsqrt(jnp.mean(xf * xf, axis=-1, keepdims=True) + EPS)
    return (y * g.astype(jnp.float32)).astype(x.dtype)


def masked_softmax(s, mask):
    s = jnp.where(mask, s, -jnp.inf)
    m = jnp.max(s, axis=-1, keepdims=True)
    m = jnp.where(jnp.isfinite(m), m, 0.0)
    e = jnp.where(mask, jnp.exp(s - m), 0.0)
    return e / jnp.maximum(jnp.sum(e, axis=-1, keepdims=True), 1e-30)


def alibi_slopes(n):
    return 2.0 ** (-8.0 * jnp.arange(1, n + 1, dtype=jnp.float32) / n)


def rope(x, pos):
    half = x.shape[-1] // 2
    inv = ROPE_BASE ** (-jnp.arange(half, dtype=jnp.float32) / half)
    ang = pos.astype(jnp.float32)[:, None] * inv[None, :]
    cos, sin = jnp.cos(ang), jnp.sin(ang)
    if x.ndim == 4:
        cos, sin = cos[:, None, :], sin[:, None, :]
    xf = x.astype(jnp.float32)
    x1, x2 = xf[..., :half], xf[..., half:]
    return jnp.concatenate([x1 * cos - x2 * sin, x2 * cos + x1 * sin], axis=-1).astype(x.dtype)


def ffn_half(x, g, wg, wu, wd):
    h = rmsnorm(x, g)
    return x + 0.5 * ((jax.nn.silu(h @ wg) * (h @ wu)) @ wd)


def mixer_project(n, pos, w_in, q_norm, w_uq, kv_norm):
    B, S, _ = n.shape
    cq, ckr, qn, kvn, gn, mg = jnp.split(n @ w_in, IN_OFFSETS, axis=-1)
    q = jnp.einsum('bsc,chd->bshd', rmsnorm(cq, q_norm), w_uq)
    q_nope, q_rope = q[..., :MLA_NOPE], rope(q[..., MLA_NOPE:], pos)
    c_kv = rmsnorm(ckr[..., :MLA_KV_LORA], kv_norm)
    k_rope = rope(ckr[..., MLA_KV_LORA:], pos)
    q_nsa = qn.reshape(B, S, NSA_KV_HEADS, NSA_HPG, NSA_DH)
    kv3 = kvn.reshape(B, S, 3, NSA_KV_HEADS, 2, NSA_DH)
    gates = jax.nn.sigmoid(gn).reshape(B, S, NSA_KV_HEADS, NSA_HPG, 3)
    merge = jax.nn.sigmoid(mg).reshape(B, S, 2, D_MODEL)
    return q_nope, q_rope, c_kv, k_rope, q_nsa, kv3[:, :, 0], kv3[:, :, 1], kv3[:, :, 2], gates, merge


def merge_branches(o_a, o_b, merge, w_a, w_b, w_o):
    return (merge[:, :, 0] * (o_a @ w_a) + merge[:, :, 1] * (o_b @ w_b)) @ w_o


def mla_prompt(q_nope, q_rope, c_kv, k_rope, w_uk, w_uv):
    B, S = q_nope.shape[:2]
    k_nope = jnp.einsum('bsc,chd->bshd', c_kv, w_uk)
    v = jnp.einsum('bsc,chd->bshd', c_kv, w_uv)
    nb = S // Q_BLOCK
    qn = q_nope.reshape(B, nb, Q_BLOCK, MLA_HEADS, MLA_NOPE).swapaxes(0, 1)
    qr = q_rope.reshape(B, nb, Q_BLOCK, MLA_HEADS, MLA_ROPE).swapaxes(0, 1)
    kpos = jnp.arange(S)

    def block(args):
        qn_b, qr_b, i = args
        s = (jnp.einsum('bqhd,bkhd->bhqk', qn_b, k_nope)
             + jnp.einsum('bqhr,bkr->bhqk', qr_b, k_rope)).astype(jnp.float32) * MLA_SCALE
        qpos = i * Q_BLOCK + jnp.arange(Q_BLOCK)
        p = masked_softmax(s, kpos[None, :] <= qpos[:, None])
        return jnp.einsum('bhqk,bkhd->bqhd', p.astype(v.dtype), v)

    o = lax.map(block, (qn, qr, jnp.arange(nb)))
    return o.swapaxes(0, 1).reshape(B, S, MLA_HEADS * MLA_V)


def mla_sample(q_nope, q_rope, c_new, kr_new, ckv_pool, kr_pool, page_table, w_uk, w_uv):
    DB, Q = q_nope.shape[:2]
    c_past = ckv_pool[page_table].reshape(DB, -1, MLA_KV_LORA)
    kr_past = kr_pool[page_table].reshape(DB, -1, MLA_ROPE)
    P = c_past.shape[1]
    q_abs = jnp.einsum('bqhd,chd->bqhc', q_nope, w_uk)
    s = jnp.concatenate([
        jnp.einsum('bqhc,bsc->bhqs', q_abs, c_past) + jnp.einsum('bqhr,bsr->bhqs', q_rope, kr_past),
        jnp.einsum('bqhc,bsc->bhqs', q_abs, c_new) + jnp.einsum('bqhr,bsr->bhqs', q_rope, kr_new)],
        axis=-1).astype(jnp.float32) * MLA_SCALE
    mask = jnp.concatenate([jnp.ones((Q, P), bool), jnp.tril(jnp.ones((Q, Q), bool))], axis=-1)
    p = masked_softmax(s, mask).astype(c_new.dtype)
    o_lat = (jnp.einsum('bhqs,bsc->bqhc', p[..., :P], c_past)
             + jnp.einsum('bhqs,bsc->bqhc', p[..., P:], c_new))
    o = jnp.einsum('bqhc,chd->bqhd', o_lat, w_uv)
    return o.reshape(DB, Q, MLA_HEADS * MLA_V)


def compress_kv(rows, phi_k, phi_v):
    B, L = rows.shape[:2]
    m = rows.reshape(B, L // CMP_BLOCK, CMP_BLOCK, NSA_KV_HEADS, 2, NSA_DH).mean(axis=2)
    ck = jnp.einsum('bngd,gde->bnge', m[..., 0, :], phi_k)
    cv = jnp.einsum('bngd,gde->bnge', m[..., 1, :], phi_v)
    return ck, cv


def nsa_cmp_sel(q, qpos, ck, cv, n_sel, gather, slopes):
    B, Q = q.shape[:2]
    nc = ck.shape[1]
    cpos = jnp.arange(nc) * CMP_BLOCK + (CMP_BLOCK - 1)
    cdist = (qpos[:, None] - cpos[None, :]).astype(jnp.float32)
    s = (jnp.einsum('bqghd,bngd->bghqn', q, ck).astype(jnp.float32) * NSA_SCALE
         - slopes[None, :, :, None, None] * cdist)
    p = masked_softmax(s, cpos[None, :] <= qpos[:, None])
    o_cmp = jnp.einsum('bghqn,bngd->bqghd', p.astype(cv.dtype), cv)
    ratio = SEL_BLOCK // CMP_BLOCK
    imp = jnp.pad(p.sum(axis=2), ((0, 0), (0, 0), (0, 0), (0, n_sel * ratio - nc)))
    imp = imp.reshape(B, NSA_KV_HEADS, Q, n_sel, ratio).sum(axis=-1)
    blk = jnp.arange(n_sel)
    valid = (blk * SEL_BLOCK)[None, :] <= qpos[:, None]
    forced = (blk[None, :] == 0) | (blk[None, :] == (qpos // SEL_BLOCK)[:, None])
    score = jnp.where(valid, imp + jnp.where(forced, FORCE_BONUS, 0.0), -1.0)
    _, idx = lax.top_k(score, min(SEL_TOPK, n_sel))
    ok = jnp.take_along_axis(jnp.broadcast_to(valid, score.shape), idx, axis=-1)
    kpos = idx[..., None] * SEL_BLOCK + jnp.arange(SEL_BLOCK)
    ks, vs = gather(kpos)
    n = kpos.shape[3] * SEL_BLOCK
    sdist = (qpos[None, None, :, None, None] - kpos).astype(jnp.float32)
    ss = (jnp.einsum('bqghd,bgqkld->bghqkl', q, ks).astype(jnp.float32) * NSA_SCALE
          - slopes[None, :, :, None, None, None] * sdist[:, :, None])
    smask = ((kpos <= qpos[None, None, :, None, None]) & ok[..., None]).reshape(B, NSA_KV_HEADS, 1, Q, n)
    ps = masked_softmax(ss.reshape(B, NSA_KV_HEADS, NSA_HPG, Q, n), smask)
    o_sel = jnp.einsum('bghqn,bgqnd->bqghd', ps.astype(vs.dtype), vs.reshape(B, NSA_KV_HEADS, Q, n, NSA_DH))
    return o_cmp, o_sel


def nsa_window_prompt(q, kv, slopes):
    B, S = q.shape[:2]
    nb, npv = S // Q_BLOCK, WINDOW // Q_BLOCK
    kb_len = (npv + 1) * Q_BLOCK
    kvb = jnp.pad(kv, ((0, 0), (npv * Q_BLOCK, 0), (0, 0), (0, 0), (0, 0)))
    kvb = kvb.reshape(B, nb + npv, Q_BLOCK, NSA_KV_HEADS, 2, NSA_DH)
    band = jnp.concatenate([kvb[:, j:j + nb] for j in range(npv + 1)], axis=2)
    qb = q.reshape(B, nb, Q_BLOCK, NSA_KV_HEADS, NSA_HPG, NSA_DH)
    start = jnp.arange(nb)[:, None] * Q_BLOCK
    qpos = start + jnp.arange(Q_BLOCK)[None, :]
    kpos = start - npv * Q_BLOCK + jnp.arange(kb_len)[None, :]
    dist = qpos[:, :, None] - kpos[:, None, :]
    mask = (dist >= 0) & (dist <= WINDOW) & (kpos[:, None, :] >= 0)
    s = (jnp.einsum('bnqghd,bnkgd->bnghqk', qb, band[..., 0, :]).astype(jnp.float32) * NSA_SCALE
         - slopes[None, None, :, :, None, None] * dist[None, :, None, None].astype(jnp.float32))
    p = masked_softmax(s, mask[None, :, None, None])
    o = jnp.einsum('bnghqk,bnkgd->bnqghd', p.astype(kv.dtype), band[..., 1, :])
    return o.reshape(B, S, NSA_KV_HEADS, NSA_HPG, NSA_DH)


def nsa_window_attend(q, qpos, kv, kpos, slopes):
    dist = qpos[:, None] - kpos[None, :]
    s = (jnp.einsum('bqghd,bkgd->bghqk', q, kv[..., 0, :]).astype(jnp.float32) * NSA_SCALE
         - slopes[None, :, :, None, None] * dist.astype(jnp.float32))
    p = masked_softmax(s, (dist >= 0) & (dist <= WINDOW))
    return jnp.einsum('bghqk,bkgd->bqghd', p.astype(kv.dtype), kv[..., 1, :])


def nsa_gate(gates, o_cmp, o_sel, o_win):
    o = gates[..., 0:1] * o_cmp + gates[..., 1:2] * o_sel + gates[..., 2:3] * o_win
    B, S = o.shape[:2]
    return o.reshape(B, S, NSA_HEADS * NSA_DH)


def nsa_prompt(q, cmp_kv, sel_kv, win_kv, gates, phi_k, phi_v, slopes):
    B, S = q.shape[:2]
    ck, cv = compress_kv(cmp_kv, phi_k, phi_v)
    bi = jnp.arange(B)[:, None, None, None, None]
    gi = jnp.arange(NSA_KV_HEADS)[None, :, None, None, None]

    def gather(kpos):
        kv = sel_kv[bi, kpos, gi]
        return kv[..., 0, :], kv[..., 1, :]

    nb = S // Q_BLOCK
    qb = q.reshape(B, nb, Q_BLOCK, NSA_KV_HEADS, NSA_HPG, NSA_DH).swapaxes(0, 1)

    def block(args):
        q_b, i = args
        return nsa_cmp_sel(q_b, i * Q_BLOCK + jnp.arange(Q_BLOCK), ck, cv, S // SEL_BLOCK, gather, slopes)

    o_cmp, o_sel = lax.map(block, (qb, jnp.arange(nb)))
    o_cmp = o_cmp.swapaxes(0, 1).reshape(B, S, NSA_KV_HEADS, NSA_HPG, NSA_DH)
    o_sel = o_sel.swapaxes(0, 1).reshape(B, S, NSA_KV_HEADS, NSA_HPG, NSA_DH)
    o_win = nsa_window_prompt(q, win_kv, slopes)
    return nsa_gate(gates, o_cmp, o_sel, o_win)


def nsa_sample(q, cmp_new, sel_new, win_new, gates, cmp_pool, sel_pool, win_state, page_table, phi_k, phi_v, slopes):
    DB, Q = q.shape[:2]
    n_pages = page_table.shape[1]
    past = n_pages * PAGE_SIZE
    past_cmp = cmp_pool[page_table].reshape(DB, past, NSA_KV_HEADS, 2, NSA_DH)
    ck0, cv0 = compress_kv(past_cmp, phi_k, phi_v)
    ck1, cv1 = compress_kv(cmp_new[:, :(Q // CMP_BLOCK) * CMP_BLOCK], phi_k, phi_v)
    ck = jnp.concatenate([ck0, ck1], axis=1)
    cv = jnp.concatenate([cv0, cv1], axis=1)
    qpos = past + jnp.arange(Q)
    n_sel = -(-(past + Q) // SEL_BLOCK)
    bi = jnp.arange(DB)[:, None, None, None, None]
    gi = jnp.arange(NSA_KV_HEADS)[None, :, None, None, None]

    def gather(kpos):
        page = page_table[bi, jnp.minimum(kpos // PAGE_SIZE, n_pages - 1)]
        from_pool = sel_pool[page, kpos % PAGE_SIZE, gi]
        from_new = sel_new[bi, jnp.clip(kpos - past, 0, Q - 1), gi]
        kv = jnp.where((kpos < past)[..., None, None], from_pool, from_new)
        return kv[..., 0, :], kv[..., 1, :]

    o_cmp, o_sel = nsa_cmp_sel(q, qpos, ck, cv, n_sel, gather, slopes)
    wbuf = win_state.shape[1]
    win_all = jnp.concatenate([win_state, win_new], axis=1)
    o_win = nsa_window_attend(q, qpos, win_all, past - wbuf + jnp.arange(wbuf + Q), slopes)
    return nsa_gate(gates, o_cmp, o_sel, o_win), win_all[:, Q:]


def mem_kv(mem, g, w_kv):
    return jnp.einsum('bmd,dkhe->bmkhe', rmsnorm(mem, g), w_kv)


def xattn_add(x, kv, g, w_q, w_o):
    B, S, _ = x.shape
    q = (rmsnorm(x, g) @ w_q).reshape(B, S, X_HEADS, X_DH)
    s = jnp.einsum('bqhe,bmhe->bhqm', q, kv[:, :, 0]).astype(jnp.float32) * X_SCALE
    p = jax.nn.softmax(s, axis=-1).astype(kv.dtype)
    o = jnp.einsum('bhqm,bmhe->bqhe', p, kv[:, :, 1]).reshape(B, S, X_HEADS * X_DH)
    return x + o @ w_o


def setup_inputs(seed: int = 0) -> dict:
    key = jax.random.key(seed)
    ks = iter(jax.random.split(key, 64))
    f32 = jnp.float32

    def nrm(shape, fan_in):
        return jax.random.normal(next(ks), shape, f32) * (fan_in ** -0.5)

    def gain(n):
        return 1.0 + 0.01 * jax.random.normal(next(ks), (DEPTH, n), f32)

    def rnd(shape):
        return jax.random.normal(next(ks), shape, f32)

    n_pages = PAST_LEN // PAGE_SIZE
    n_used = DEC_BATCH * n_pages
    n_pool = n_used + max(1, n_used // 4)
    wbuf = min(WINDOW, PAST_LEN)
    G, DH = NSA_KV_HEADS, NSA_DH
    x_prompt = rnd((BATCH, SEQ, D_MODEL))
    x_sample = rnd((DEC_BATCH, DEC_SEQ, D_MODEL))
    mem_prompt = rnd((BATCH, N_MEM, D_MODEL))
    cache_mla_ckv = rnd((DEPTH, n_pool, PAGE_SIZE, MLA_KV_LORA))
    cache_mla_krope = rnd((DEPTH, n_pool, PAGE_SIZE, MLA_ROPE))
    cache_nsa_cmp_kv = rnd((DEPTH, n_pool, PAGE_SIZE, G, 2, DH))
    cache_nsa_sel_kv = rnd((DEPTH, n_pool, PAGE_SIZE, G, 2, DH))
    state_nsa_win_kv = rnd((DEPTH, DEC_BATCH, wbuf, G, 2, DH))
    cache_xattn_kv = rnd((DEPTH, DEC_BATCH, N_MEM, 2, X_HEADS, X_DH))
    page_table = jax.random.permutation(next(ks), n_pool)[:n_used].reshape(DEC_BATCH, n_pages).astype(jnp.int32)
    return {
        'x_prompt': x_prompt,
        'x_sample': x_sample,
        'mem_prompt': mem_prompt,
        'cache_mla_ckv': cache_mla_ckv,
        'cache_mla_krope': cache_mla_krope,
        'cache_nsa_cmp_kv': cache_nsa_cmp_kv,
        'cache_nsa_sel_kv': cache_nsa_sel_kv,
        'state_nsa_win_kv': state_nsa_win_kv,
        'cache_xattn_kv': cache_xattn_kv,
        'page_table': page_table,
        'ffn1_norm': gain(D_MODEL),
        'ffn1_w_gate': nrm((DEPTH, D_MODEL, D_FF), D_MODEL),
        'ffn1_w_up': nrm((DEPTH, D_MODEL, D_FF), D_MODEL),
        'ffn1_w_down': nrm((DEPTH, D_FF, D_MODEL), D_FF),
        'mix_norm': gain(D_MODEL),
        'w_in': nrm((DEPTH, D_MODEL, IN_COLS), D_MODEL),
        'mla_q_norm': gain(MLA_Q_LORA),
        'mla_w_uq': nrm((DEPTH, MLA_Q_LORA, MLA_HEADS, MLA_NOPE + MLA_ROPE), MLA_Q_LORA),
        'mla_kv_norm': gain(MLA_KV_LORA),
        'mla_w_uk': nrm((DEPTH, MLA_KV_LORA, MLA_HEADS, MLA_NOPE), MLA_KV_LORA),
        'mla_w_uv': nrm((DEPTH, MLA_KV_LORA, MLA_HEADS, MLA_V), MLA_KV_LORA),
        'nsa_phi_k': nrm((DEPTH, G, DH, DH), DH),
        'nsa_phi_v': nrm((DEPTH, G, DH, DH), DH),
        'w_br_mla': nrm((DEPTH, MLA_HEADS * MLA_V, D_MODEL), MLA_HEADS * MLA_V),
        'w_br_nsa': nrm((DEPTH, NSA_HEADS * NSA_DH, D_MODEL), NSA_HEADS * NSA_DH),
        'w_out': nrm((DEPTH, D_MODEL, D_MODEL), D_MODEL),
        'xattn_norm': gain(D_MODEL),
        'xattn_mem_norm': gain(D_MODEL),
        'xattn_w_q': nrm((DEPTH, D_MODEL, X_HEADS * X_DH), D_MODEL),
        'xattn_w_kv': nrm((DEPTH, D_MODEL, 2, X_HEADS, X_DH), D_MODEL),
        'xattn_w_o': nrm((DEPTH, X_HEADS * X_DH, D_MODEL), X_HEADS * X_DH),
        'ffn2_norm': gain(D_MODEL),
        'ffn2_w_gate': nrm((DEPTH, D_MODEL, D_FF), D_MODEL),
        'ffn2_w_up': nrm((DEPTH, D_MODEL, D_FF), D_MODEL),
        'ffn2_w_down': nrm((DEPTH, D_FF, D_MODEL), D_FF),
        'final_norm': 1.0 + 0.01 * jax.random.normal(next(ks), (D_MODEL,), f32),
    }


def reference(x_prompt, x_sample, mem_prompt, cache_mla_ckv, cache_mla_krope, cache_nsa_cmp_kv,
              cache_nsa_sel_kv, state_nsa_win_kv, cache_xattn_kv, page_table,
              ffn1_norm, ffn1_w_gate, ffn1_w_up, ffn1_w_down, mix_norm, w_in,
              mla_q_norm, mla_w_uq, mla_kv_norm, mla_w_uk, mla_w_uv, nsa_phi_k, nsa_phi_v,
              w_br_mla, w_br_nsa, w_out, xattn_norm, xattn_mem_norm, xattn_w_q, xattn_w_kv, xattn_w_o,
              ffn2_norm, ffn2_w_gate, ffn2_w_up, ffn2_w_down, final_norm):
    slopes = alibi_slopes(NSA_HEADS).reshape(NSA_KV_HEADS, NSA_HPG)
    S = x_prompt.shape[1]
    Q = x_sample.shape[1]
    past = page_table.shape[1] * PAGE_SIZE
    pos_p = jnp.arange(S)
    pos_s = past + jnp.arange(Q)
    hp, hs = x_prompt, x_sample
    p_ckv, p_kr, p_cmp, p_sel, p_win, p_x = [], [], [], [], [], []
    s_ckv, s_kr, s_cmp, s_sel, s_win = [], [], [], [], []
    for l in range(DEPTH):
        hp = ffn_half(hp, ffn1_norm[l], ffn1_w_gate[l], ffn1_w_up[l], ffn1_w_down[l])
        qn, qr, ckv, kr, qs, cmp_kv, sel_kv, win_kv, gt, mg = mixer_project(
            rmsnorm(hp, mix_norm[l]), pos_p, w_in[l], mla_q_norm[l], mla_w_uq[l], mla_kv_norm[l])
        o_a = mla_prompt(qn, qr, ckv, kr, mla_w_uk[l], mla_w_uv[l])
        o_b = nsa_prompt(qs, cmp_kv, sel_kv, win_kv, gt, nsa_phi_k[l], nsa_phi_v[l], slopes)
        hp = hp + merge_branches(o_a, o_b, mg, w_br_mla[l], w_br_nsa[l], w_out[l])
        kv_mem = mem_kv(mem_prompt, xattn_mem_norm[l], xattn_w_kv[l])
        hp = xattn_add(hp, kv_mem, xattn_norm[l], xattn_w_q[l], xattn_w_o[l])
        hp = ffn_half(hp, ffn2_norm[l], ffn2_w_gate[l], ffn2_w_up[l], ffn2_w_down[l])
        p_ckv.append(ckv)
        p_kr.append(kr)
        p_cmp.append(cmp_kv)
        p_sel.append(sel_kv)
        p_win.append(win_kv[:, S - min(WINDOW, S):])
        p_x.append(kv_mem)
        hs = ffn_half(hs, ffn1_norm[l], ffn1_w_gate[l], ffn1_w_up[l], ffn1_w_down[l])
        qn, qr, ckv, kr, qs, cmp_kv, sel_kv, win_kv, gt, mg = mixer_project(
            rmsnorm(hs, mix_norm[l]), pos_s, w_in[l], mla_q_norm[l], mla_w_uq[l], mla_kv_norm[l])
        o_a = mla_sample(qn, qr, ckv, kr, cache_mla_ckv[l], cache_mla_krope[l], page_table, mla_w_uk[l], mla_w_uv[l])
        o_b, win_upd = nsa_sample(qs, cmp_kv, sel_kv, win_kv, gt, cache_nsa_cmp_kv[l], cache_nsa_sel_kv[l],
                                  state_nsa_win_kv[l], page_table, nsa_phi_k[l], nsa_phi_v[l], slopes)
        hs = hs + merge_branches(o_a, o_b, mg, w_br_mla[l], w_br_nsa[l], w_out[l])
        hs = xattn_add(hs, cache_xattn_kv[l], xattn_norm[l], xattn_w_q[l], xattn_w_o[l])
        hs = ffn_half(hs, ffn2_norm[l], ffn2_w_gate[l], ffn2_w_up[l], ffn2_w_down[l])
        s_ckv.append(ckv)
        s_kr.append(kr)
        s_cmp.append(cmp_kv)
        s_sel.append(sel_kv)
        s_win.append(win_upd)
    y_prompt = rmsnorm(hp, final_norm)
    y_sample = rmsnorm(hs, final_norm)
    return (y_prompt, y_sample,
            jnp.stack(p_ckv), jnp.stack(p_kr), jnp.stack(p_cmp), jnp.stack(p_sel), jnp.stack(p_win), jnp.stack(p_x),
            jnp.stack(s_ckv), jnp.stack(s_kr), jnp.stack(s_cmp), jnp.stack(s_sel), jnp.stack(s_win))
```

```python
import functools

import jax
import jax.numpy as jnp
from jax import lax
from jax.experimental import pallas as pl
from jax.experimental.pallas import tpu as pltpu

F32 = jnp.float32
BF16 = jnp.bfloat16
I32 = jnp.int32

D_MODEL = 1024
EPS = 1e-6
PAGE = 128
MLA_HEADS = 8
MLA_Q_LORA = 384
MLA_KV_LORA = 256
MLA_NOPE = 64
MLA_ROPE = 32
MLA_V = 64
MLA_SCALE = (MLA_NOPE + MLA_ROPE) ** -0.5
ROPE_BASE = 10000.0
NSA_HEADS = 8
NSA_G = 2
NSA_HPG = 4
NSA_DH = 64
NSA_SCALE = NSA_DH ** -0.5
CMP_BLOCK = 32
SEL_BLOCK = 64
SEL_TOPK = 16
WINDOW = 512
FORCE_BONUS = 4.0 * NSA_HPG
X_HEADS = 4
X_DH = 128
X_SCALE = X_DH ** -0.5
LANES = 128
NEG_INF = float("-inf")
VMEM_LIMIT = 48 * 1024 * 1024

C_CQ = 0
C_CKV = 384
C_KR = 640
C_QN = 768
C_KVN = 1280
C_GT = 2048
C_MG = 2304
C_END = 4352


def _cparams(n_axes):
    return pltpu.CompilerParams(dimension_semantics=("arbitrary",) * n_axes, vmem_limit_bytes=VMEM_LIMIT)


def _dot(a, b):
    return jnp.dot(a, b, preferred_element_type=F32)


def _dot_nt(a, b):
    return lax.dot_general(a, b, (((1,), (1,)), ((), ())), preferred_element_type=F32)


def _rms(x, g):
    ms = jnp.mean(x * x, axis=-1, keepdims=True)
    return x * lax.rsqrt(ms + EPS) * g


def _iota(shape, dim):
    return lax.broadcasted_iota(I32, shape, dim)


def _pow2_neg(e):
    return lax.bitcast_convert_type((127 - e) << 23, F32)


def _softmax_init(m_ref, l_ref, acc_ref):
    m_ref[...] = jnp.full(m_ref.shape, NEG_INF, F32)
    l_ref[...] = jnp.zeros(l_ref.shape, F32)
    acc_ref[...] = jnp.zeros(acc_ref.shape, F32)


def _softmax_step(s, v, m_ref, l_ref, acc_ref):
    m_prev = m_ref[...]
    m_new = jnp.maximum(m_prev, jnp.max(s, axis=-1, keepdims=True))
    m_safe = jnp.where(m_new == NEG_INF, 0.0, m_new)
    alpha = jnp.exp(m_prev - m_safe)
    p = jnp.exp(s - m_safe)
    l_ref[...] = alpha * l_ref[...] + jnp.sum(p, axis=-1, keepdims=True)
    acc_ref[...] = alpha * acc_ref[...] + _dot(p.astype(BF16), v)
    m_ref[...] = m_new


def _softmax_finish(l_ref, acc_ref):
    return acc_ref[...] / jnp.maximum(l_ref[...], 1e-30)


def _ffn_kernel(x_ref, g_ref, wg_ref, wu_ref, wd_ref, fg_ref, o_ref, h_ref, acc_ref, *, final):
    j = pl.program_id(1)

    @pl.when(j == 0)
    def _():
        h_ref[...] = _rms(x_ref[...], g_ref[...]).astype(BF16)
        acc_ref[...] = jnp.zeros(acc_ref.shape, F32)

    h = h_ref[...]
    a = _dot(h, wg_ref[...])
    u = _dot(h, wu_ref[...])
    act = (a * jax.nn.sigmoid(a)) * u
    acc_ref[...] += _dot(act.astype(BF16), wd_ref[...])

    @pl.when(j == pl.num_programs(1) - 1)
    def _():
        y = x_ref[...] + 0.5 * acc_ref[...]
        if final:
            y = _rms(y, fg_ref[...])
        o_ref[...] = y


def _ffn(x, g, wg, wu, wd, fg, *, final):
    t, d = x.shape
    ff = wg.shape[1]
    tm = min(t, 1024)
    tf = 256
    return pl.pallas_call(
        functools.partial(_ffn_kernel, final=final),
        grid=(t // tm, ff // tf),
        in_specs=[
            pl.BlockSpec((tm, d), lambda i, j: (i, 0)),
            pl.BlockSpec((1, d), lambda i, j: (0, 0)),
            pl.BlockSpec((d, tf), lambda i, j: (0, j)),
            pl.BlockSpec((d, tf), lambda i, j: (0, j)),
            pl.BlockSpec((tf, d), lambda i, j: (j, 0)),
            pl.BlockSpec((1, d), lambda i, j: (0, 0)),
        ],
        out_specs=pl.BlockSpec((tm, d), lambda i, j: (i, 0)),
        out_shape=jax.ShapeDtypeStruct((t, d), F32),
        scratch_shapes=[pltpu.VMEM((tm, d), BF16), pltpu.VMEM((tm, d), F32)],
        compiler_params=_cparams(2),
    )(x, g, wg, wu, wd, fg)


def _proj_kernel(x_ref, g_ref, win_ref, qn_ref, wuq_ref, kvn_ref, wuk_ref, wuv_ref, rc_ref, rs1_ref, rs2_ref,
                 qmla_ref, kmla_ref, vmla_ref, ckv_ref, kr_ref, qnsa_ref, cmp_ref, sel_ref, win_o_ref,
                 selb_ref, winb_ref, gates_ref, merge_ref):
    n = _rms(x_ref[...], g_ref[...]).astype(BF16)

    def seg(a, b):
        return _dot(n, win_ref[:, a:b])

    rc, rs1, rs2 = rc_ref[...], rs1_ref[...], rs2_ref[...]

    def rope(blk):
        return blk * rc + pltpu.roll(blk, 112, 1) * rs1 + pltpu.roll(blk, 16, 1) * rs2

    cq = _rms(seg(C_CQ, C_CKV), qn_ref[...]).astype(BF16)
    q = _dot(cq, wuq_ref[...]) * MLA_SCALE
    for h in range(MLA_HEADS):
        qmla_ref[:, h * LANES:(h + 1) * LANES] = rope(q[:, h * LANES:(h + 1) * LANES]).astype(BF16)

    c_kv = _rms(seg(C_CKV, C_KR), kvn_ref[...])
    ckv_ref[...] = c_kv
    krb = rope(seg(C_KR, C_QN))
    kr_ref[...] = krb
    cb = c_kv.astype(BF16)
    k = _dot(cb, wuk_ref[...])
    for h in range(MLA_HEADS):
        kmla_ref[:, h * LANES:(h + 1) * LANES] = (k[:, h * LANES:(h + 1) * LANES] + krb).astype(BF16)
    vmla_ref[...] = _dot(cb, wuv_ref[...]).astype(BF16)

    qnsa_ref[...] = (seg(C_QN, C_KVN) * NSA_SCALE).astype(BF16)
    cmp_ref[...] = seg(C_KVN, C_KVN + 256)
    sel = seg(C_KVN + 256, C_KVN + 512)
    sel_ref[...] = sel
    selb_ref[...] = sel.astype(BF16)
    win = seg(C_KVN + 512, C_GT)
    win_o_ref[...] = win
    winb_ref[...] = win.astype(BF16)

    gates_ref[...] = jax.nn.sigmoid(seg(C_GT, C_MG))
    merge_ref[...] = jax.nn.sigmoid(seg(C_MG, C_END))


def _mixer_project(x, g, w_in_p, q_norm, w_uq_p, kv_norm, w_uk_p, w_uv_p, rc, rs1, rs2):
    t, d = x.shape
    tm = min(t, 256)
    ntab = rc.shape[0] // tm
    full = lambda shape: pl.BlockSpec(shape, lambda i: (0, 0))
    row = lambda w: pl.BlockSpec((tm, w), lambda i: (i, 0))
    tab = pl.BlockSpec((tm, LANES), lambda i: (i % ntab, 0))
    widths = [(1024, BF16), (1024, BF16), (512, BF16), (256, F32), (128, F32), (512, BF16), (256, F32), (256, F32),
              (256, F32), (256, BF16), (256, BF16), (256, F32), (2048, F32)]
    return pl.pallas_call(
        _proj_kernel,
        grid=(t // tm,),
        in_specs=[row(d), full((1, d)), full(w_in_p.shape), full((1, MLA_Q_LORA)), full(w_uq_p.shape),
                  full((1, MLA_KV_LORA)), full(w_uk_p.shape), full(w_uv_p.shape), tab, tab, tab],
        out_specs=[row(w) for w, _ in widths],
        out_shape=[jax.ShapeDtypeStruct((t, w), dt) for w, dt in widths],
        compiler_params=_cparams(1),
    )(x, g, w_in_p, q_norm, w_uq_p, kv_norm, w_uk_p, w_uv_p, rc, rs1, rs2)


def _norm_mm_kernel(x_ref, g_ref, w_ref, o_ref, *, scale):
    y = _dot(_rms(x_ref[...], g_ref[...]).astype(BF16), w_ref[...])
    if scale != 1.0:
        y = y * scale
    o_ref[...] = y.astype(o_ref.dtype)


def _norm_matmul(x, g, w, *, scale=1.0, out_dtype=F32):
    t, d = x.shape
    n = w.shape[1]
    tm = min(t, 512)
    return pl.pallas_call(
        functools.partial(_norm_mm_kernel, scale=scale),
        grid=(t // tm,),
        in_specs=[pl.BlockSpec((tm, d), lambda i: (i, 0)), pl.BlockSpec((1, d), lambda i: (0, 0)),
                  pl.BlockSpec((d, n), lambda i: (0, 0))],
        out_specs=pl.BlockSpec((tm, n), lambda i: (i, 0)),
        out_shape=jax.ShapeDtypeStruct((t, n), out_dtype),
        compiler_params=_cparams(1),
    )(x, g, w)


def _mm_kernel(a_ref, w_ref, o_ref):
    o_ref[...] = _dot(a_ref[...], w_ref[...]).astype(o_ref.dtype)


def _matmul(a, w, *, out_dtype=F32):
    t, k = a.shape
    n = w.shape[1]
    tm = min(t, 512)
    return pl.pallas_call(
        _mm_kernel,
        grid=(t // tm,),
        in_specs=[pl.BlockSpec((tm, k), lambda i: (i, 0)), pl.BlockSpec((k, n), lambda i: (0, 0))],
        out_specs=pl.BlockSpec((tm, n), lambda i: (i, 0)),
        out_shape=jax.ShapeDtypeStruct((t, n), out_dtype),
        compiler_params=_cparams(1),
    )(a, w)


def _mm_res_kernel(a_ref, w_ref, x_ref, o_ref):
    o_ref[...] = x_ref[...] + _dot(a_ref[...], w_ref[...])


def _matmul_residual(a, w, x):
    t, k = a.shape
    n = w.shape[1]
    tm = min(t, 512)
    return pl.pallas_call(
        _mm_res_kernel,
        grid=(t // tm,),
        in_specs=[pl.BlockSpec((tm, k), lambda i: (i, 0)), pl.BlockSpec((k, n), lambda i: (0, 0)),
                  pl.BlockSpec((tm, n), lambda i: (i, 0))],
        out_specs=pl.BlockSpec((tm, n), lambda i: (i, 0)),
        out_shape=jax.ShapeDtypeStruct((t, n), F32),
        compiler_params=_cparams(1),
    )(a, w, x)


def _merge_kernel(oa_ref, ob_ref, mg_ref, x_ref, wa_ref, wb_ref, wo_ref, o_ref):
    d = x_ref.shape[1]
    mix = mg_ref[:, :d] * _dot(oa_ref[...], wa_ref[...]) + mg_ref[:, d:] * _dot(ob_ref[...], wb_ref[...])
    o_ref[...] = x_ref[...] + _dot(mix.astype(BF16), wo_ref[...])


def _merge(o_a, o_b, mg, x, w_a, w_b, w_o):
    t, d = x.shape
    tm = min(t, 512)
    row = lambda w: pl.BlockSpec((tm, w), lambda i: (i, 0))
    full = lambda a: pl.BlockSpec(a.shape, lambda i: (0, 0))
    return pl.pallas_call(
        _merge_kernel,
        grid=(t // tm,),
        in_specs=[row(o_a.shape[1]), row(o_b.shape[1]), row(2 * d), row(d), full(w_a), full(w_b), full(w_o)],
        out_specs=row(d),
        out_shape=jax.ShapeDtypeStruct((t, d), F32),
        compiler_params=_cparams(1),
    )(o_a, o_b, mg, x, w_a, w_b, w_o)


def _mla_prompt_kernel(q_ref, k_ref, v_ref, o_ref, m_ref, l_ref, acc_ref, *, tq):
    i = pl.program_id(2)
    q = q_ref[0]
    rows = _iota((tq, 1), 0)
    cols = _iota((1, tq), 1)
    outs = []
    for hh in range(2):
        qh = q[:, hh * LANES:(hh + 1) * LANES]
        _softmax_init(m_ref, l_ref, acc_ref)

        def step(j, causal, hh=hh, qh=qh):
            start = pl.multiple_of(j * tq, tq)
            s = _dot_nt(qh, k_ref[0, pl.ds(start, tq), hh * LANES:(hh + 1) * LANES])
            if causal:
                s = jnp.where(cols <= rows, s, NEG_INF)
            _softmax_step(s, v_ref[0, pl.ds(start, tq), :], m_ref, l_ref, acc_ref)

        def body(j, c):
            step(j, False)
            return c

        lax.fori_loop(0, i, body, 0)
        step(i, True)
        outs.append(_softmax_finish(l_ref, acc_ref))
    lane = _iota((1, LANES), 1)
    o_ref[0] = jnp.where(lane < MLA_V, outs[0], outs[1]).astype(o_ref.dtype)


def _mla_prompt(q, k, v):
    b, s, _ = q.shape
    tq = min(s, 256)
    return pl.pallas_call(
        functools.partial(_mla_prompt_kernel, tq=tq),
        grid=(b, MLA_HEADS // 2, s // tq),
        in_specs=[pl.BlockSpec((1, tq, 2 * LANES), lambda bi, hp, i: (bi, i, hp)),
                  pl.BlockSpec((1, s, 2 * LANES), lambda bi, hp, i: (bi, 0, hp)),
                  pl.BlockSpec((1, s, LANES), lambda bi, hp, i: (bi, 0, hp))],
        out_specs=pl.BlockSpec((1, tq, LANES), lambda bi, hp, i: (bi, i, hp)),
        out_shape=jax.ShapeDtypeStruct((b, s, MLA_HEADS * MLA_V), BF16),
        scratch_shapes=[pltpu.VMEM((tq, 1), F32), pltpu.VMEM((tq, 1), F32), pltpu.VMEM((tq, LANES), F32)],
        compiler_params=_cparams(3),
    )(q, k, v)


def _compress_kernel(x_ref, phi_ref, o_ref):
    x = x_ref[0]
    n = x.shape[0] // CMP_BLOCK
    m = jnp.sum(x.reshape(n, CMP_BLOCK, x.shape[1]), axis=1) * (1.0 / CMP_BLOCK)
    o_ref[0] = _dot(m.astype(BF16), phi_ref[...])


def _compress_prompt(cmp_kv, phi):
    b, s, w = cmp_kv.shape
    return pl.pallas_call(
        _compress_kernel,
        grid=(b,),
        in_specs=[pl.BlockSpec((1, s, w), lambda i: (i, 0, 0)), pl.BlockSpec(phi.shape, lambda i: (0, 0))],
        out_specs=pl.BlockSpec((1, s // CMP_BLOCK, w), lambda i: (i, 0, 0)),
        out_shape=jax.ShapeDtypeStruct((b, s // CMP_BLOCK, w), F32),
        compiler_params=_cparams(1),
    )(cmp_kv, phi)


def _nsa_prompt_kernel(q_ref, ck_ref, cv_ref, ks_ref, vs_ref, kw_ref, vw_ref, gt_ref, o_ref,
                       m_ref, l_ref, acc_ref, *, tq, n_sel):
    g = pl.program_id(1)
    i = pl.program_id(2)
    r = NSA_HPG * tq
    shift = tq.bit_length() - 1
    q = q_ref[0, 0, 0]
    row = _iota((r, 1), 0)
    qpos = i * tq + (row & (tq - 1))
    slope = _pow2_neg(NSA_HPG * g + (row >> shift) + 1)

    nc = ck_ref.shape[2]
    half = nc // 2
    lane = _iota((1, nc), 1)
    cblk = jnp.where(lane < half, 2 * lane, 2 * (lane - half) + 1)
    cpos = cblk * CMP_BLOCK + (CMP_BLOCK - 1)
    cmask = cpos <= qpos
    s = _dot_nt(q, ck_ref[0, 0]) - slope * (qpos - cpos).astype(F32)
    s = jnp.where(cmask, s, NEG_INF)
    mx = jnp.max(s, axis=-1, keepdims=True)
    mx = jnp.where(mx == NEG_INF, 0.0, mx)
    e = jnp.where(cmask, jnp.exp(s - mx), 0.0)
    p = e / jnp.maximum(jnp.sum(e, axis=-1, keepdims=True), 1e-30)
    o_cmp = _dot(p.astype(BF16), cv_ref[0, 0])

    psum = p[0:tq] + p[tq:2 * tq] + p[2 * tq:3 * tq] + p[3 * tq:4 * tq]
    imp = psum + pltpu.roll(psum, half, 1)
    qp = i * tq + _iota((tq, 1), 0)
    forced = (lane == 0) | (lane == (qp >> 6))
    score = jnp.where(lane * SEL_BLOCK <= qp, imp + jnp.where(forced, FORCE_BONUS, 0.0), -1.0)
    st = jnp.transpose(score)[0:n_sel]
    jidx = _iota((n_sel, 1), 0)
    rank = jnp.zeros((n_sel, tq), F32)
    for ii in range(n_sel):
        ri = st[ii:ii + 1, :]
        first = jnp.where(jidx > ii, 1.0, 0.0)
        rank = rank + jnp.where(ri > st, 1.0, jnp.where(ri == st, first, 0.0))
    valid_t = (jidx * SEL_BLOCK) <= (i * tq + _iota((1, tq), 1))
    mt = jnp.where(valid_t, jnp.where(rank < float(min(SEL_TOPK, n_sel)), 1.0, 0.0), 0.0)
    if n_sel < LANES:
        mt = jnp.concatenate([mt, jnp.zeros((LANES - n_sel, tq), F32)], axis=0)
    msel = jnp.transpose(mt).astype(BF16)
    mst = jnp.concatenate([msel] * NSA_HPG, axis=0)

    _softmax_init(m_ref, l_ref, acc_ref)
    erow = _iota((LANES, 1), 0)

    def sel_step(j, causal):
        start = pl.multiple_of(j * tq, tq)
        kpos = j * tq + _iota((1, tq), 1)
        expand = jnp.where(erow == (kpos >> 6), 1.0, 0.0).astype(BF16)
        member = _dot(mst, expand)
        sc = _dot_nt(q, ks_ref[0, 0, pl.ds(start, tq), :]) - slope * (qpos - kpos).astype(F32)
        sc = jnp.where(member > 0.5, sc, NEG_INF)
        if causal:
            sc = jnp.where(kpos <= qpos, sc, NEG_INF)
        _softmax_step(sc, vs_ref[0, 0, pl.ds(start, tq), :], m_ref, l_ref, acc_ref)

    def sel_body(j, c):
        sel_step(j, False)
        return c

    lax.fori_loop(0, i, sel_body, 0)
    sel_step(i, True)
    o_sel = _softmax_finish(l_ref, acc_ref)

    _softmax_init(m_ref, l_ref, acc_ref)

    def win_body(j, c):
        start = pl.multiple_of(j * tq, tq)
        kpos = j * tq + _iota((1, tq), 1)
        dist = qpos - kpos
        sc = _dot_nt(q, kw_ref[0, 0, pl.ds(start, tq), :]) - slope * dist.astype(F32)
        sc = jnp.where(dist >= 0, jnp.where(dist <= WINDOW, sc, NEG_INF), NEG_INF)
        _softmax_step(sc, vw_ref[0, 0, pl.ds(start, tq), :], m_ref, l_ref, acc_ref)
        return c

    lax.fori_loop(jnp.maximum(i - WINDOW // tq, 0), i + 1, win_body, 0)
    o_win = _softmax_finish(l_ref, acc_ref)

    gt = gt_ref[0]
    for h in range(NSA_HPG):
        rs = slice(h * tq, (h + 1) * tq)
        o = (gt[:, 3 * h:3 * h + 1] * o_cmp[rs] + gt[:, 3 * h + 1:3 * h + 2] * o_sel[rs]
             + gt[:, 3 * h + 2:3 * h + 3] * o_win[rs])
        o_ref[0, 0, h] = o.astype(o_ref.dtype)


def _nsa_prompt(q_st, ck, cv, ks, vs, kw, vw, gates, *, tq):
    b, g, nq, r, dh = q_st.shape
    s = ks.shape[2]
    nc = ck.shape[2]
    kv = lambda n: pl.BlockSpec((1, 1, n, dh), lambda bi, gi, i: (bi, gi, 0, 0))
    return pl.pallas_call(
        functools.partial(_nsa_prompt_kernel, tq=tq, n_sel=s // SEL_BLOCK),
        grid=(b, g, nq),
        in_specs=[pl.BlockSpec((1, 1, 1, r, dh), lambda bi, gi, i: (bi, gi, i, 0, 0)),
                  kv(nc), kv(nc), kv(s), kv(s), kv(s), kv(s),
                  pl.BlockSpec((1, tq, LANES), lambda bi, gi, i: (bi, i, gi))],
        out_specs=pl.BlockSpec((1, 1, NSA_HPG, tq, dh), lambda bi, gi, i: (bi, gi, 0, i, 0)),
        out_shape=jax.ShapeDtypeStruct((b, g, NSA_HPG, s, dh), BF16),
        scratch_shapes=[pltpu.VMEM((r, 1), F32), pltpu.VMEM((r, 1), F32), pltpu.VMEM((r, dh), F32)],
        compiler_params=_cparams(3),
    )(q_st, ck, cv, ks, vs, kw, vw, gates)


def _xattn_prompt_kernel(q_ref, kv_ref, o_ref):
    hw = X_HEADS * X_DH
    for h in range(X_HEADS):
        cs = slice(h * X_DH, (h + 1) * X_DH)
        s = _dot_nt(q_ref[0, :, cs], kv_ref[0, :, cs])
        e = jnp.exp(s - jnp.max(s, axis=-1, keepdims=True))
        p = e / jnp.sum(e, axis=-1, keepdims=True)
        o_ref[0, :, cs] = _dot(p.astype(BF16), kv_ref[0, :, hw + h * X_DH:hw + (h + 1) * X_DH]).astype(o_ref.dtype)


def _xattn_prompt(q, kv):
    b, s, w = q.shape
    m = kv.shape[1]
    tq = min(s, 512)
    return pl.pallas_call(
        _xattn_prompt_kernel,
        grid=(b, s // tq),
        in_specs=[pl.BlockSpec((1, tq, w), lambda bi, i: (bi, i, 0)),
                  pl.BlockSpec((1, m, 2 * w), lambda bi, i: (bi, 0, 0))],
        out_specs=pl.BlockSpec((1, tq, w), lambda bi, i: (bi, i, 0)),
        out_shape=jax.ShapeDtypeStruct((b, s, w), BF16),
        compiler_params=_cparams(2),
    )(q, kv)


def _qabs_kernel(q_ref, w_ref, o_ref):
    o_ref[0] = _dot(q_ref[...], w_ref[0]).astype(o_ref.dtype)


def _mla_absorb_q(q_mla, w_abs):
    t = q_mla.shape[0]
    n = w_abs.shape[2]
    return pl.pallas_call(
        _qabs_kernel,
        grid=(MLA_HEADS,),
        in_specs=[pl.BlockSpec((t, LANES), lambda h: (0, h)), pl.BlockSpec((1, LANES, n), lambda h: (h, 0, 0))],
        out_specs=pl.BlockSpec((1, t, n), lambda h: (h, 0, 0)),
        out_shape=jax.ShapeDtypeStruct((MLA_HEADS, t, n), BF16),
        compiler_params=_cparams(1),
    )(q_mla, w_abs)


def _mla_sample_kernel(pt_ref, qa_ref, cn_ref, krn_ref, *refs, pg, nq):
    del pt_ref
    c_refs, kr_refs = refs[:pg], refs[pg:2 * pg]
    o_ref, m_ref, l_ref, acc_ref = refs[2 * pg:]
    jj = pl.program_id(1)
    qa = qa_ref[0]
    qc, qr = qa[:, :MLA_KV_LORA], qa[:, MLA_KV_LORA:MLA_KV_LORA + MLA_ROPE]

    @pl.when(jj == 0)
    def _():
        _softmax_init(m_ref, l_ref, acc_ref)

    c = jnp.concatenate([ref[0] for ref in c_refs], axis=0).astype(BF16)
    kr = jnp.concatenate([ref[0] for ref in kr_refs], axis=0).astype(BF16)
    _softmax_step(_dot_nt(qc, c) + _dot_nt(qr, kr), c, m_ref, l_ref, acc_ref)

    @pl.when(jj == pl.num_programs(1) - 1)
    def _():
        cn = cn_ref[0].astype(BF16)
        s = _dot_nt(qc, cn) + _dot_nt(qr, krn_ref[0].astype(BF16))
        qi = _iota((qa.shape[0], 1), 0) & (nq - 1)
        s = jnp.where(_iota((1, cn.shape[0]), 1) <= qi, s, NEG_INF)
        _softmax_step(s, cn, m_ref, l_ref, acc_ref)
        o_ref[0] = _softmax_finish(l_ref, acc_ref)


def _mla_sample(page_table, qa, c_new, kr_new, ckv_pool, kr_pool, *, nq):
    db, n_pages = page_table.shape
    pg = min(n_pages, 16)
    r = qa.shape[1]

    def page_spec(width, p):
        return pl.BlockSpec((1, PAGE, width), lambda b, j, pt: (pt[b * n_pages + j * pg + p], 0, 0))

    per_b = lambda a: pl.BlockSpec((1,) + a.shape[1:], lambda b, j, pt: (b, 0, 0))
    grid_spec = pltpu.PrefetchScalarGridSpec(
        num_scalar_prefetch=1,
        grid=(db, n_pages // pg),
        in_specs=[per_b(qa), per_b(c_new), per_b(kr_new)]
        + [page_spec(MLA_KV_LORA, p) for p in range(pg)] + [page_spec(MLA_ROPE, p) for p in range(pg)],
        out_specs=pl.BlockSpec((1, r, MLA_KV_LORA), lambda b, j, pt: (b, 0, 0)),
        scratch_shapes=[pltpu.VMEM((r, 1), F32), pltpu.VMEM((r, 1), F32), pltpu.VMEM((r, MLA_KV_LORA), F32)],
    )
    return pl.pallas_call(
        functools.partial(_mla_sample_kernel, pg=pg, nq=nq),
        grid_spec=grid_spec,
        out_shape=jax.ShapeDtypeStruct((db, r, MLA_KV_LORA), F32),
        compiler_params=_cparams(2),
    )(page_table.reshape(-1), qa, c_new, kr_new, *([ckv_pool] * pg), *([kr_pool] * pg))


def _cmp_sample_kernel(pt_ref, phi_ref, *refs, pg):
    del pt_ref
    o_ref = refs[pg]
    per_page = PAGE // CMP_BLOCK
    means = []
    for ref in refs[:pg]:
        x = ref[0]
        means.append(jnp.sum(x.reshape(per_page, CMP_BLOCK, x.shape[1]), axis=1) * (1.0 / CMP_BLOCK))
    m = jnp.concatenate(means, axis=0)
    o_ref[0] = _dot(m.astype(BF16), phi_ref[...])


def _compress_sample(page_table, cmp_pool, phi):
    db, n_pages = page_table.shape
    pg = min(n_pages, 16)
    w = cmp_pool.shape[2]
    per_page = PAGE // CMP_BLOCK
    grid_spec = pltpu.PrefetchScalarGridSpec(
        num_scalar_prefetch=1,
        grid=(db, n_pages // pg),
        in_specs=[pl.BlockSpec(phi.shape, lambda b, j, pt: (0, 0))]
        + [pl.BlockSpec((1, PAGE, w), lambda b, j, pt, p=p: (pt[b * n_pages + j * pg + p], 0, 0)) for p in range(pg)],
        out_specs=pl.BlockSpec((1, pg * per_page, w), lambda b, j, pt: (b, j, 0)),
    )
    return pl.pallas_call(
        functools.partial(_cmp_sample_kernel, pg=pg),
        grid_spec=grid_spec,
        out_shape=jax.ShapeDtypeStruct((db, n_pages * per_page, w), F32),
        compiler_params=_cparams(2),
    )(page_table.reshape(-1), phi, *([cmp_pool] * pg))


def _nsa_cmp_sample_kernel(q_ref, ck_ref, cv_ref, o_ref, imp_ref, *, nq, past):
    r = q_ref.shape[2]
    nc = ck_ref.shape[2]
    half = nc // 2
    row = _iota((r, 1), 0)
    qpos = past + (row & (nq - 1))
    lane = _iota((1, nc), 1)
    cblk = jnp.where(lane < half, 2 * lane, 2 * (lane - half) + 1)
    cpos = cblk * CMP_BLOCK + (CMP_BLOCK - 1)
    cmask = cpos <= qpos
    for g in range(NSA_G):
        slope = _pow2_neg(NSA_HPG * g + (row >> (nq.bit_length() - 1)) + 1)
        s = _dot_nt(q_ref[0, g], ck_ref[0, g]) - slope * (qpos - cpos).astype(F32)
        s = jnp.where(cmask, s, NEG_INF)
        mx = jnp.max(s, axis=-1, keepdims=True)
        mx = jnp.where(mx == NEG_INF, 0.0, mx)
        e = jnp.where(cmask, jnp.exp(s - mx), 0.0)
        p = e / jnp.maximum(jnp.sum(e, axis=-1, keepdims=True), 1e-30)
        o_ref[0, g] = _dot(p.astype(BF16), cv_ref[0, g])
        ps = p
        for h in range(1, NSA_HPG):
            ps = ps + pltpu.roll(p, h * nq, 0)
        imp_ref[0, g] = ps[:, :half] + ps[:, half:]


def _nsa_cmp_sample(q, ck, cv, *, nq, past):
    db, g, r, dh = q.shape
    nc = ck.shape[2]
    blk = lambda a: pl.BlockSpec((1,) + a.shape[1:], lambda b: (b, 0, 0, 0))
    return pl.pallas_call(
        functools.partial(_nsa_cmp_sample_kernel, nq=nq, past=past),
        grid=(db,),
        in_specs=[blk(q), blk(ck), blk(cv)],
        out_specs=[pl.BlockSpec((1, g, r, dh), lambda b: (b, 0, 0, 0)),
                   pl.BlockSpec((1, g, r, nc // 2), lambda b: (b, 0, 0, 0))],
        out_shape=[jax.ShapeDtypeStruct((db, g, r, dh), F32), jax.ShapeDtypeStruct((db, g, r, nc // 2), F32)],
        compiler_params=_cparams(1),
    )(q, ck, cv)


def _topk_sample_kernel(imp_ref, o_ref, *, n_pick, last_blk):
    s = imp_ref[...]
    lane = _iota(s.shape, 1).astype(F32)
    s = jnp.where(lane == 0.0, -1.0, s)
    olane = _iota(o_ref.shape, 1)
    out = jnp.where(olane == n_pick + 1, float(last_blk), 0.0)
    for it in range(n_pick):
        mx = jnp.max(s, axis=-1, keepdims=True)
        idx = jnp.min(jnp.where(s == mx, lane, float(s.shape[1])), axis=-1, keepdims=True)
        s = jnp.where(lane == idx, -1.0, s)
        out = jnp.where(olane == it, idx, out)
    o_ref[...] = out.astype(I32)


def _topk_sample(imp, *, n_pick, last_blk):
    rows = imp.shape[0]
    return pl.pallas_call(
        functools.partial(_topk_sample_kernel, n_pick=n_pick, last_blk=last_blk),
        grid=(1,),
        in_specs=[pl.BlockSpec(imp.shape, lambda i: (0, 0))],
        out_specs=pl.BlockSpec((rows, LANES), lambda i: (0, 0)),
        out_shape=jax.ShapeDtypeStruct((rows, LANES), I32),
        compiler_params=_cparams(1),
    )(imp)


def _nsa_sel_sample_kernel(pt_ref, ids_ref, q_ref, idv_ref, new_ref, *refs, nblk, nq, past):
    del pt_ref, ids_ref
    o_ref = refs[nblk]
    g = pl.program_id(1)
    qi = pl.program_id(2)
    q = q_ref[0, 0, 0]
    nrow = q.shape[0]
    pad = jnp.zeros((SEL_BLOCK - new_ref.shape[1], LANES), F32)
    kv = jnp.concatenate([ref[0] for ref in refs[:nblk]] + [new_ref[0], pad], axis=0).astype(BF16)
    nkey = kv.shape[0]
    idv = idv_ref[0].astype(F32)
    idv = jnp.where(_iota((1, LANES), 1) < nblk, idv, 0.0)
    erow = _iota((LANES, 1), 0)
    ecol = _iota((1, nkey), 1)
    expand = jnp.where(erow == (ecol >> 6), float(SEL_BLOCK), 0.0).astype(BF16)
    base = _dot(jnp.broadcast_to(idv, (nrow, LANES)).astype(BF16), expand)
    off = (ecol & (SEL_BLOCK - 1)).astype(F32)
    is_new = ecol >= nblk * SEL_BLOCK
    kpos = jnp.where(is_new, float(past) + off, base + off)
    qpos = (past + qi).astype(F32)
    slope = _pow2_neg(NSA_HPG * g + jnp.minimum(_iota((nrow, 1), 0), NSA_HPG - 1) + 1)
    s = _dot_nt(q, kv[:, :NSA_DH]) - slope * (qpos - kpos)
    s = jnp.where(kpos <= qpos, s, NEG_INF)
    mx = jnp.max(s, axis=-1, keepdims=True)
    e = jnp.exp(s - mx)
    p = e / jnp.sum(e, axis=-1, keepdims=True)
    o_ref[0, 0, 0] = _dot(p.astype(BF16), kv)


def _nsa_sel_sample(page_table, ids, q, idv, sel_new, sel_pool, *, nblk, nq, past):
    db, n_pages = page_table.shape
    half_pages = PAGE // SEL_BLOCK
    pool = sel_pool.reshape(sel_pool.shape[0] * half_pages, SEL_BLOCK, sel_pool.shape[2])

    def blk_spec(c):
        def index(b, g, qi, pt, idr):
            blk = idr[((b * NSA_G + g) * nq + qi) * SEL_TOPK + c]
            return (pt[b * n_pages + blk // half_pages] * half_pages + blk % half_pages, 0, g)
        return pl.BlockSpec((1, SEL_BLOCK, LANES), index)

    grid_spec = pltpu.PrefetchScalarGridSpec(
        num_scalar_prefetch=2,
        grid=(db, NSA_G, nq),
        in_specs=[pl.BlockSpec((1, 1, 1) + q.shape[3:], lambda b, g, qi, pt, idr: (b, g, qi, 0, 0)),
                  pl.BlockSpec((1, 1, LANES), lambda b, g, qi, pt, idr: ((b * NSA_G + g) * nq + qi, 0, 0)),
                  pl.BlockSpec((1, sel_new.shape[1], LANES), lambda b, g, qi, pt, idr: (b, 0, g))]
        + [blk_spec(c) for c in range(nblk)],
        out_specs=pl.BlockSpec((1, 1, 1, q.shape[3], LANES), lambda b, g, qi, pt, idr: (b, g, qi, 0, 0)),
    )
    return pl.pallas_call(
        functools.partial(_nsa_sel_sample_kernel, nblk=nblk, nq=nq, past=past),
        grid_spec=grid_spec,
        out_shape=jax.ShapeDtypeStruct((db, NSA_G, nq, q.shape[3], LANES), F32),
        compiler_params=_cparams(3),
    )(page_table.reshape(-1), ids.reshape(-1), q, idv, sel_new, *([pool] * nblk))


def _nsa_win_sample_kernel(q_ref, st_ref, new_ref, o_ref, *, nq, past):
    r = q_ref.shape[2]
    wbuf = st_ref.shape[1]
    row = _iota((r, 1), 0)
    qi = row & (nq - 1)
    dist_st = qi + wbuf - _iota((1, wbuf), 1)
    tnew = _iota((1, new_ref.shape[1]), 1)
    dist_new = qi - tnew
    for g in range(NSA_G):
        slope = _pow2_neg(NSA_HPG * g + (row >> (nq.bit_length() - 1)) + 1)
        q = q_ref[0, g]
        st = st_ref[0, :, g * LANES:(g + 1) * LANES].astype(BF16)
        nw = new_ref[0, :, g * LANES:(g + 1) * LANES].astype(BF16)
        s1 = _dot_nt(q, st[:, :NSA_DH]) - slope * dist_st.astype(F32)
        s1 = jnp.where(dist_st <= WINDOW, s1, NEG_INF)
        s2 = _dot_nt(q, nw[:, :NSA_DH]) - slope * dist_new.astype(F32)
        s2 = jnp.where(dist_new >= 0, jnp.where(tnew < nq, s2, NEG_INF), NEG_INF)
        mx = jnp.maximum(jnp.max(s1, axis=-1, keepdims=True), jnp.max(s2, axis=-1, keepdims=True))
        e1 = jnp.exp(s1 - mx)
        e2 = jnp.exp(s2 - mx)
        den = jnp.sum(e1, axis=-1, keepdims=True) + jnp.sum(e2, axis=-1, keepdims=True)
        o_ref[0, g] = (_dot(e1.astype(BF16), st) + _dot(e2.astype(BF16), nw)) / den


def _nsa_win_sample(q, state, win_new, *, nq, past):
    db, g, r, _ = q.shape
    return pl.pallas_call(
        functools.partial(_nsa_win_sample_kernel, nq=nq, past=past),
        grid=(db,),
        in_specs=[pl.BlockSpec((1,) + q.shape[1:], lambda b: (b, 0, 0, 0)),
                  pl.BlockSpec((1,) + state.shape[1:], lambda b: (b, 0, 0)),
                  pl.BlockSpec((1,) + win_new.shape[1:], lambda b: (b, 0, 0))],
        out_specs=pl.BlockSpec((1, g, r, LANES), lambda b: (b, 0, 0, 0)),
        out_shape=jax.ShapeDtypeStruct((db, g, r, LANES), F32),
        compiler_params=_cparams(1),
    )(q, state, win_new)


def _gate_kernel(g_ref, a_ref, b_ref, c_ref, o_ref):
    o_ref[...] = (g_ref[0] * a_ref[...] + g_ref[1] * b_ref[...] + g_ref[2] * c_ref[...]).astype(o_ref.dtype)


def _nsa_gate_sample(gexp, o_cmp, o_sel, o_win):
    t, w = o_cmp.shape
    row = pl.BlockSpec((t, w), lambda i: (0, 0))
    return pl.pallas_call(
        _gate_kernel,
        grid=(1,),
        in_specs=[pl.BlockSpec((3, t, w), lambda i: (0, 0, 0)), row, row, row],
        out_specs=row,
        out_shape=jax.ShapeDtypeStruct((t, w), BF16),
        compiler_params=_cparams(1),
    )(gexp, o_cmp, o_sel, o_win)


def _xattn_sample_kernel(q_ref, kv_ref, o_ref, *, bc):
    hw = X_HEADS * X_DH
    for bi in range(bc):
        for h in range(X_HEADS):
            cs = slice(h * X_DH, (h + 1) * X_DH)
            k = kv_ref[bi, :, cs].astype(BF16)
            v = kv_ref[bi, :, hw + h * X_DH:hw + (h + 1) * X_DH].astype(BF16)
            s = _dot_nt(q_ref[bi, :, cs], k)
            e = jnp.exp(s - jnp.max(s, axis=-1, keepdims=True))
            p = e / jnp.sum(e, axis=-1, keepdims=True)
            o_ref[bi, :, cs] = _dot(p.astype(BF16), v).astype(o_ref.dtype)


def _xattn_sample(q, kv):
    db, r, w = q.shape
    m = kv.shape[1]
    bc = 4 if db % 4 == 0 else 1
    return pl.pallas_call(
        functools.partial(_xattn_sample_kernel, bc=bc),
        grid=(db // bc,),
        in_specs=[pl.BlockSpec((bc, r, w), lambda i: (i, 0, 0)), pl.BlockSpec((bc, m, 2 * w), lambda i: (i, 0, 0))],
        out_specs=pl.BlockSpec((bc, r, w), lambda i: (i, 0, 0)),
        out_shape=jax.ShapeDtypeStruct((db, r, w), BF16),
        compiler_params=_cparams(1),
    )(q, kv)


def _prep_weights(w_in, mla_w_uq, mla_w_uk, mla_w_uv, nsa_phi_k, nsa_phi_v):
    d = w_in.shape[0]
    z = lambda n: jnp.zeros((d, n), F32)
    o = 0
    cq, o = w_in[:, o:o + 384], o + 384
    ckv, o = w_in[:, o:o + 256], o + 256
    kr, o = w_in[:, o:o + 32], o + 32
    qn, o = w_in[:, o:o + 512], o + 512
    kvn, o = w_in[:, o:o + 768], o + 768
    gn, o = w_in[:, o:o + 24], o + 24
    mg = w_in[:, o:]
    w_in_p = jnp.concatenate([cq, ckv, z(64), kr, z(32), qn, kvn, gn[:, :12], z(116), gn[:, 12:], z(116), mg],
                             axis=1).astype(BF16)
    w_uq_p = jnp.pad(mla_w_uq, ((0, 0), (0, 0), (0, LANES - MLA_NOPE - MLA_ROPE))).reshape(MLA_Q_LORA, -1).astype(BF16)
    w_uk_p = jnp.pad(mla_w_uk, ((0, 0), (0, 0), (0, LANES - MLA_NOPE))).reshape(MLA_KV_LORA, -1).astype(BF16)
    w_uv_p = mla_w_uv.reshape(MLA_KV_LORA, -1).astype(BF16)
    w_abs = jnp.zeros((MLA_HEADS, LANES, 384), F32)
    w_abs = w_abs.at[:, :MLA_NOPE, :MLA_KV_LORA].set(jnp.transpose(mla_w_uk, (1, 2, 0)))
    w_abs = w_abs.at[:, MLA_NOPE:MLA_NOPE + MLA_ROPE, MLA_KV_LORA:MLA_KV_LORA + MLA_ROPE].set(jnp.eye(MLA_ROPE, dtype=F32))
    w_ov = jnp.zeros((MLA_HEADS, MLA_KV_LORA, MLA_HEADS, MLA_V), F32)
    for h in range(MLA_HEADS):
        w_ov = w_ov.at[h, :, h, :].set(mla_w_uv[:, h, :])
    w_ov = w_ov.reshape(MLA_HEADS * MLA_KV_LORA, MLA_HEADS * MLA_V)
    phi = jnp.zeros((4, NSA_DH, 4, NSA_DH), F32)
    for g in range(NSA_G):
        phi = phi.at[2 * g, :, 2 * g, :].set(nsa_phi_k[g])
        phi = phi.at[2 * g + 1, :, 2 * g + 1, :].set(nsa_phi_v[g])
    phi = phi.reshape(4 * NSA_DH, 4 * NSA_DH)
    return w_in_p, w_uq_p, w_uk_p, w_uv_p, w_abs.astype(BF16), w_ov.astype(BF16), phi.astype(BF16)


def _rope_tables(pos):
    half = MLA_ROPE // 2
    inv = ROPE_BASE ** (-jnp.arange(half, dtype=F32) / half)
    ang = pos.astype(F32)[:, None] * inv[None, :]
    cos, sin = jnp.cos(ang), jnp.sin(ang)
    n = pos.shape[0]
    one, zero = jnp.ones((n, MLA_NOPE), F32), jnp.zeros((n, half), F32)
    tail = jnp.zeros((n, LANES - MLA_NOPE - MLA_ROPE), F32)
    rc = jnp.concatenate([one, cos, cos, tail], axis=1)
    rs1 = jnp.concatenate([0 * one, -sin, zero, tail], axis=1)
    rs2 = jnp.concatenate([0 * one, zero, sin, tail], axis=1)
    return rc, rs1, rs2


def _even_odd(a, axis):
    idx = jnp.concatenate([jnp.arange(0, a.shape[axis], 2), jnp.arange(1, a.shape[axis], 2)])
    return jnp.take(a, idx, axis=axis)


def _split_cmp(c):
    n, nc, _ = c.shape
    c = _even_odd(c, 1).reshape(n, nc, NSA_G, 2, NSA_DH).astype(BF16)
    c = jnp.transpose(c, (0, 2, 3, 1, 4))
    return c[:, :, 0], c[:, :, 1]


def kernel(x_prompt, x_sample, mem_prompt, cache_mla_ckv, cache_mla_krope, cache_nsa_cmp_kv, cache_nsa_sel_kv,
           state_nsa_win_kv, cache_xattn_kv, page_table, ffn1_norm, ffn1_w_gate, ffn1_w_up, ffn1_w_down, mix_norm,
           w_in, mla_q_norm, mla_w_uq, mla_kv_norm, mla_w_uk, mla_w_uv, nsa_phi_k, nsa_phi_v, w_br_mla, w_br_nsa,
           w_out, xattn_norm, xattn_mem_norm, xattn_w_q, xattn_w_kv, xattn_w_o, ffn2_norm, ffn2_w_gate, ffn2_w_up,
           ffn2_w_down, final_norm):
    assert x_prompt.shape[2] == D_MODEL and ffn1_norm.shape[0] == 1
    b, s, d = x_prompt.shape
    db, nq, _ = x_sample.shape
    n_pages = page_table.shape[1]
    past = n_pages * PAGE
    n_mem = mem_prompt.shape[1]
    g2 = lambda a: a.reshape(1, -1)
    bf = lambda a: a.astype(BF16)

    w_in_p, w_uq_p, w_uk_p, w_uv_p, w_abs, w_ov, phi = _prep_weights(
        w_in[0], mla_w_uq[0], mla_w_uk[0], mla_w_uv[0], nsa_phi_k[0], nsa_phi_v[0])
    f1 = (g2(ffn1_norm[0]), bf(ffn1_w_gate[0]), bf(ffn1_w_up[0]), bf(ffn1_w_down[0]))
    f2 = (g2(ffn2_norm[0]), bf(ffn2_w_gate[0]), bf(ffn2_w_up[0]), bf(ffn2_w_down[0]))
    fg = g2(final_norm)
    w_a, w_b, w_o = bf(w_br_mla[0]), bf(w_br_nsa[0]), bf(w_out[0])
    w_xq, w_xo = bf(xattn_w_q[0]), bf(xattn_w_o[0])
    w_xkv = bf(xattn_w_kv[0].reshape(d, -1))
    proj_w = (g2(mix_norm[0]), w_in_p, g2(mla_q_norm[0]), w_uq_p, g2(mla_kv_norm[0]), w_uk_p, w_uv_p)

    t = b * s
    hp = _ffn(x_prompt.reshape(t, d), *f1, fg, final=False)
    (q_mla, k_mla, v_mla, p_ckv, p_krb, q_nsa, p_cmp, p_sel, p_win, selb, winb, gates, merge) = _mixer_project(
        hp, *proj_w, *_rope_tables(jnp.arange(s)))
    o_a = _mla_prompt(q_mla.reshape(b, s, -1), k_mla.reshape(b, s, -1), v_mla.reshape(b, s, -1)).reshape(t, -1)

    tq = min(s, 128)
    ck, cv = _split_cmp(_compress_prompt(p_cmp.reshape(b, s, -1), phi))
    q_st = jnp.transpose(q_nsa.reshape(b, s // tq, tq, NSA_G, NSA_HPG, NSA_DH), (0, 3, 1, 4, 2, 5))
    q_st = q_st.reshape(b, NSA_G, s // tq, NSA_HPG * tq, NSA_DH)
    kv_heads = lambda a: jnp.transpose(a.reshape(b, s, NSA_G, 2, NSA_DH), (3, 0, 2, 1, 4))
    ks, vs = kv_heads(selb)
    kw, vw = kv_heads(winb)
    o_nsa = _nsa_prompt(q_st, ck, cv, ks, vs, kw, vw, gates.reshape(b, s, -1), tq=tq)
    o_b = jnp.transpose(o_nsa, (0, 3, 1, 2, 4)).reshape(t, -1)
    hp = _merge(o_a, o_b, merge, hp, w_a, w_b, w_o)

    kv_mem = _norm_matmul(mem_prompt.reshape(b * n_mem, d), g2(xattn_mem_norm[0]), w_xkv)
    xq = _norm_matmul(hp, g2(xattn_norm[0]), w_xq, scale=X_SCALE, out_dtype=BF16)
    xo = _xattn_prompt(xq.reshape(b, s, -1), bf(kv_mem).reshape(b, n_mem, -1))
    hp = _matmul_residual(xo.reshape(t, -1), w_xo, hp)
    y_prompt = _ffn(hp, *f2, fg, final=True).reshape(b, s, d)

    wlen = min(WINDOW, s)
    p_kr = p_krb[:, MLA_NOPE:MLA_NOPE + MLA_ROPE]
    kvshape = (1, b, s, NSA_G, 2, NSA_DH)
    prompt_caches = (p_ckv.reshape(1, b, s, -1), p_kr.reshape(1, b, s, -1), p_cmp.reshape(kvshape),
                     p_sel.reshape(kvshape), p_win.reshape(kvshape)[:, :, s - wlen:],
                     kv_mem.reshape(1, b, n_mem, 2, X_HEADS, X_DH))

    ts = db * nq
    hs = _ffn(x_sample.reshape(ts, d), *f1, fg, final=False)
    pos_s = jnp.tile(past + jnp.arange(nq), db)
    (q_mla, _, _, s_ckv, s_krb, q_nsa, s_cmp, s_sel, s_win, _, _, gates, merge) = _mixer_project(
        hs, *proj_w, *_rope_tables(pos_s))
    s_kr = s_krb[:, MLA_NOPE:MLA_NOPE + MLA_ROPE]
    rpad = 8 - nq
    pad_rows = lambda a: jnp.pad(a.reshape(db, nq, -1), ((0, 0), (0, rpad), (0, 0)))

    qa = _mla_absorb_q(q_mla, w_abs)
    qa = jnp.transpose(qa.reshape(MLA_HEADS, db, nq, -1), (1, 0, 2, 3)).reshape(db, MLA_HEADS * nq, -1)
    o_lat = _mla_sample(page_table, qa, pad_rows(s_ckv), pad_rows(s_kr), cache_mla_ckv[0], cache_mla_krope[0], nq=nq)
    o_lat = jnp.transpose(o_lat.reshape(db, MLA_HEADS, nq, -1), (0, 2, 1, 3)).reshape(ts, -1)
    o_a = _matmul(bf(o_lat), w_ov, out_dtype=BF16)

    cmp_pool = cache_nsa_cmp_kv[0].reshape(-1, PAGE, 2 * NSA_G * NSA_DH)
    sel_pool = cache_nsa_sel_kv[0].reshape(-1, PAGE, 2 * NSA_G * NSA_DH)
    ck, cv = _split_cmp(_compress_sample(page_table, cmp_pool, phi))
    qn = jnp.transpose(q_nsa.reshape(db, nq, NSA_G, NSA_HPG, NSA_DH), (0, 2, 3, 1, 4))
    q_rows = qn.reshape(db, NSA_G, NSA_HPG * nq, NSA_DH)
    o_cmp, imp = _nsa_cmp_sample(q_rows, ck, cv, nq=nq, past=past)
    n_sel = -(-(past + nq) // SEL_BLOCK)
    n_pick = min(SEL_TOPK, n_sel) - 2
    ids = _topk_sample(imp[:, :, :nq].reshape(db * NSA_G * nq, -1), n_pick=n_pick, last_blk=n_sel - 1)
    q_sel = jnp.pad(jnp.transpose(qn, (0, 1, 3, 2, 4)), ((0, 0), (0, 0), (0, 0), (0, 8 - NSA_HPG), (0, 0)))
    o_sel = _nsa_sel_sample(page_table, ids[:, :SEL_TOPK], q_sel, ids.reshape(-1, 1, LANES), pad_rows(s_sel),
                            sel_pool, nblk=n_pick + 1, nq=nq, past=past)
    state = state_nsa_win_kv[0].reshape(db, -1, 2 * NSA_G * NSA_DH)
    o_win = _nsa_win_sample(q_rows, state, pad_rows(s_win), nq=nq, past=past)
    to_tok = lambda a: jnp.transpose(a.reshape(db, NSA_G, NSA_HPG, nq, NSA_DH), (0, 3, 1, 2, 4)).reshape(ts, -1)
    o_sel_t = jnp.transpose(o_sel[:, :, :, :NSA_HPG, NSA_DH:], (0, 2, 1, 3, 4)).reshape(ts, -1)
    gts = jnp.stack([gates[:, :12], gates[:, LANES:LANES + 12]], axis=1).reshape(ts, NSA_HEADS, 3)
    gexp = jnp.transpose(jnp.broadcast_to(gts[:, :, :, None], (ts, NSA_HEADS, 3, NSA_DH)), (2, 0, 1, 3)).reshape(3, ts, -1)
    o_b = _nsa_gate_sample(gexp, to_tok(o_cmp), o_sel_t, to_tok(o_win[..., NSA_DH:]))
    hs = _merge(o_a, o_b, merge, hs, w_a, w_b, w_o)

    xq = _norm_matmul(hs, g2(xattn_norm[0]), w_xq, scale=X_SCALE, out_dtype=BF16)
    xo = _xattn_sample(pad_rows(xq), cache_xattn_kv[0].reshape(db, n_mem, -1))
    hs = _matmul_residual(xo[:, :nq].reshape(ts, -1), w_xo, hs)
    y_sample = _ffn(hs, *f2, fg, final=True).reshape(db, nq, d)

    kvs = (1, db, nq, NSA_G, 2, NSA_DH)
    win_upd = jnp.concatenate([state_nsa_win_kv[0], s_win.reshape(kvs[1:])], axis=1)[:, nq:]
    sample_caches = (s_ckv.reshape(1, db, nq, -1), s_kr.reshape(1, db, nq, -1), s_cmp.reshape(kvs),
                     s_sel.reshape(kvs), win_upd[None])
    return (y_prompt, y_sample) + prompt_caches + sample_caches
```

```python
import functools
import math

import jax
import jax.numpy as jnp
from jax import lax
from jax.experimental import pallas as pl
from jax.experimental.pallas import tpu as pltpu

F32 = jnp.float32
BF16 = jnp.bfloat16
I32 = jnp.int32

D_MODEL = 1024
EPS = 1e-6
PAGE = 128
MLA_HEADS = 8
MLA_Q_LORA = 384
MLA_KV_LORA = 256
MLA_NOPE = 64
MLA_ROPE = 32
MLA_V = 64
MLA_SCALE = (MLA_NOPE + MLA_ROPE) ** -0.5
ROPE_BASE = 10000.0
NSA_HEADS = 8
NSA_G = 2
NSA_HPG = 4
NSA_DH = 64
NSA_SCALE = NSA_DH ** -0.5
CMP_BLOCK = 32
SEL_BLOCK = 64
SEL_TOPK = 16
WINDOW = 512
FORCE_BONUS = 4.0 * NSA_HPG
X_HEADS = 4
X_DH = 128
X_SCALE = X_DH ** -0.5
LANES = 128
LOG2E = math.log2(math.e)
NEG_INF = float("-inf")
MASK_BIG = 2.0 ** 60
VMEM_LIMIT = 56 * 1024 * 1024
TILE = 256

C_CQ = 0
C_CKV = 384
C_KR = 640
C_QN = 768
C_GT = 1792
C_MG = 2048
C_KVN = 4096
C_END = 4864
A_BLK, A_OFF, A_ONE0, A_ONE1 = 64, 65, 66, 67


def _cparams(n_axes):
    return pltpu.CompilerParams(dimension_semantics=("arbitrary",) * n_axes, vmem_limit_bytes=VMEM_LIMIT)


def _dot(a, b):
    return jnp.dot(a, b, preferred_element_type=F32)


def _dot_nt(a, b):
    return lax.dot_general(a, b, (((1,), (1,)), ((), ())), preferred_element_type=F32)


def _rms(x, g):
    ms = jnp.mean(x * x, axis=-1, keepdims=True)
    return x * lax.rsqrt(ms + EPS) * g


def _iota(shape, dim):
    return lax.broadcasted_iota(I32, shape, dim)


def _pow2_neg(e):
    return lax.bitcast_convert_type((127 - e) << 23, F32)


def _query_aug(slope, qpos, lane):
    s64 = slope * float(SEL_BLOCK)
    return jnp.where(lane == A_BLK, s64,
           jnp.where(lane == A_OFF, slope,
           jnp.where(lane == A_ONE0, -s64 * (qpos >> 6).astype(F32),
           jnp.where(lane == A_ONE1, -slope * (qpos & (SEL_BLOCK - 1)).astype(F32), 0.0))))


def _flash_step(s, v_t, m_ref, acc_ref, idx, exp_fn):
    m_prev = m_ref[idx]
    m_new = jnp.maximum(m_prev, jnp.max(s, axis=-1, keepdims=True))
    alpha = exp_fn(m_prev - m_new)
    p = exp_fn(s - jnp.concatenate([m_new] * (s.shape[1] // LANES), axis=1))
    acc_ref[idx] = alpha * acc_ref[idx] + _dot_nt(p.astype(BF16), v_t)
    m_ref[idx] = m_new


def _col_softmax_init(m_ref, l_ref, acc_ref):
    m_ref[...] = jnp.full(m_ref.shape, NEG_INF, F32)
    l_ref[...] = jnp.zeros(l_ref.shape, F32)
    acc_ref[...] = jnp.zeros(acc_ref.shape, F32)


def _col_softmax_step(s, v, m_ref, l_ref, acc_ref, exp_fn):
    m_prev = m_ref[...]
    m_new = jnp.maximum(m_prev, jnp.max(s, axis=-1, keepdims=True))
    m_safe = jnp.where(m_new == NEG_INF, 0.0, m_new)
    alpha = exp_fn(m_prev - m_safe)
    p = exp_fn(s - m_safe)
    l_ref[...] = alpha * l_ref[...] + jnp.sum(p, axis=-1, keepdims=True)
    acc_ref[...] = alpha * acc_ref[...] + _dot(p.astype(BF16), v)
    m_ref[...] = m_new


def _ffn_kernel(x_ref, g_ref, wg_ref, wu_ref, wd_ref, fg_ref, o_ref, h_ref, acc_ref, *, final):
    j = pl.program_id(1)

    @pl.when(j == 0)
    def _():
        h_ref[...] = _rms(x_ref[...], g_ref[...]).astype(BF16)
        acc_ref[...] = jnp.zeros(acc_ref.shape, F32)

    h = h_ref[...]
    a = _dot(h, wg_ref[...])
    u = _dot(h, wu_ref[...])
    act = (a * jax.nn.sigmoid(a)) * u
    acc_ref[...] += _dot(act.astype(BF16), wd_ref[...])

    @pl.when(j == pl.num_programs(1) - 1)
    def _():
        y = x_ref[...] + 0.5 * acc_ref[...]
        if final:
            y = _rms(y, fg_ref[...])
        o_ref[...] = y


def _ffn(x, g, wg, wu, wd, fg, *, final):
    t, d = x.shape
    ff = wg.shape[1]
    tm = min(t, 1024)
    tf = 256
    return pl.pallas_call(
        functools.partial(_ffn_kernel, final=final),
        grid=(t // tm, ff // tf),
        in_specs=[
            pl.BlockSpec((tm, d), lambda i, j: (i, 0)),
            pl.BlockSpec((1, d), lambda i, j: (0, 0)),
            pl.BlockSpec((d, tf), lambda i, j: (0, j)),
            pl.BlockSpec((d, tf), lambda i, j: (0, j)),
            pl.BlockSpec((tf, d), lambda i, j: (j, 0)),
            pl.BlockSpec((1, d), lambda i, j: (0, 0)),
        ],
        out_specs=pl.BlockSpec((tm, d), lambda i, j: (i, 0)),
        out_shape=jax.ShapeDtypeStruct((t, d), F32),
        scratch_shapes=[pltpu.VMEM((tm, d), BF16), pltpu.VMEM((tm, d), F32)],
        compiler_params=_cparams(2),
    )(x, g, wg, wu, wd, fg)


def _proj_common(x_ref, g_ref, win_ref, qn_ref, wuq_ref, kvn_ref, rc_ref, rs1_ref, rs2_ref,
                 qmla_ref, ckv_ref, qnsa_ref, gates_ref, merge_ref):
    n = _rms(x_ref[...], g_ref[...]).astype(BF16)

    def seg(a, b):
        return _dot(n, win_ref[:, a:b])

    rc, rs1, rs2 = rc_ref[...], rs1_ref[...], rs2_ref[...]

    def rope(blk):
        return blk * rc + pltpu.roll(blk, 112, 1) * rs1 + pltpu.roll(blk, 16, 1) * rs2

    cq = _rms(seg(C_CQ, C_CKV), qn_ref[...]).astype(BF16)
    q = _dot(cq, wuq_ref[...]) * (MLA_SCALE * LOG2E)
    for h in range(MLA_HEADS):
        qmla_ref[:, h * LANES:(h + 1) * LANES] = rope(q[:, h * LANES:(h + 1) * LANES]).astype(BF16)

    c_kv = _rms(seg(C_CKV, C_KR), kvn_ref[...])
    ckv_ref[...] = c_kv
    krb = rope(seg(C_KR, C_QN))
    qnsa_ref[...] = (seg(C_QN, C_GT) * NSA_SCALE).astype(BF16)
    gates_ref[...] = jax.nn.sigmoid(seg(C_GT, C_MG))
    merge_ref[...] = jax.nn.sigmoid(seg(C_MG, C_KVN))
    return n, seg, c_kv, krb


def _proj_prompt_kernel(x_ref, g_ref, win_ref, wkv_t_ref, qn_ref, wuq_ref, kvn_ref, wuk_t_ref, wuv_t_ref,
                        rc_ref, rs1_ref, rs2_ref,
                        qmla_ref, k_t_ref, v_t_ref, ckv_ref, kr_t_ref, qnsa_ref, cmp_t_ref, sel_t_ref, win_t_ref,
                        sk_ref, sv_ref, wk_ref, wv_ref, gates_ref, merge_ref):
    n, _, c_kv, krb = _proj_common(x_ref, g_ref, win_ref, qn_ref, wuq_ref, kvn_ref, rc_ref, rs1_ref, rs2_ref,
                                   qmla_ref, ckv_ref, qnsa_ref, gates_ref, merge_ref)
    tm = n.shape[0]
    cb = c_kv.astype(BF16)
    kr_t = jnp.transpose(krb)
    kr_t_ref[0] = kr_t[MLA_NOPE:MLA_NOPE + MLA_ROPE]
    rows = _iota((LANES, tm), 0)
    k_t = _dot_nt(wuk_t_ref[...], cb)
    v_t = _dot_nt(wuv_t_ref[...], cb)
    for h in range(MLA_HEADS):
        hs = slice(h * LANES, (h + 1) * LANES)
        k_t_ref[0, h, 0] = (k_t[hs] + kr_t).astype(BF16)
        v_t_ref[0, h, 0] = jnp.where(rows >= MLA_V, 1.0, v_t[hs]).astype(BF16)
    kv_t = _dot_nt(wkv_t_ref[...], n)
    cmp_t_ref[0] = kv_t[0:256]
    sel_t_ref[0] = kv_t[256:512]
    win_t_ref[0] = kv_t[512:768]
    for g in range(NSA_G):
        for base, k_out, v_out in ((256, sk_ref, sv_ref), (512, wk_ref, wv_ref)):
            blk = kv_t[base + g * LANES:base + (g + 1) * LANES]
            k_out[0, g, 0] = blk.astype(BF16)
            v_out[0, g, 0] = jnp.where(rows < NSA_DH, 1.0, blk).astype(BF16)


def _proj_sample_kernel(x_ref, g_ref, win_ref, qn_ref, wuq_ref, kvn_ref, rc_ref, rs1_ref, rs2_ref,
                        qmla_ref, ckv_ref, kr_ref, qnsa_ref, cmp_ref, sel_ref, win_o_ref, gates_ref, merge_ref):
    _, seg, _, krb = _proj_common(x_ref, g_ref, win_ref, qn_ref, wuq_ref, kvn_ref, rc_ref, rs1_ref, rs2_ref,
                                  qmla_ref, ckv_ref, qnsa_ref, gates_ref, merge_ref)
    kr_ref[...] = krb
    cmp_ref[...] = seg(C_KVN, C_KVN + 256)
    sel_ref[...] = seg(C_KVN + 256, C_KVN + 512)
    win_o_ref[...] = seg(C_KVN + 512, C_END)


def _mixer_project_prompt(x, b, s, g, w_in_p, w_kv_t, q_norm, w_uq_p, kv_norm, w_uk_t, w_uv_t, rc, rs1, rs2):
    t, d = x.shape
    tm = TILE
    nt = s // tm
    full = lambda a: pl.BlockSpec(a.shape, lambda i: (0,) * a.ndim)
    row = lambda w: pl.BlockSpec((tm, w), lambda i: (i, 0))
    tab = pl.BlockSpec((tm, LANES), lambda i: (i % nt, 0))
    fm = lambda r: pl.BlockSpec((1, r, tm), lambda i: (i // nt, 0, i % nt))
    tiles = lambda h: pl.BlockSpec((1, h, 1, LANES, tm), lambda i: (i // nt, 0, i % nt, 0, 0))
    tok = lambda w, dt: jax.ShapeDtypeStruct((t, w), dt)
    fms = lambda r: jax.ShapeDtypeStruct((b, r, s), F32)
    til = lambda h: jax.ShapeDtypeStruct((b, h, nt, LANES, tm), BF16)
    outs = [(row(1024), tok(1024, BF16)), (tiles(MLA_HEADS), til(MLA_HEADS)), (tiles(MLA_HEADS), til(MLA_HEADS)),
            (row(256), tok(256, F32)), (fm(MLA_ROPE), fms(MLA_ROPE)), (row(1024), tok(1024, BF16)),
            (fm(256), fms(256)), (fm(256), fms(256)), (fm(256), fms(256)),
            (tiles(NSA_G), til(NSA_G)), (tiles(NSA_G), til(NSA_G)), (tiles(NSA_G), til(NSA_G)), (tiles(NSA_G), til(NSA_G)),
            (row(256), tok(256, F32)), (row(2048), tok(2048, F32))]
    ins = (x, g, w_in_p, w_kv_t, q_norm, w_uq_p, kv_norm, w_uk_t, w_uv_t)
    return pl.pallas_call(
        _proj_prompt_kernel,
        grid=(t // tm,),
        in_specs=[row(d)] + [full(a) for a in ins[1:]] + [tab, tab, tab],
        out_specs=[o[0] for o in outs],
        out_shape=[o[1] for o in outs],
        compiler_params=_cparams(1),
    )(*ins, rc, rs1, rs2)


def _mixer_project_sample(x, g, w_in_p, q_norm, w_uq_p, kv_norm, rc, rs1, rs2):
    t, d = x.shape
    tm = min(t, TILE)
    full = lambda a: pl.BlockSpec(a.shape, lambda i: (0,) * a.ndim)
    row = lambda w: pl.BlockSpec((tm, w), lambda i: (i, 0))
    widths = [(1024, BF16), (256, F32), (128, F32), (1024, BF16), (256, F32), (256, F32), (256, F32), (256, F32),
              (2048, F32)]
    ins = (x, g, w_in_p, q_norm, w_uq_p, kv_norm)
    return pl.pallas_call(
        _proj_sample_kernel,
        grid=(t // tm,),
        in_specs=[row(d)] + [full(a) for a in ins[1:]] + [row(LANES)] * 3,
        out_specs=[row(w) for w, _ in widths],
        out_shape=[jax.ShapeDtypeStruct((t, w), dt) for w, dt in widths],
        compiler_params=_cparams(1),
    )(*ins, rc, rs1, rs2)


def _norm_mm_kernel(x_ref, g_ref, w_ref, o_ref, *, scale):
    y = _dot(_rms(x_ref[...], g_ref[...]).astype(BF16), w_ref[...])
    if scale != 1.0:
        y = y * scale
    o_ref[...] = y.astype(o_ref.dtype)


def _norm_matmul(x, g, w, *, scale=1.0, out_dtype=F32):
    t, d = x.shape
    n = w.shape[1]
    tm = min(t, 512)
    return pl.pallas_call(
        functools.partial(_norm_mm_kernel, scale=scale),
        grid=(t // tm,),
        in_specs=[pl.BlockSpec((tm, d), lambda i: (i, 0)), pl.BlockSpec((1, d), lambda i: (0, 0)),
                  pl.BlockSpec((d, n), lambda i: (0, 0))],
        out_specs=pl.BlockSpec((tm, n), lambda i: (i, 0)),
        out_shape=jax.ShapeDtypeStruct((t, n), out_dtype),
        compiler_params=_cparams(1),
    )(x, g, w)


def _mm_kernel(a_ref, w_ref, o_ref):
    o_ref[...] = _dot(a_ref[...], w_ref[...]).astype(o_ref.dtype)


def _matmul(a, w, *, out_dtype=F32):
    t, k = a.shape
    n = w.shape[1]
    tm = min(t, 512)
    return pl.pallas_call(
        _mm_kernel,
        grid=(t // tm,),
        in_specs=[pl.BlockSpec((tm, k), lambda i: (i, 0)), pl.BlockSpec((k, n), lambda i: (0, 0))],
        out_specs=pl.BlockSpec((tm, n), lambda i: (i, 0)),
        out_shape=jax.ShapeDtypeStruct((t, n), out_dtype),
        compiler_params=_cparams(1),
    )(a, w)


def _mm_res_kernel(a_ref, w_ref, x_ref, o_ref):
    o_ref[...] = x_ref[...] + _dot(a_ref[...], w_ref[...])


def _matmul_residual(a, w, x):
    t, k = a.shape
    n = w.shape[1]
    tm = min(t, 512)
    return pl.pallas_call(
        _mm_res_kernel,
        grid=(t // tm,),
        in_specs=[pl.BlockSpec((tm, k), lambda i: (i, 0)), pl.BlockSpec((k, n), lambda i: (0, 0)),
                  pl.BlockSpec((tm, n), lambda i: (i, 0))],
        out_specs=pl.BlockSpec((tm, n), lambda i: (i, 0)),
        out_shape=jax.ShapeDtypeStruct((t, n), F32),
        compiler_params=_cparams(1),
    )(a, w, x)


def _merge_kernel(oa_ref, ob_ref, mg_ref, x_ref, wa_ref, wb_ref, wo_ref, o_ref):
    d = x_ref.shape[1]
    mix = mg_ref[:, :d] * _dot(oa_ref[...], wa_ref[...]) + mg_ref[:, d:] * _dot(ob_ref[...], wb_ref[...])
    o_ref[...] = x_ref[...] + _dot(mix.astype(BF16), wo_ref[...])


def _merge(o_a, o_b, mg, x, w_a, w_b, w_o):
    t, d = x.shape
    tm = min(t, 512)
    row = lambda w: pl.BlockSpec((tm, w), lambda i: (i, 0))
    full = lambda a: pl.BlockSpec(a.shape, lambda i: (0, 0))
    return pl.pallas_call(
        _merge_kernel,
        grid=(t // tm,),
        in_specs=[row(o_a.shape[1]), row(o_b.shape[1]), row(2 * d), row(d), full(w_a), full(w_b), full(w_o)],
        out_specs=row(d),
        out_shape=jax.ShapeDtypeStruct((t, d), F32),
        compiler_params=_cparams(1),
    )(o_a, o_b, mg, x, w_a, w_b, w_o)


MLA_HPS = 4


def _mla_prompt_kernel(q_ref, k_ref, v_ref, o_ref, m_ref, acc_ref, *, tq):
    i = pl.program_id(2)
    q = q_ref[0]
    qs = [q[:, h * LANES:(h + 1) * LANES] for h in range(MLA_HPS)]
    causal_ok = _iota((tq, tq), 1) <= _iota((tq, tq), 0)
    m_ref[...] = jnp.full(m_ref.shape, NEG_INF, F32)
    acc_ref[...] = jnp.zeros(acc_ref.shape, F32)

    def tile(j, causal):
        for h in range(MLA_HPS):
            s = _dot(qs[h], k_ref[0, h, j])
            if causal:
                s = jnp.where(causal_ok, s, -MASK_BIG)
            _flash_step(s, v_ref[0, h, j], m_ref, acc_ref, h, jnp.exp2)

    def body(j, c):
        tile(j, False)
        return c

    lax.fori_loop(0, i, body, 0)
    tile(i, True)
    lane = _iota((1, LANES), 1)
    outs = []
    for h in range(MLA_HPS):
        a = acc_ref[h]
        outs.append(a / pltpu.roll(a, MLA_V, 1))
    for p in range(MLA_HPS // 2):
        pair = jnp.where(lane < MLA_V, outs[2 * p], pltpu.roll(outs[2 * p + 1], MLA_V, 1))
        o_ref[0, :, p * LANES:(p + 1) * LANES] = pair.astype(o_ref.dtype)


def _mla_prompt(q, k_t, v_t):
    b, s, _ = q.shape
    tq = TILE
    nt = s // tq
    kv_spec = pl.BlockSpec((1, MLA_HPS, nt, LANES, tq), lambda bi, hq, i: (bi, hq, 0, 0, 0))
    return pl.pallas_call(
        functools.partial(_mla_prompt_kernel, tq=tq),
        grid=(b, MLA_HEADS // MLA_HPS, nt),
        in_specs=[pl.BlockSpec((1, tq, MLA_HPS * LANES), lambda bi, hq, i: (bi, i, hq)), kv_spec, kv_spec],
        out_specs=pl.BlockSpec((1, tq, MLA_HPS * MLA_V), lambda bi, hq, i: (bi, i, hq)),
        out_shape=jax.ShapeDtypeStruct((b, s, MLA_HEADS * MLA_V), BF16),
        scratch_shapes=[pltpu.VMEM((MLA_HPS, tq, LANES), F32), pltpu.VMEM((MLA_HPS, tq, LANES), F32)],
        compiler_params=_cparams(3),
    )(q, k_t, v_t)


def _split_hi_lo(x):
    hi = x.astype(BF16)
    return hi, (x - hi.astype(F32)).astype(BF16)


def _compress_prompt_kernel(x_ref, a_ref, phi_ref, o_ref):
    hi, lo = _split_hi_lo(x_ref[0])
    means = _dot(jnp.concatenate([hi, lo], axis=1), a_ref[...])
    o_ref[0] = _dot(phi_ref[...], means.astype(BF16)).astype(o_ref.dtype)


def _compress_prompt(cmp_t, avg2, phi_t):
    b, w, s = cmp_t.shape
    nc = avg2.shape[1]
    return pl.pallas_call(
        _compress_prompt_kernel,
        grid=(b,),
        in_specs=[pl.BlockSpec((1, w, s), lambda i: (i, 0, 0)), pl.BlockSpec(avg2.shape, lambda i: (0, 0)),
                  pl.BlockSpec(phi_t.shape, lambda i: (0, 0))],
        out_specs=pl.BlockSpec((1, w, nc), lambda i: (i, 0, 0)),
        out_shape=jax.ShapeDtypeStruct((b, w, nc), BF16),
        compiler_params=_cparams(1),
    )(cmp_t, avg2, phi_t)


def _nsa_prompt_kernel(q_ref, c_ref, ctab_ref, sk_ref, sv_ref, wk_ref, wv_ref, ktab_ref, gt_ref, o_ref,
                       m_ref, acc_ref, *, tq, n_sel):
    g = pl.program_id(1)
    i = pl.program_id(2)
    r = NSA_HPG * tq
    shift = tq.bit_length() - 1
    row = _iota((r, LANES), 0)
    lane_r = _iota((r, LANES), 1)
    qpos = i * tq + (row & (tq - 1))
    slope = _pow2_neg(NSA_HPG * g + (row >> shift) + 1)
    aq = _query_aug(slope, qpos, lane_r)
    q_st = jnp.concatenate([q_ref[0, :, h * LANES:(h + 1) * LANES] for h in range(NSA_HPG)], axis=0)
    q_plain = jnp.concatenate([q_st, aq.astype(BF16)], axis=1)

    kc = c_ref[0]
    nc = kc.shape[1]
    half = nc // 2
    lane_c = _iota((1, nc), 1)
    cblk = jnp.where(lane_c < half, 2 * lane_c, 2 * (lane_c - half) + 1)
    cmask = (cblk * CMP_BLOCK + (CMP_BLOCK - 1)) <= (i * tq + (_iota((r, nc), 0) & (tq - 1)))
    s = _dot(q_plain, jnp.concatenate([kc, ctab_ref[...]], axis=0))
    s = jnp.where(cmask, s, NEG_INF)
    mx = jnp.max(s, axis=-1, keepdims=True)
    mx = jnp.where(mx == NEG_INF, 0.0, mx)
    e = jnp.where(cmask, jnp.exp(s - mx), 0.0)
    p = e / jnp.maximum(jnp.sum(e, axis=-1, keepdims=True), 1e-30)
    o_cmp = _dot_nt(p.astype(BF16), kc)

    psum = p[0:tq] + p[tq:2 * tq] + p[2 * tq:3 * tq] + p[3 * tq:4 * tq]
    imp = psum + pltpu.roll(psum, half, 1)
    qp = i * tq + _iota((tq, 1), 0)
    forced = (lane_c == 0) | (lane_c == (qp >> 6))
    score = jnp.where(lane_c * SEL_BLOCK <= qp, imp + jnp.where(forced, FORCE_BONUS, 0.0), -1.0)
    st = jnp.transpose(score)[0:n_sel]
    jidx = _iota((n_sel, 1), 0)
    rank = jnp.zeros((n_sel, tq), F32)
    for ii in range(n_sel):
        ri = st[ii:ii + 1, :]
        first = jnp.where(jidx > ii, 1.0, 0.0)
        rank = rank + jnp.where(ri > st, 1.0, jnp.where(ri == st, first, 0.0))
    valid_t = (jidx * SEL_BLOCK) <= (i * tq + _iota((1, tq), 1))
    mt = jnp.where(valid_t, jnp.where(rank < float(min(SEL_TOPK, n_sel)), 1.0, 0.0), 0.0)
    if n_sel < LANES:
        mt = jnp.concatenate([mt, jnp.zeros((LANES - n_sel, tq), F32)], axis=0)
    msel = jnp.transpose(mt)
    mst = jnp.concatenate([msel] * NSA_HPG, axis=0)
    q_sel = jnp.concatenate([q_st, jnp.where(lane_r < n_sel, (mst - 1.0) * MASK_BIG, aq).astype(BF16)], axis=1)

    lane = _iota((1, LANES), 1)

    def finish():
        a = acc_ref[0]
        return a / pltpu.roll(a, NSA_DH, 1)

    m_ref[...] = jnp.full(m_ref.shape, NEG_INF, F32)
    acc_ref[...] = jnp.zeros(acc_ref.shape, F32)
    kpos_d = i * tq + _iota((r, tq), 1)
    qpos_d = i * tq + (_iota((r, tq), 0) & (tq - 1))

    def sel_tile(j, causal):
        sc = _dot(q_sel, jnp.concatenate([sk_ref[0, 0, j], ktab_ref[j]], axis=0))
        if causal:
            sc = jnp.where(kpos_d <= qpos_d, sc, -MASK_BIG)
        _flash_step(sc, sv_ref[0, 0, j], m_ref, acc_ref, 0, jnp.exp)

    def sel_body(j, c):
        sel_tile(j, False)
        return c

    lax.fori_loop(0, i, sel_body, 0)
    sel_tile(i, True)
    o_sel = finish()

    m_ref[...] = jnp.full(m_ref.shape, NEG_INF, F32)
    acc_ref[...] = jnp.zeros(acc_ref.shape, F32)

    def win_body(j, c):
        dist = qpos_d - (j * tq + _iota((r, tq), 1))
        sc = _dot(q_plain, jnp.concatenate([wk_ref[0, 0, j], ktab_ref[j]], axis=0))
        sc = jnp.where(dist >= 0, jnp.where(dist <= WINDOW, sc, -MASK_BIG), -MASK_BIG)
        _flash_step(sc, wv_ref[0, 0, j], m_ref, acc_ref, 0, jnp.exp)
        return c

    lax.fori_loop(jnp.maximum(i - WINDOW // tq, 0), i + 1, win_body, 0)
    o_win = finish()

    gt = gt_ref[0]
    heads = []
    for h in range(NSA_HPG):
        rs = slice(h * tq, (h + 1) * tq)
        heads.append(gt[:, 3 * h:3 * h + 1] * o_cmp[rs] + gt[:, 3 * h + 1:3 * h + 2] * o_sel[rs]
                     + gt[:, 3 * h + 2:3 * h + 3] * o_win[rs])
    for p2 in range(NSA_HPG // 2):
        pair = jnp.where(lane < NSA_DH, pltpu.roll(heads[2 * p2], NSA_DH, 1), heads[2 * p2 + 1])
        o_ref[0, :, p2 * LANES:(p2 + 1) * LANES] = pair.astype(o_ref.dtype)


def _nsa_prompt(q, ckcv_t, ctab, sk, sv, wk, wv, ktab, gates):
    b, s, _ = q.shape
    tq = TILE
    nt = s // tq
    nc = ckcv_t.shape[2]
    r = NSA_HPG * tq
    kv = pl.BlockSpec((1, 1, nt, LANES, tq), lambda bi, gi, i: (bi, gi, 0, 0, 0))
    return pl.pallas_call(
        functools.partial(_nsa_prompt_kernel, tq=tq, n_sel=s // SEL_BLOCK),
        grid=(b, NSA_G, nt),
        in_specs=[pl.BlockSpec((1, tq, NSA_HPG * LANES), lambda bi, gi, i: (bi, i, gi)),
                  pl.BlockSpec((1, LANES, nc), lambda bi, gi, i: (bi, gi, 0)),
                  pl.BlockSpec(ctab.shape, lambda bi, gi, i: (0, 0)),
                  kv, kv, kv, kv,
                  pl.BlockSpec(ktab.shape, lambda bi, gi, i: (0, 0, 0)),
                  pl.BlockSpec((1, tq, LANES), lambda bi, gi, i: (bi, i, gi))],
        out_specs=pl.BlockSpec((1, tq, NSA_HPG * NSA_DH), lambda bi, gi, i: (bi, i, gi)),
        out_shape=jax.ShapeDtypeStruct((b, s, NSA_HEADS * NSA_DH), BF16),
        scratch_shapes=[pltpu.VMEM((1, r, LANES), F32), pltpu.VMEM((1, r, LANES), F32)],
        compiler_params=_cparams(3),
    )(q, ckcv_t, ctab, sk, sv, wk, wv, ktab, gates)


def _xattn_prompt_kernel(q_ref, kv_ref, o_ref):
    hw = X_HEADS * X_DH
    for h in range(X_HEADS):
        cs = slice(h * X_DH, (h + 1) * X_DH)
        s = _dot_nt(q_ref[0, :, cs], kv_ref[0, :, cs])
        e = jnp.exp(s - jnp.max(s, axis=-1, keepdims=True))
        p = e / jnp.sum(e, axis=-1, keepdims=True)
        o_ref[0, :, cs] = _dot(p.astype(BF16), kv_ref[0, :, hw + h * X_DH:hw + (h + 1) * X_DH]).astype(o_ref.dtype)


def _xattn_prompt(q, kv):
    b, s, w = q.shape
    m = kv.shape[1]
    tq = min(s, 512)
    return pl.pallas_call(
        _xattn_prompt_kernel,
        grid=(b, s // tq),
        in_specs=[pl.BlockSpec((1, tq, w), lambda bi, i: (bi, i, 0)),
                  pl.BlockSpec((1, m, 2 * w), lambda bi, i: (bi, 0, 0))],
        out_specs=pl.BlockSpec((1, tq, w), lambda bi, i: (bi, i, 0)),
        out_shape=jax.ShapeDtypeStruct((b, s, w), BF16),
        compiler_params=_cparams(2),
    )(q, kv)


def _qabs_kernel(q_ref, w_ref, o_ref):
    o_ref[0] = _dot(q_ref[...], w_ref[0]).astype(o_ref.dtype)


def _mla_absorb_q(q_mla, w_abs):
    t = q_mla.shape[0]
    n = w_abs.shape[2]
    return pl.pallas_call(
        _qabs_kernel,
        grid=(MLA_HEADS,),
        in_specs=[pl.BlockSpec((t, LANES), lambda h: (0, h)), pl.BlockSpec((1, LANES, n), lambda h: (h, 0, 0))],
        out_specs=pl.BlockSpec((1, t, n), lambda h: (h, 0, 0)),
        out_shape=jax.ShapeDtypeStruct((MLA_HEADS, t, n), BF16),
        compiler_params=_cparams(1),
    )(q_mla, w_abs)


def _mla_sample_kernel(pt_ref, qa_ref, cn_ref, krn_ref, *refs, pg, nq):
    del pt_ref
    c_refs, kr_refs = refs[:pg], refs[pg:2 * pg]
    o_ref, m_ref, l_ref, acc_ref = refs[2 * pg:]
    jj = pl.program_id(1)
    qa = qa_ref[0]
    qc, qr = qa[:, :MLA_KV_LORA], qa[:, MLA_KV_LORA:MLA_KV_LORA + MLA_ROPE]

    @pl.when(jj == 0)
    def _():
        _col_softmax_init(m_ref, l_ref, acc_ref)

    c = jnp.concatenate([ref[0] for ref in c_refs], axis=0).astype(BF16)
    kr_t = jnp.concatenate([ref[0] for ref in kr_refs], axis=1).astype(BF16)
    _col_softmax_step(_dot_nt(qc, c) + _dot(qr, kr_t), c, m_ref, l_ref, acc_ref, jnp.exp2)

    @pl.when(jj == pl.num_programs(1) - 1)
    def _():
        cn = cn_ref[0].astype(BF16)
        s = _dot_nt(qc, cn) + _dot_nt(qr, krn_ref[0].astype(BF16))
        qi = _iota((qa.shape[0], 1), 0) & (nq - 1)
        s = jnp.where(_iota((1, cn.shape[0]), 1) <= qi, s, NEG_INF)
        _col_softmax_step(s, cn, m_ref, l_ref, acc_ref, jnp.exp2)
        o_ref[0] = acc_ref[...] / jnp.maximum(l_ref[...], 1e-30)


def _mla_sample(pt_flat, db, n_pages, qa, c_new, kr_new, ckv_pool, kr_pool_t, *, nq):
    pg = min(n_pages, 16)
    r = qa.shape[1]

    def page_spec(shape, p):
        return pl.BlockSpec(shape, lambda b, j, pt: (pt[(j * pg + p) * db + b], 0, 0))

    per_b = lambda a: pl.BlockSpec((1,) + a.shape[1:], lambda b, j, pt: (b, 0, 0))
    grid_spec = pltpu.PrefetchScalarGridSpec(
        num_scalar_prefetch=1,
        grid=(db, n_pages // pg),
        in_specs=[per_b(qa), per_b(c_new), per_b(kr_new)]
        + [page_spec((1, PAGE, MLA_KV_LORA), p) for p in range(pg)]
        + [page_spec((1, MLA_ROPE, PAGE), p) for p in range(pg)],
        out_specs=pl.BlockSpec((1, r, MLA_KV_LORA), lambda b, j, pt: (b, 0, 0)),
        scratch_shapes=[pltpu.VMEM((r, 1), F32), pltpu.VMEM((r, 1), F32), pltpu.VMEM((r, MLA_KV_LORA), F32)],
    )
    return pl.pallas_call(
        functools.partial(_mla_sample_kernel, pg=pg, nq=nq),
        grid_spec=grid_spec,
        out_shape=jax.ShapeDtypeStruct((db, r, MLA_KV_LORA), F32),
        compiler_params=_cparams(2),
    )(pt_flat, qa, c_new, kr_new, *([ckv_pool] * pg), *([kr_pool_t] * pg))


def _cmp_sample_kernel(pt_ref, a_ref, phi_ref, *refs, pg):
    del pt_ref
    o_ref = refs[pg]
    x = jnp.concatenate([ref[0] for ref in refs[:pg]], axis=1)
    hi, lo = _split_hi_lo(x)
    means = _dot(jnp.concatenate([hi, lo], axis=1), a_ref[...])
    o_ref[0] = _dot(phi_ref[...], means.astype(BF16)).astype(o_ref.dtype)


def _compress_sample(pt_flat, db, n_pages, cmp_pool_t, avg2, phi_t):
    pg = avg2.shape[0] // (2 * PAGE)
    w = cmp_pool_t.shape[1]
    ncs = avg2.shape[1]
    grid_spec = pltpu.PrefetchScalarGridSpec(
        num_scalar_prefetch=1,
        grid=(db, n_pages // pg),
        in_specs=[pl.BlockSpec(avg2.shape, lambda b, j, pt: (0, 0)), pl.BlockSpec(phi_t.shape, lambda b, j, pt: (0, 0))]
        + [pl.BlockSpec((1, w, PAGE), lambda b, j, pt, p=p: (pt[(j * pg + p) * db + b], 0, 0)) for p in range(pg)],
        out_specs=pl.BlockSpec((1, w, ncs), lambda b, j, pt: (b, 0, j)),
    )
    return pl.pallas_call(
        functools.partial(_cmp_sample_kernel, pg=pg),
        grid_spec=grid_spec,
        out_shape=jax.ShapeDtypeStruct((db, w, (n_pages // pg) * ncs), BF16),
        compiler_params=_cparams(2),
    )(pt_flat, avg2, phi_t, *([cmp_pool_t] * pg))


def _nsa_cmp_sample_kernel(q_ref, c_ref, ctab_ref, o_ref, imp_ref, *, nq, past):
    r = q_ref.shape[2]
    nc = c_ref.shape[2]
    row = _iota((r, LANES), 0)
    lane_r = _iota((r, LANES), 1)
    qpos = past + (row & (nq - 1))
    cmask = (_iota((1, nc), 1) * CMP_BLOCK + (CMP_BLOCK - 1)) <= (past + (_iota((r, nc), 0) & (nq - 1)))
    for g in range(NSA_G):
        slope = _pow2_neg(NSA_HPG * g + (row >> (nq.bit_length() - 1)) + 1)
        q2 = jnp.concatenate([q_ref[0, g], _query_aug(slope, qpos, lane_r).astype(BF16)], axis=1)
        kc = c_ref[0, g * LANES:(g + 1) * LANES, :]
        s = _dot(q2, jnp.concatenate([kc, ctab_ref[...]], axis=0))
        s = jnp.where(cmask, s, NEG_INF)
        mx = jnp.max(s, axis=-1, keepdims=True)
        mx = jnp.where(mx == NEG_INF, 0.0, mx)
        e = jnp.where(cmask, jnp.exp(s - mx), 0.0)
        p = e / jnp.maximum(jnp.sum(e, axis=-1, keepdims=True), 1e-30)
        o_ref[0, g] = _dot_nt(p.astype(BF16), kc)
        ps = p
        for h in range(1, NSA_HPG):
            ps = ps + pltpu.roll(p, h * nq, 0)
        imp_ref[0, g] = ps + pltpu.roll(ps, nc - 1, 1)


def _nsa_cmp_sample(q, ckcv_t, ctab, *, nq, past):
    db, g, r, _ = q.shape
    nc = ckcv_t.shape[2]
    return pl.pallas_call(
        functools.partial(_nsa_cmp_sample_kernel, nq=nq, past=past),
        grid=(db,),
        in_specs=[pl.BlockSpec((1,) + q.shape[1:], lambda b: (b, 0, 0, 0)),
                  pl.BlockSpec((1,) + ckcv_t.shape[1:], lambda b: (b, 0, 0)),
                  pl.BlockSpec(ctab.shape, lambda b: (0, 0))],
        out_specs=[pl.BlockSpec((1, g, r, LANES), lambda b: (b, 0, 0, 0)),
                   pl.BlockSpec((1, g, r, nc), lambda b: (b, 0, 0, 0))],
        out_shape=[jax.ShapeDtypeStruct((db, g, r, LANES), F32), jax.ShapeDtypeStruct((db, g, r, nc), F32)],
        compiler_params=_cparams(1),
    )(q, ckcv_t, ctab)


def _topk_sample_kernel(imp_ref, o_ref, *, n_pick, last_blk):
    s = imp_ref[...]
    lane_i = _iota(s.shape, 1)
    lane = lane_i.astype(F32)
    s = jnp.where(((lane_i & 1) == 0) & (lane_i > 0), s, -1.0)
    olane = _iota(o_ref.shape, 1)
    out = jnp.where(olane == n_pick + 1, float(last_blk), 0.0)
    for it in range(n_pick):
        mx = jnp.max(s, axis=-1, keepdims=True)
        idx = jnp.min(jnp.where(s == mx, lane, float(s.shape[1])), axis=-1, keepdims=True)
        s = jnp.where(lane == idx, -1.0, s)
        out = jnp.where(olane == it, idx * 0.5, out)
    o_ref[...] = out.astype(I32)


def _topk_sample(imp, *, n_pick, last_blk):
    rows = imp.shape[0]
    return pl.pallas_call(
        functools.partial(_topk_sample_kernel, n_pick=n_pick, last_blk=last_blk),
        grid=(1,),
        in_specs=[pl.BlockSpec(imp.shape, lambda i: (0, 0))],
        out_specs=pl.BlockSpec((rows, LANES), lambda i: (0, 0)),
        out_shape=jax.ShapeDtypeStruct((rows, LANES), I32),
        compiler_params=_cparams(1),
    )(imp)


def _nsa_sel_sample_kernel(pt_ref, ids_ref, q_ref, new_ref, *refs, nblk, nq, past):
    del pt_ref
    o_ref = refs[nblk]
    b = pl.program_id(0)
    g = pl.program_id(1)
    qi = pl.program_id(2)
    q = q_ref[0, 0, 0]
    nrow = q.shape[0]
    base = ((b * NSA_G + g) * nq + qi) * SEL_TOPK
    lane = _iota((1, LANES), 1)
    kv_t = jnp.concatenate([ref[0] for ref in refs[:nblk]], axis=1).astype(BF16)
    kpos, keep = [], []
    for c in range(nblk):
        blk = ids_ref[base + c]
        kpos.append((blk >> 1) * PAGE + lane)
        keep.append((lane >> 6) == (blk & 1))
    kpos = jnp.concatenate(kpos, axis=1)
    keep = jnp.concatenate(keep, axis=1)
    qpos = past + qi
    slope = _pow2_neg(NSA_HPG * g + jnp.minimum(_iota((nrow, 1), 0), NSA_HPG - 1) + 1)
    s1 = _dot(q, kv_t) - slope * (qpos - kpos).astype(F32)
    s1 = jnp.where(keep, s1, NEG_INF)
    nw = new_ref[0].astype(BF16)
    tnew = _iota((1, nw.shape[0]), 1)
    s2 = _dot_nt(q, nw) - slope * (qi - tnew).astype(F32)
    s2 = jnp.where(tnew <= qi, s2, NEG_INF)
    mx = jnp.maximum(jnp.max(s1, axis=-1, keepdims=True), jnp.max(s2, axis=-1, keepdims=True))
    e1 = jnp.exp(s1 - mx)
    e2 = jnp.exp(s2 - mx)
    den = jnp.sum(e1, axis=-1, keepdims=True) + jnp.sum(e2, axis=-1, keepdims=True)
    o_ref[0, 0, 0] = (_dot_nt(e1.astype(BF16), kv_t) + _dot(e2.astype(BF16), nw)) / den


def _nsa_sel_sample(pt_flat, ids, db, n_pages, q, sel_new, sel_pool_t, *, nblk, nq, past):
    blocks_per_page = PAGE // SEL_BLOCK

    def blk_spec(c):
        def index(b, g, qi, pt, idr):
            blk = idr[((b * NSA_G + g) * nq + qi) * SEL_TOPK + c]
            return (pt[(blk // blocks_per_page) * db + b], g, 0)
        return pl.BlockSpec((1, LANES, PAGE), index)

    grid_spec = pltpu.PrefetchScalarGridSpec(
        num_scalar_prefetch=2,
        grid=(db, NSA_G, nq),
        in_specs=[pl.BlockSpec((1, 1, 1) + q.shape[3:], lambda b, g, qi, pt, idr: (b, g, qi, 0, 0)),
                  pl.BlockSpec((1, sel_new.shape[1], LANES), lambda b, g, qi, pt, idr: (b, 0, g))]
        + [blk_spec(c) for c in range(nblk)],
        out_specs=pl.BlockSpec((1, 1, 1, q.shape[3], LANES), lambda b, g, qi, pt, idr: (b, g, qi, 0, 0)),
    )
    return pl.pallas_call(
        functools.partial(_nsa_sel_sample_kernel, nblk=nblk, nq=nq, past=past),
        grid_spec=grid_spec,
        out_shape=jax.ShapeDtypeStruct((db, NSA_G, nq, q.shape[3], LANES), F32),
        compiler_params=_cparams(3),
    )(pt_flat, ids.reshape(-1), q, sel_new, *([sel_pool_t] * nblk))


def _nsa_win_sample_kernel(q_ref, st_ref, wtab_ref, new_ref, o_ref, *, nq, past):
    r = q_ref.shape[2]
    wbuf = st_ref.shape[2]
    row = _iota((r, LANES), 0)
    lane_r = _iota((r, LANES), 1)
    qpos = past + (row & (nq - 1))
    qi = _iota((r, 1), 0) & (nq - 1)
    dist_st = qi + wbuf - _iota((1, wbuf), 1)
    tnew = _iota((1, new_ref.shape[1]), 1)
    for g in range(NSA_G):
        slope = _pow2_neg(NSA_HPG * g + (row >> (nq.bit_length() - 1)) + 1)
        q = q_ref[0, g]
        q2 = jnp.concatenate([q, _query_aug(slope, qpos, lane_r).astype(BF16)], axis=1)
        st = st_ref[0, g * LANES:(g + 1) * LANES, :].astype(BF16)
        nw = new_ref[0, :, g * LANES:(g + 1) * LANES].astype(BF16)
        s1 = _dot(q2, jnp.concatenate([st, wtab_ref[...]], axis=0))
        s1 = jnp.where(dist_st <= WINDOW, s1, NEG_INF)
        s2 = _dot_nt(q, nw) - slope[:, :1] * (qi - tnew).astype(F32)
        s2 = jnp.where(tnew <= qi, s2, NEG_INF)
        mx = jnp.maximum(jnp.max(s1, axis=-1, keepdims=True), jnp.max(s2, axis=-1, keepdims=True))
        e1 = jnp.exp(s1 - mx)
        e2 = jnp.exp(s2 - mx)
        den = jnp.sum(e1, axis=-1, keepdims=True) + jnp.sum(e2, axis=-1, keepdims=True)
        o_ref[0, g] = (_dot_nt(e1.astype(BF16), st) + _dot(e2.astype(BF16), nw)) / den


def _nsa_win_sample(q, state_t, wtab, win_new, *, nq, past):
    db, g, r, _ = q.shape
    return pl.pallas_call(
        functools.partial(_nsa_win_sample_kernel, nq=nq, past=past),
        grid=(db,),
        in_specs=[pl.BlockSpec((1,) + q.shape[1:], lambda b: (b, 0, 0, 0)),
                  pl.BlockSpec((1,) + state_t.shape[1:], lambda b: (b, 0, 0)),
                  pl.BlockSpec(wtab.shape, lambda b: (0, 0)),
                  pl.BlockSpec((1,) + win_new.shape[1:], lambda b: (b, 0, 0))],
        out_specs=pl.BlockSpec((1, g, r, LANES), lambda b: (b, 0, 0, 0)),
        out_shape=jax.ShapeDtypeStruct((db, g, r, LANES), F32),
        compiler_params=_cparams(1),
    )(q, state_t, wtab, win_new)


def _gate_kernel(g_ref, a_ref, b_ref, c_ref, o_ref):
    o_ref[...] = (g_ref[0] * a_ref[...] + g_ref[1] * b_ref[...] + g_ref[2] * c_ref[...]).astype(o_ref.dtype)


def _nsa_gate_sample(gexp, o_cmp, o_sel, o_win):
    t, w = o_cmp.shape
    row = pl.BlockSpec((t, w), lambda i: (0, 0))
    return pl.pallas_call(
        _gate_kernel,
        grid=(1,),
        in_specs=[pl.BlockSpec((3, t, w), lambda i: (0, 0, 0)), row, row, row],
        out_specs=row,
        out_shape=jax.ShapeDtypeStruct((t, w), BF16),
        compiler_params=_cparams(1),
    )(gexp, o_cmp, o_sel, o_win)


def _xattn_sample_kernel(q_ref, kv_ref, o_ref, *, bc, n_mem):
    stride = 2 * X_HEADS
    for bi in range(bc):
        for h in range(X_HEADS):
            cs = slice(h * X_DH, (h + 1) * X_DH)
            k = kv_ref[bi, pl.ds(h, n_mem, stride=stride), :].astype(BF16)
            v = kv_ref[bi, pl.ds(X_HEADS + h, n_mem, stride=stride), :].astype(BF16)
            s = _dot_nt(q_ref[bi, :, cs], k)
            e = jnp.exp(s - jnp.max(s, axis=-1, keepdims=True))
            p = e / jnp.sum(e, axis=-1, keepdims=True)
            o_ref[bi, :, cs] = _dot(p.astype(BF16), v).astype(o_ref.dtype)


def _xattn_sample(q, kv, *, n_mem):
    db, r, w = q.shape
    bc = 4 if db % 4 == 0 else 1
    return pl.pallas_call(
        functools.partial(_xattn_sample_kernel, bc=bc, n_mem=n_mem),
        grid=(db // bc,),
        in_specs=[pl.BlockSpec((bc, r, w), lambda i: (i, 0, 0)),
                  pl.BlockSpec((bc,) + kv.shape[1:], lambda i: (i, 0, 0))],
        out_specs=pl.BlockSpec((bc, r, w), lambda i: (i, 0, 0)),
        out_shape=jax.ShapeDtypeStruct((db, r, w), BF16),
        compiler_params=_cparams(1),
    )(q, kv)


def _prep_weights(w_in, mla_w_uq, mla_w_uk, mla_w_uv, nsa_phi_k, nsa_phi_v):
    d = w_in.shape[0]
    z = lambda n: jnp.zeros((d, n), F32)
    o = 0
    cq, o = w_in[:, o:o + 384], o + 384
    ckv, o = w_in[:, o:o + 256], o + 256
    kr, o = w_in[:, o:o + 32], o + 32
    qn, o = w_in[:, o:o + 512], o + 512
    kvn, o = w_in[:, o:o + 768], o + 768
    gn, o = w_in[:, o:o + 24], o + 24
    mg = w_in[:, o:]
    qn_p = jnp.pad(qn.reshape(d, NSA_HEADS, NSA_DH), ((0, 0), (0, 0), (0, LANES - NSA_DH))).reshape(d, -1)
    w_in_p = jnp.concatenate([cq, ckv, z(64), kr, z(32), qn_p, gn[:, :12], z(116), gn[:, 12:], z(116), mg, kvn],
                             axis=1).astype(BF16)
    w_kv_t = jnp.transpose(kvn).astype(BF16)
    w_uq_p = jnp.pad(mla_w_uq, ((0, 0), (0, 0), (0, LANES - MLA_NOPE - MLA_ROPE))).reshape(MLA_Q_LORA, -1).astype(BF16)
    pad_t = lambda w: jnp.pad(jnp.transpose(w, (1, 2, 0)), ((0, 0), (0, LANES - w.shape[2]), (0, 0))).reshape(
        MLA_HEADS * LANES, MLA_KV_LORA).astype(BF16)
    w_uk_t, w_uv_t = pad_t(mla_w_uk), pad_t(mla_w_uv)
    w_abs = jnp.zeros((MLA_HEADS, LANES, 384), F32)
    w_abs = w_abs.at[:, :MLA_NOPE, :MLA_KV_LORA].set(jnp.transpose(mla_w_uk, (1, 2, 0)))
    w_abs = w_abs.at[:, MLA_NOPE:MLA_NOPE + MLA_ROPE, MLA_KV_LORA:MLA_KV_LORA + MLA_ROPE].set(jnp.eye(MLA_ROPE, dtype=F32))
    w_ov = jnp.zeros((MLA_HEADS, MLA_KV_LORA, MLA_HEADS, MLA_V), F32)
    for h in range(MLA_HEADS):
        w_ov = w_ov.at[h, :, h, :].set(mla_w_uv[:, h, :])
    w_ov = w_ov.reshape(MLA_HEADS * MLA_KV_LORA, MLA_HEADS * MLA_V)
    phi_t = jnp.zeros((4, NSA_DH, 4, NSA_DH), F32)
    for g in range(NSA_G):
        phi_t = phi_t.at[2 * g, :, 2 * g, :].set(jnp.transpose(nsa_phi_k[g]))
        phi_t = phi_t.at[2 * g + 1, :, 2 * g + 1, :].set(jnp.transpose(nsa_phi_v[g]))
    phi_t = phi_t.reshape(4 * NSA_DH, 4 * NSA_DH)
    return w_in_p, w_kv_t, w_uq_p, w_uk_t, w_uv_t, w_abs.astype(BF16), w_ov.astype(BF16), phi_t.astype(BF16)


def _rope_tables(pos):
    half = MLA_ROPE // 2
    inv = ROPE_BASE ** (-jnp.arange(half, dtype=F32) / half)
    ang = pos.astype(F32)[:, None] * inv[None, :]
    cos, sin = jnp.cos(ang), jnp.sin(ang)
    n = pos.shape[0]
    one, zero = jnp.ones((n, MLA_NOPE), F32), jnp.zeros((n, half), F32)
    tail = jnp.zeros((n, LANES - MLA_NOPE - MLA_ROPE), F32)
    rc = jnp.concatenate([one, cos, cos, tail], axis=1)
    rs1 = jnp.concatenate([0 * one, -sin, zero, tail], axis=1)
    rs2 = jnp.concatenate([0 * one, zero, sin, tail], axis=1)
    return rc, rs1, rs2


def _key_rows(pos):
    rows = jnp.arange(LANES, dtype=I32)[:, None]
    blk = (pos >> 6)[None, :]
    t = jnp.where((rows == blk) & (rows < SEL_BLOCK), 1.0, 0.0)
    t = jnp.where(rows == A_BLK, blk.astype(F32), t)
    t = jnp.where(rows == A_OFF, (pos & (SEL_BLOCK - 1))[None, :].astype(F32), t)
    t = jnp.where((rows == A_ONE0) | (rows == A_ONE1), 1.0, t)
    return t.astype(BF16)


def _avg_matrix(n_keys, order):
    blk_of_key = jnp.tile(jnp.arange(n_keys, dtype=I32) // CMP_BLOCK, 2)[:, None]
    return jnp.where(blk_of_key == order[None, :], 1.0 / CMP_BLOCK, 0.0).astype(BF16)


def _even_odd_order(n):
    return jnp.concatenate([jnp.arange(0, n, 2, dtype=I32), jnp.arange(1, n, 2, dtype=I32)])


def kernel(x_prompt, x_sample, mem_prompt, cache_mla_ckv, cache_mla_krope, cache_nsa_cmp_kv, cache_nsa_sel_kv,
           state_nsa_win_kv, cache_xattn_kv, page_table, ffn1_norm, ffn1_w_gate, ffn1_w_up, ffn1_w_down, mix_norm,
           w_in, mla_q_norm, mla_w_uq, mla_kv_norm, mla_w_uk, mla_w_uv, nsa_phi_k, nsa_phi_v, w_br_mla, w_br_nsa,
           w_out, xattn_norm, xattn_mem_norm, xattn_w_q, xattn_w_kv, xattn_w_o, ffn2_norm, ffn2_w_gate, ffn2_w_up,
           ffn2_w_down, final_norm):
    assert x_prompt.shape[2] == D_MODEL and ffn1_norm.shape[0] == 1
    b, s, d = x_prompt.shape
    db, nq, _ = x_sample.shape
    n_pages = page_table.shape[1]
    past = n_pages * PAGE
    n_mem = mem_prompt.shape[1]
    assert s % TILE == 0 and past % SEL_BLOCK == 0 and nq <= 8
    g2 = lambda a: a.reshape(1, -1)
    bf = lambda a: a.astype(BF16)

    w_in_p, w_kv_t, w_uq_p, w_uk_t, w_uv_t, w_abs, w_ov, phi_t = _prep_weights(
        w_in[0], mla_w_uq[0], mla_w_uk[0], mla_w_uv[0], nsa_phi_k[0], nsa_phi_v[0])
    f1 = (g2(ffn1_norm[0]), bf(ffn1_w_gate[0]), bf(ffn1_w_up[0]), bf(ffn1_w_down[0]))
    f2 = (g2(ffn2_norm[0]), bf(ffn2_w_gate[0]), bf(ffn2_w_up[0]), bf(ffn2_w_down[0]))
    fg = g2(final_norm)
    w_a, w_b, w_o = bf(w_br_mla[0]), bf(w_br_nsa[0]), bf(w_out[0])
    w_xq, w_xo = bf(xattn_w_q[0]), bf(xattn_w_o[0])
    w_xkv = bf(xattn_w_kv[0].reshape(d, -1))
    mixg, qng, kvg = g2(mix_norm[0]), g2(mla_q_norm[0]), g2(mla_kv_norm[0])

    t = b * s
    nt = s // TILE
    hp = _ffn(x_prompt.reshape(t, d), *f1, fg, final=False)
    (q_mla, k_t, v_t, p_ckv, p_kr_t, q_nsa, p_cmp_t, p_sel_t, p_win_t, sk, sv, wk, wv, gates, merge) = (
        _mixer_project_prompt(hp, b, s, mixg, w_in_p[:, :C_KVN], w_kv_t, qng, w_uq_p, kvg, w_uk_t, w_uv_t,
                              *_rope_tables(jnp.arange(s))))
    o_a = _mla_prompt(q_mla.reshape(b, s, -1), k_t, v_t).reshape(t, -1)

    nc = s // CMP_BLOCK
    order = _even_odd_order(nc)
    ckcv_t = _compress_prompt(p_cmp_t, _avg_matrix(s, order), phi_t)
    ctab = _key_rows(order * CMP_BLOCK + (CMP_BLOCK - 1))
    ktab = jnp.transpose(_key_rows(jnp.arange(s, dtype=I32)).reshape(LANES, nt, TILE), (1, 0, 2))
    o_b = _nsa_prompt(q_nsa.reshape(b, s, -1), ckcv_t, ctab, sk, sv, wk, wv, ktab, gates.reshape(b, s, -1)).reshape(t, -1)
    hp = _merge(o_a, o_b, merge, hp, w_a, w_b, w_o)

    kv_mem = _norm_matmul(mem_prompt.reshape(b * n_mem, d), g2(xattn_mem_norm[0]), w_xkv)
    xq = _norm_matmul(hp, g2(xattn_norm[0]), w_xq, scale=X_SCALE, out_dtype=BF16)
    xo = _xattn_prompt(xq.reshape(b, s, -1), bf(kv_mem).reshape(b, n_mem, -1))
    hp = _matmul_residual(xo.reshape(t, -1), w_xo, hp)
    y_prompt = _ffn(hp, *f2, fg, final=True).reshape(b, s, d)

    wlen = min(WINDOW, s)
    kv_out = lambda a: jnp.transpose(a.reshape(1, b, NSA_G, 2, NSA_DH, a.shape[-1]), (0, 1, 5, 2, 3, 4))
    prompt_caches = (p_ckv.reshape(1, b, s, -1), jnp.transpose(p_kr_t, (0, 2, 1))[None], kv_out(p_cmp_t),
                     kv_out(p_sel_t), kv_out(p_win_t[:, :, s - wlen:]),
                     kv_mem.reshape(1, b, n_mem, 2, X_HEADS, X_DH))

    ts = db * nq
    hs = _ffn(x_sample.reshape(ts, d), *f1, fg, final=False)
    pos_s = jnp.tile(past + jnp.arange(nq), db)
    (q_mla, s_ckv, s_krb, q_nsa, s_cmp, s_sel, s_win, gates, merge) = _mixer_project_sample(
        hs, mixg, w_in_p, qng, w_uq_p, kvg, *_rope_tables(pos_s))
    s_kr = s_krb[:, MLA_NOPE:MLA_NOPE + MLA_ROPE]
    pad_rows = lambda a: jnp.pad(a.reshape(db, nq, -1), ((0, 0), (0, 8 - nq), (0, 0)))
    pt_flat = jnp.transpose(page_table).reshape(-1)

    qa = _mla_absorb_q(q_mla, w_abs)
    qa = jnp.transpose(qa.reshape(MLA_HEADS, db, nq, -1), (1, 0, 2, 3)).reshape(db, MLA_HEADS * nq, -1)
    kr_pool_t = jnp.transpose(cache_mla_krope[0], (0, 2, 1))
    o_lat = _mla_sample(pt_flat, db, n_pages, qa, pad_rows(s_ckv), pad_rows(s_kr), cache_mla_ckv[0], kr_pool_t, nq=nq)
    o_lat = jnp.transpose(o_lat.reshape(db, MLA_HEADS, nq, -1), (0, 2, 1, 3)).reshape(ts, -1)
    o_a = _matmul(bf(o_lat), w_ov, out_dtype=BF16)

    fm_pool = lambda c: jnp.transpose(c[0], (0, 2, 3, 4, 1)).reshape(c.shape[1], -1, PAGE)
    cmp_pool_t, sel_pool_t = fm_pool(cache_nsa_cmp_kv), fm_pool(cache_nsa_sel_kv)
    pg_cmp = min(n_pages, 32)
    blocks_per_step = pg_cmp * PAGE // CMP_BLOCK
    ckcv_s = _compress_sample(pt_flat, db, n_pages, cmp_pool_t,
                              _avg_matrix(pg_cmp * PAGE, jnp.arange(blocks_per_step, dtype=I32)), phi_t)
    ncs = past // CMP_BLOCK
    ctab_s = _key_rows(jnp.arange(ncs, dtype=I32) * CMP_BLOCK + (CMP_BLOCK - 1))
    qn = jnp.transpose(q_nsa.reshape(db, nq, NSA_G, NSA_HPG, LANES), (0, 2, 3, 1, 4))
    q_rows = qn.reshape(db, NSA_G, NSA_HPG * nq, LANES)
    o_cmp, imp = _nsa_cmp_sample(q_rows, ckcv_s, ctab_s, nq=nq, past=past)
    n_sel = -(-(past + nq) // SEL_BLOCK)
    n_pick = min(SEL_TOPK, n_sel) - 2
    ids = _topk_sample(imp[:, :, :nq].reshape(db * NSA_G * nq, -1), n_pick=n_pick, last_blk=n_sel - 1)
    q_sel = jnp.pad(jnp.transpose(qn, (0, 1, 3, 2, 4)), ((0, 0), (0, 0), (0, 0), (0, 8 - NSA_HPG), (0, 0)))
    o_sel = _nsa_sel_sample(pt_flat, ids[:, :SEL_TOPK], db, n_pages, q_sel, pad_rows(s_sel), sel_pool_t,
                            nblk=n_pick + 1, nq=nq, past=past)
    wbuf = state_nsa_win_kv.shape[2]
    state_t = jnp.transpose(state_nsa_win_kv[0], (0, 2, 3, 4, 1)).reshape(db, -1, wbuf)
    wtab = _key_rows(past - wbuf + jnp.arange(wbuf, dtype=I32))
    o_win = _nsa_win_sample(q_rows, state_t, wtab, pad_rows(s_win), nq=nq, past=past)
    to_tok = lambda a: jnp.transpose(a[..., NSA_DH:].reshape(db, NSA_G, NSA_HPG, nq, NSA_DH), (0, 3, 1, 2, 4)).reshape(ts, -1)
    o_sel_t = jnp.transpose(o_sel[:, :, :, :NSA_HPG, NSA_DH:], (0, 2, 1, 3, 4)).reshape(ts, -1)
    gts = jnp.stack([gates[:, :12], gates[:, LANES:LANES + 12]], axis=1).reshape(ts, NSA_HEADS, 3)
    gexp = jnp.transpose(jnp.broadcast_to(gts[:, :, :, None], (ts, NSA_HEADS, 3, NSA_DH)), (2, 0, 1, 3)).reshape(3, ts, -1)
    o_b = _nsa_gate_sample(gexp, to_tok(o_cmp), o_sel_t, to_tok(o_win))
    hs = _merge(o_a, o_b, merge, hs, w_a, w_b, w_o)

    xq = _norm_matmul(hs, g2(xattn_norm[0]), w_xq, scale=X_SCALE, out_dtype=BF16)
    xo = _xattn_sample(pad_rows(xq), cache_xattn_kv[0].reshape(db, -1, X_DH), n_mem=n_mem)
    hs = _matmul_residual(xo[:, :nq].reshape(ts, -1), w_xo, hs)
    y_sample = _ffn(hs, *f2, fg, final=True).reshape(db, nq, d)

    kvs = (1, db, nq, NSA_G, 2, NSA_DH)
    win_upd = jnp.concatenate([state_nsa_win_kv[0], s_win.reshape(kvs[1:])], axis=1)[:, nq:]
    sample_caches = (s_ckv.reshape(1, db, nq, -1), s_kr.reshape(1, db, nq, -1), s_cmp.reshape(kvs),
                     s_sel.reshape(kvs), win_upd[None])
    return (y_prompt, y_sample) + prompt_caches + sample_caches
```

```python
import functools
import math

import jax
import jax.numpy as jnp
from jax import lax
from jax.experimental import pallas as pl
from jax.experimental.pallas import tpu as pltpu

F32 = jnp.float32
BF16 = jnp.bfloat16
I32 = jnp.int32

D_MODEL = 1024
EPS = 1e-6
PAGE = 128
MLA_HEADS = 8
MLA_Q_LORA = 384
MLA_KV_LORA = 256
MLA_NOPE = 64
MLA_ROPE = 32
MLA_V = 64
MLA_SCALE = (MLA_NOPE + MLA_ROPE) ** -0.5
ROPE_BASE = 10000.0
NSA_HEADS = 8
NSA_G = 2
NSA_HPG = 4
NSA_DH = 64
NSA_SCALE = NSA_DH ** -0.5
CMP_BLOCK = 32
SEL_BLOCK = 64
SEL_TOPK = 16
WINDOW = 512
FORCE_BONUS = 4.0 * NSA_HPG
X_HEADS = 4
X_DH = 128
X_SCALE = X_DH ** -0.5
LANES = 128
LOG2E = math.log2(math.e)
NEG_INF = float("-inf")
MASK_BIG = 2.0 ** 60
VMEM_LIMIT = 56 * 1024 * 1024
TILE = 256
QTILE = 512

C_CQ = 0
C_CKV = 384
C_KR = 640
C_QN = 768
C_GT = 1792
C_MG = 2048
C_KVN = 4096
C_END = 4864
A_BLK, A_OFF, A_ONE0, A_ONE1 = 64, 65, 66, 67


def _cparams(n_axes):
    return pltpu.CompilerParams(dimension_semantics=("arbitrary",) * n_axes, vmem_limit_bytes=VMEM_LIMIT)


def _dot(a, b):
    return jnp.dot(a, b, preferred_element_type=F32)


def _dot_nt(a, b):
    return lax.dot_general(a, b, (((1,), (1,)), ((), ())), preferred_element_type=F32)


def _rms(x, g):
    ms = jnp.mean(x * x, axis=-1, keepdims=True)
    return x * lax.rsqrt(ms + EPS) * g


def _iota(shape, dim):
    return lax.broadcasted_iota(I32, shape, dim)


def _pow2_neg(e):
    return lax.bitcast_convert_type((127 - e) << 23, F32)


def _query_aug(slope, qpos, lane):
    s64 = slope * float(SEL_BLOCK)
    return jnp.where(lane == A_BLK, s64,
           jnp.where(lane == A_OFF, slope,
           jnp.where(lane == A_ONE0, -s64 * (qpos >> 6).astype(F32),
           jnp.where(lane == A_ONE1, -slope * (qpos & (SEL_BLOCK - 1)).astype(F32), 0.0))))


def _flash_step(s, v_t, m_ref, acc_ref, idx, exp_fn):
    m_prev = m_ref[idx]
    m_new = jnp.maximum(m_prev, jnp.max(s, axis=-1, keepdims=True))
    alpha = exp_fn(m_prev - m_new)
    p = exp_fn(s - jnp.concatenate([m_new] * (s.shape[1] // LANES), axis=1))
    acc_ref[idx] = alpha * acc_ref[idx] + _dot_nt(p.astype(BF16), v_t)
    m_ref[idx] = m_new


def _ffn_kernel(x_ref, g_ref, wg_ref, wu_ref, wd_ref, fg_ref, o_ref, h_ref, acc_ref, *, final):
    j = pl.program_id(1)

    @pl.when(j == 0)
    def _():
        h_ref[...] = _rms(x_ref[...], g_ref[...]).astype(BF16)
        acc_ref[...] = jnp.zeros(acc_ref.shape, F32)

    h = h_ref[...]
    a = _dot(h, wg_ref[...])
    u = _dot(h, wu_ref[...])
    act = (a * jax.nn.sigmoid(a)) * u
    acc_ref[...] += _dot(act.astype(BF16), wd_ref[...])

    @pl.when(j == pl.num_programs(1) - 1)
    def _():
        y = x_ref[...] + 0.5 * acc_ref[...]
        if final:
            y = _rms(y, fg_ref[...])
        o_ref[...] = y


def _ffn(x, g, wg, wu, wd, fg, *, final):
    t, d = x.shape
    ff = wg.shape[1]
    tm = min(t, 1024)
    tf = 256
    return pl.pallas_call(
        functools.partial(_ffn_kernel, final=final),
        grid=(t // tm, ff // tf),
        in_specs=[
            pl.BlockSpec((tm, d), lambda i, j: (i, 0)),
            pl.BlockSpec((1, d), lambda i, j: (0, 0)),
            pl.BlockSpec((d, tf), lambda i, j: (0, j)),
            pl.BlockSpec((d, tf), lambda i, j: (0, j)),
            pl.BlockSpec((tf, d), lambda i, j: (j, 0)),
            pl.BlockSpec((1, d), lambda i, j: (0, 0)),
        ],
        out_specs=pl.BlockSpec((tm, d), lambda i, j: (i, 0)),
        out_shape=jax.ShapeDtypeStruct((t, d), F32),
        scratch_shapes=[pltpu.VMEM((tm, d), BF16), pltpu.VMEM((tm, d), F32)],
        compiler_params=_cparams(2),
    )(x, g, wg, wu, wd, fg)


def _proj_common(x_ref, g_ref, win_ref, qn_ref, wuq_ref, kvn_ref, rc_ref, rs1_ref, rs2_ref,
                 qmla_ref, ckv_ref, qnsa_ref, gates_ref, merge_ref):
    n = _rms(x_ref[...], g_ref[...]).astype(BF16)

    def seg(a, b):
        return _dot(n, win_ref[:, a:b])

    rc, rs1, rs2 = rc_ref[...], rs1_ref[...], rs2_ref[...]

    def rope(blk):
        return blk * rc + pltpu.roll(blk, 112, 1) * rs1 + pltpu.roll(blk, 16, 1) * rs2

    cq = _rms(seg(C_CQ, C_CKV), qn_ref[...]).astype(BF16)
    q = _dot(cq, wuq_ref[...]) * (MLA_SCALE * LOG2E)
    for h in range(MLA_HEADS):
        qmla_ref[:, h * LANES:(h + 1) * LANES] = rope(q[:, h * LANES:(h + 1) * LANES]).astype(BF16)

    c_kv = _rms(seg(C_CKV, C_KR), kvn_ref[...])
    ckv_ref[...] = c_kv
    krb = rope(seg(C_KR, C_QN))
    qnsa_ref[...] = (seg(C_QN, C_GT) * NSA_SCALE).astype(BF16)
    gates_ref[...] = jax.nn.sigmoid(seg(C_GT, C_MG))
    merge_ref[...] = jax.nn.sigmoid(seg(C_MG, C_KVN))
    return n, seg, c_kv, krb


def _proj_prompt_kernel(x_ref, g_ref, win_ref, wkv_t_ref, qn_ref, wuq_ref, kvn_ref, wuk_t_ref, wuv_t_ref,
                        rc_ref, rs1_ref, rs2_ref,
                        qmla_ref, k_t_ref, v_t_ref, ckv_ref, kr_t_ref, qnsa_ref, cmp_t_ref, sel_t_ref, win_t_ref,
                        sk_ref, sv_ref, wk_ref, wv_ref, gates_ref, merge_ref):
    n, _, c_kv, krb = _proj_common(x_ref, g_ref, win_ref, qn_ref, wuq_ref, kvn_ref, rc_ref, rs1_ref, rs2_ref,
                                   qmla_ref, ckv_ref, qnsa_ref, gates_ref, merge_ref)
    tm = n.shape[0]
    cb = c_kv.astype(BF16)
    kr_t = jnp.transpose(krb)
    kr_t_ref[0] = kr_t[MLA_NOPE:MLA_NOPE + MLA_ROPE]
    rows = _iota((LANES, tm), 0)
    k_t = _dot_nt(wuk_t_ref[...], cb)
    v_t = _dot_nt(wuv_t_ref[...], cb)
    for h in range(MLA_HEADS):
        hs = slice(h * LANES, (h + 1) * LANES)
        k_t_ref[0, h, 0] = (k_t[hs] + kr_t).astype(BF16)
        v_t_ref[0, h, 0] = jnp.where(rows >= MLA_V, 1.0, v_t[hs]).astype(BF16)
    kv_t = _dot_nt(wkv_t_ref[...], n)
    cmp_t_ref[0] = kv_t[0:256]
    sel_t_ref[0] = kv_t[256:512]
    win_t_ref[0] = kv_t[512:768]
    for g in range(NSA_G):
        for base, k_out, v_out in ((256, sk_ref, sv_ref), (512, wk_ref, wv_ref)):
            blk = kv_t[base + g * LANES:base + (g + 1) * LANES]
            k_out[0, g, 0] = blk.astype(BF16)
            v_out[0, g, 0] = jnp.where(rows < NSA_DH, 1.0, blk).astype(BF16)


def _proj_sample_kernel(x_ref, g_ref, win_ref, qn_ref, wuq_ref, kvn_ref, rc_ref, rs1_ref, rs2_ref,
                        qmla_ref, ckv_ref, kr_ref, qnsa_ref, cmp_ref, sel_ref, win_o_ref, gates_ref, merge_ref):
    _, seg, _, krb = _proj_common(x_ref, g_ref, win_ref, qn_ref, wuq_ref, kvn_ref, rc_ref, rs1_ref, rs2_ref,
                                  qmla_ref, ckv_ref, qnsa_ref, gates_ref, merge_ref)
    kr_ref[...] = krb
    cmp_ref[...] = seg(C_KVN, C_KVN + 256)
    sel_ref[...] = seg(C_KVN + 256, C_KVN + 512)
    win_o_ref[...] = seg(C_KVN + 512, C_END)


def _mixer_project_prompt(x, b, s, g, w_in_p, w_kv_t, q_norm, w_uq_p, kv_norm, w_uk_t, w_uv_t, rc, rs1, rs2):
    t, d = x.shape
    tm = TILE
    nt = s // tm
    full = lambda a: pl.BlockSpec(a.shape, lambda i: (0,) * a.ndim)
    row = lambda w: pl.BlockSpec((tm, w), lambda i: (i, 0))
    tab = pl.BlockSpec((tm, LANES), lambda i: (i % nt, 0))
    fm = lambda r: pl.BlockSpec((1, r, tm), lambda i: (i // nt, 0, i % nt))
    tiles = lambda h: pl.BlockSpec((1, h, 1, LANES, tm), lambda i: (i // nt, 0, i % nt, 0, 0))
    tok = lambda w, dt: jax.ShapeDtypeStruct((t, w), dt)
    fms = lambda r: jax.ShapeDtypeStruct((b, r, s), F32)
    til = lambda h: jax.ShapeDtypeStruct((b, h, nt, LANES, tm), BF16)
    outs = [(row(1024), tok(1024, BF16)), (tiles(MLA_HEADS), til(MLA_HEADS)), (tiles(MLA_HEADS), til(MLA_HEADS)),
            (row(256), tok(256, F32)), (fm(MLA_ROPE), fms(MLA_ROPE)), (row(1024), tok(1024, BF16)),
            (fm(256), fms(256)), (fm(256), fms(256)), (fm(256), fms(256)),
            (tiles(NSA_G), til(NSA_G)), (tiles(NSA_G), til(NSA_G)), (tiles(NSA_G), til(NSA_G)), (tiles(NSA_G), til(NSA_G)),
            (row(256), tok(256, F32)), (row(2048), tok(2048, F32))]
    ins = (x, g, w_in_p, w_kv_t, q_norm, w_uq_p, kv_norm, w_uk_t, w_uv_t)
    return pl.pallas_call(
        _proj_prompt_kernel,
        grid=(t // tm,),
        in_specs=[row(d)] + [full(a) for a in ins[1:]] + [tab, tab, tab],
        out_specs=[o[0] for o in outs],
        out_shape=[o[1] for o in outs],
        compiler_params=_cparams(1),
    )(*ins, rc, rs1, rs2)


def _mixer_project_sample(x, g, w_in_p, q_norm, w_uq_p, kv_norm, rc, rs1, rs2):
    t, d = x.shape
    tm = min(t, TILE)
    full = lambda a: pl.BlockSpec(a.shape, lambda i: (0,) * a.ndim)
    row = lambda w: pl.BlockSpec((tm, w), lambda i: (i, 0))
    widths = [(1024, BF16), (256, F32), (128, F32), (1024, BF16), (256, F32), (256, F32), (256, F32), (256, F32),
              (2048, F32)]
    ins = (x, g, w_in_p, q_norm, w_uq_p, kv_norm)
    return pl.pallas_call(
        _proj_sample_kernel,
        grid=(t // tm,),
        in_specs=[row(d)] + [full(a) for a in ins[1:]] + [row(LANES)] * 3,
        out_specs=[row(w) for w, _ in widths],
        out_shape=[jax.ShapeDtypeStruct((t, w), dt) for w, dt in widths],
        compiler_params=_cparams(1),
    )(*ins, rc, rs1, rs2)


def _norm_mm_kernel(x_ref, g_ref, w_ref, o_ref, *, scale):
    y = _dot(_rms(x_ref[...], g_ref[...]).astype(BF16), w_ref[...])
    if scale != 1.0:
        y = y * scale
    o_ref[...] = y.astype(o_ref.dtype)


def _norm_matmul(x, g, w, *, scale=1.0, out_dtype=F32):
    t, d = x.shape
    n = w.shape[1]
    tm = min(t, 512)
    return pl.pallas_call(
        functools.partial(_norm_mm_kernel, scale=scale),
        grid=(t // tm,),
        in_specs=[pl.BlockSpec((tm, d), lambda i: (i, 0)), pl.BlockSpec((1, d), lambda i: (0, 0)),
                  pl.BlockSpec((d, n), lambda i: (0, 0))],
        out_specs=pl.BlockSpec((tm, n), lambda i: (i, 0)),
        out_shape=jax.ShapeDtypeStruct((t, n), out_dtype),
        compiler_params=_cparams(1),
    )(x, g, w)


def _mm_kernel(a_ref, w_ref, o_ref):
    o_ref[...] = _dot(a_ref[...], w_ref[...]).astype(o_ref.dtype)


def _matmul(a, w, *, out_dtype=F32):
    t, k = a.shape
    n = w.shape[1]
    tm = min(t, 512)
    return pl.pallas_call(
        _mm_kernel,
        grid=(t // tm,),
        in_specs=[pl.BlockSpec((tm, k), lambda i: (i, 0)), pl.BlockSpec((k, n), lambda i: (0, 0))],
        out_specs=pl.BlockSpec((tm, n), lambda i: (i, 0)),
        out_shape=jax.ShapeDtypeStruct((t, n), out_dtype),
        compiler_params=_cparams(1),
    )(a, w)


def _mm_res_kernel(a_ref, w_ref, x_ref, o_ref):
    o_ref[...] = x_ref[...] + _dot(a_ref[...], w_ref[...])


def _matmul_residual(a, w, x):
    t, k = a.shape
    n = w.shape[1]
    tm = min(t, 512)
    return pl.pallas_call(
        _mm_res_kernel,
        grid=(t // tm,),
        in_specs=[pl.BlockSpec((tm, k), lambda i: (i, 0)), pl.BlockSpec((k, n), lambda i: (0, 0)),
                  pl.BlockSpec((tm, n), lambda i: (i, 0))],
        out_specs=pl.BlockSpec((tm, n), lambda i: (i, 0)),
        out_shape=jax.ShapeDtypeStruct((t, n), F32),
        compiler_params=_cparams(1),
    )(a, w, x)


def _merge_kernel(oa_ref, ob_ref, mg_ref, x_ref, wa_ref, wb_ref, wo_ref, o_ref):
    d = x_ref.shape[1]
    mix = mg_ref[:, :d] * _dot(oa_ref[...], wa_ref[...]) + mg_ref[:, d:] * _dot(ob_ref[...], wb_ref[...])
    o_ref[...] = x_ref[...] + _dot(mix.astype(BF16), wo_ref[...])


def _merge(o_a, o_b, mg, x, w_a, w_b, w_o):
    t, d = x.shape
    tm = min(t, 512)
    row = lambda w: pl.BlockSpec((tm, w), lambda i: (i, 0))
    full = lambda a: pl.BlockSpec(a.shape, lambda i: (0, 0))
    return pl.pallas_call(
        _merge_kernel,
        grid=(t // tm,),
        in_specs=[row(o_a.shape[1]), row(o_b.shape[1]), row(2 * d), row(d), full(w_a), full(w_b), full(w_o)],
        out_specs=row(d),
        out_shape=jax.ShapeDtypeStruct((t, d), F32),
        compiler_params=_cparams(1),
    )(o_a, o_b, mg, x, w_a, w_b, w_o)


MLA_HPS = 4


def _mla_prompt_kernel(q_ref, k_ref, v_ref, o_ref, m_ref, acc_ref, *, tq, tk):
    i = pl.program_id(2)
    q = q_ref[0]
    qs = [q[:, h * LANES:(h + 1) * LANES] for h in range(MLA_HPS)]
    per_q = tq // tk
    m_ref[...] = jnp.full(m_ref.shape, NEG_INF, F32)
    acc_ref[...] = jnp.zeros(acc_ref.shape, F32)

    def tile(j, causal):
        if causal:
            ok = (j * tk + _iota((tq, tk), 1)) <= (i * tq + _iota((tq, tk), 0))
        for h in range(MLA_HPS):
            s = _dot(qs[h], k_ref[0, h, j])
            if causal:
                s = jnp.where(ok, s, -MASK_BIG)
            _flash_step(s, v_ref[0, h, j], m_ref, acc_ref, h, jnp.exp2)

    def body(j, c):
        tile(j, False)
        return c

    lax.fori_loop(0, per_q * i, body, 0)
    for d in range(per_q):
        tile(per_q * i + d, True)
    lane = _iota((1, LANES), 1)
    outs = []
    for h in range(MLA_HPS):
        a = acc_ref[h]
        outs.append(a / pltpu.roll(a, MLA_V, 1))
    for p in range(MLA_HPS // 2):
        pair = jnp.where(lane < MLA_V, outs[2 * p], pltpu.roll(outs[2 * p + 1], MLA_V, 1))
        o_ref[0, :, p * LANES:(p + 1) * LANES] = pair.astype(o_ref.dtype)


def _mla_prompt(q, k_t, v_t):
    b, s, _ = q.shape
    tq = QTILE
    nt = s // TILE
    kv_spec = pl.BlockSpec((1, MLA_HPS, nt, LANES, TILE), lambda bi, hq, i: (bi, hq, 0, 0, 0))
    return pl.pallas_call(
        functools.partial(_mla_prompt_kernel, tq=tq, tk=TILE),
        grid=(b, MLA_HEADS // MLA_HPS, s // tq),
        in_specs=[pl.BlockSpec((1, tq, MLA_HPS * LANES), lambda bi, hq, i: (bi, i, hq)), kv_spec, kv_spec],
        out_specs=pl.BlockSpec((1, tq, MLA_HPS * MLA_V), lambda bi, hq, i: (bi, i, hq)),
        out_shape=jax.ShapeDtypeStruct((b, s, MLA_HEADS * MLA_V), BF16),
        scratch_shapes=[pltpu.VMEM((MLA_HPS, tq, LANES), F32), pltpu.VMEM((MLA_HPS, tq, LANES), F32)],
        compiler_params=_cparams(3),
    )(q, k_t, v_t)


def _split_hi_lo(x):
    hi = x.astype(BF16)
    return hi, (x - hi.astype(F32)).astype(BF16)


def _compress_prompt_kernel(x_ref, a_ref, phi_ref, o_ref):
    hi, lo = _split_hi_lo(x_ref[0])
    means = _dot(jnp.concatenate([hi, lo], axis=1), a_ref[...])
    o_ref[0] = _dot(phi_ref[...], means.astype(BF16)).astype(o_ref.dtype)


def _compress_prompt(cmp_t, avg2, phi_t):
    b, w, s = cmp_t.shape
    nc = avg2.shape[1]
    return pl.pallas_call(
        _compress_prompt_kernel,
        grid=(b,),
        in_specs=[pl.BlockSpec((1, w, s), lambda i: (i, 0, 0)), pl.BlockSpec(avg2.shape, lambda i: (0, 0)),
                  pl.BlockSpec(phi_t.shape, lambda i: (0, 0))],
        out_specs=pl.BlockSpec((1, w, nc), lambda i: (i, 0, 0)),
        out_shape=jax.ShapeDtypeStruct((b, w, nc), BF16),
        compiler_params=_cparams(1),
    )(cmp_t, avg2, phi_t)


def _nsa_prompt_kernel(q_ref, c_ref, ctab_ref, sk_ref, sv_ref, wk_ref, wv_ref, ktab_ref, gt_ref, o_ref,
                       m_ref, acc_ref, *, tq, tk, n_sel):
    g = pl.program_id(1)
    i = pl.program_id(2)
    r = NSA_HPG * tq
    shift = tq.bit_length() - 1
    row = _iota((r, LANES), 0)
    lane_r = _iota((r, LANES), 1)
    qpos = i * tq + (row & (tq - 1))
    slope = _pow2_neg(NSA_HPG * g + (row >> shift) + 1)
    aq = _query_aug(slope, qpos, lane_r)
    q_st = jnp.concatenate([q_ref[0, :, h * LANES:(h + 1) * LANES] for h in range(NSA_HPG)], axis=0)
    q_plain = jnp.concatenate([q_st, aq.astype(BF16)], axis=1)

    kc = c_ref[0]
    nc = kc.shape[1]
    half = nc // 2
    lane_c = _iota((1, nc), 1)
    cblk = jnp.where(lane_c < half, 2 * lane_c, 2 * (lane_c - half) + 1)
    cmask = (cblk * CMP_BLOCK + (CMP_BLOCK - 1)) <= (i * tq + (_iota((r, nc), 0) & (tq - 1)))
    s = _dot(q_plain, jnp.concatenate([kc, ctab_ref[...]], axis=0))
    s = jnp.where(cmask, s, NEG_INF)
    mx = jnp.max(s, axis=-1, keepdims=True)
    mx = jnp.where(mx == NEG_INF, 0.0, mx)
    e = jnp.where(cmask, jnp.exp(s - mx), 0.0)
    p = e / jnp.maximum(jnp.sum(e, axis=-1, keepdims=True), 1e-30)
    o_cmp = _dot_nt(p.astype(BF16), kc)

    psum = p[0:tq] + p[tq:2 * tq] + p[2 * tq:3 * tq] + p[3 * tq:4 * tq]
    imp = psum + pltpu.roll(psum, half, 1)
    qp = i * tq + _iota((tq, 1), 0)
    forced = (lane_c == 0) | (lane_c == (qp >> 6))
    score = jnp.where(lane_c * SEL_BLOCK <= qp, imp + jnp.where(forced, FORCE_BONUS, 0.0), -1.0)
    st = jnp.transpose(score)[0:n_sel]
    jidx = _iota((n_sel, 1), 0)
    rank = jnp.zeros((n_sel, tq), F32)
    for ii in range(n_sel):
        ri = st[ii:ii + 1, :]
        first = jnp.where(jidx > ii, 1.0, 0.0)
        rank = rank + jnp.where(ri > st, 1.0, jnp.where(ri == st, first, 0.0))
    valid_t = (jidx * SEL_BLOCK) <= (i * tq + _iota((1, tq), 1))
    mt = jnp.where(valid_t, jnp.where(rank < float(min(SEL_TOPK, n_sel)), 1.0, 0.0), 0.0)
    if n_sel < LANES:
        mt = jnp.concatenate([mt, jnp.zeros((LANES - n_sel, tq), F32)], axis=0)
    msel = jnp.transpose(mt)
    mst = jnp.concatenate([msel] * NSA_HPG, axis=0)
    q_sel = jnp.concatenate([q_st, jnp.where(lane_r < n_sel, (mst - 1.0) * MASK_BIG, aq).astype(BF16)], axis=1)

    lane = _iota((1, LANES), 1)

    def finish():
        a = acc_ref[0]
        return a / pltpu.roll(a, NSA_DH, 1)

    m_ref[...] = jnp.full(m_ref.shape, NEG_INF, F32)
    acc_ref[...] = jnp.zeros(acc_ref.shape, F32)
    per_q = tq // tk

    def dist_to(j):
        return (i * tq + (_iota((r, tk), 0) & (tq - 1))) - (j * tk + _iota((r, tk), 1))

    def sel_tile(j, causal):
        sc = _dot(q_sel, jnp.concatenate([sk_ref[0, 0, j], ktab_ref[j]], axis=0))
        if causal:
            sc = jnp.where(dist_to(j) >= 0, sc, -MASK_BIG)
        _flash_step(sc, sv_ref[0, 0, j], m_ref, acc_ref, 0, jnp.exp)

    def sel_body(j, c):
        sel_tile(j, False)
        return c

    lax.fori_loop(0, per_q * i, sel_body, 0)
    for dd in range(per_q):
        sel_tile(per_q * i + dd, True)
    o_sel = finish()

    m_ref[...] = jnp.full(m_ref.shape, NEG_INF, F32)
    acc_ref[...] = jnp.zeros(acc_ref.shape, F32)

    def win_body(j, c):
        dist = dist_to(j)
        sc = _dot(q_plain, jnp.concatenate([wk_ref[0, 0, j], ktab_ref[j]], axis=0))
        sc = jnp.where(dist >= 0, jnp.where(dist <= WINDOW, sc, -MASK_BIG), -MASK_BIG)
        _flash_step(sc, wv_ref[0, 0, j], m_ref, acc_ref, 0, jnp.exp)
        return c

    lax.fori_loop(jnp.maximum(per_q * i - WINDOW // tk, 0), per_q * (i + 1), win_body, 0)
    o_win = finish()

    gt = gt_ref[0]
    heads = []
    for h in range(NSA_HPG):
        rs = slice(h * tq, (h + 1) * tq)
        heads.append(gt[:, 3 * h:3 * h + 1] * o_cmp[rs] + gt[:, 3 * h + 1:3 * h + 2] * o_sel[rs]
                     + gt[:, 3 * h + 2:3 * h + 3] * o_win[rs])
    for p2 in range(NSA_HPG // 2):
        pair = jnp.where(lane < NSA_DH, pltpu.roll(heads[2 * p2], NSA_DH, 1), heads[2 * p2 + 1])
        o_ref[0, :, p2 * LANES:(p2 + 1) * LANES] = pair.astype(o_ref.dtype)


def _nsa_prompt(q, ckcv_t, ctab, sk, sv, wk, wv, ktab, gates):
    b, s, _ = q.shape
    tq = QTILE
    nt = s // TILE
    nc = ckcv_t.shape[2]
    r = NSA_HPG * tq
    kv = pl.BlockSpec((1, 1, nt, LANES, TILE), lambda bi, gi, i: (bi, gi, 0, 0, 0))
    return pl.pallas_call(
        functools.partial(_nsa_prompt_kernel, tq=tq, tk=TILE, n_sel=s // SEL_BLOCK),
        grid=(b, NSA_G, s // tq),
        in_specs=[pl.BlockSpec((1, tq, NSA_HPG * LANES), lambda bi, gi, i: (bi, i, gi)),
                  pl.BlockSpec((1, LANES, nc), lambda bi, gi, i: (bi, gi, 0)),
                  pl.BlockSpec(ctab.shape, lambda bi, gi, i: (0, 0)),
                  kv, kv, kv, kv,
                  pl.BlockSpec(ktab.shape, lambda bi, gi, i: (0, 0, 0)),
                  pl.BlockSpec((1, tq, LANES), lambda bi, gi, i: (bi, i, gi))],
        out_specs=pl.BlockSpec((1, tq, NSA_HPG * NSA_DH), lambda bi, gi, i: (bi, i, gi)),
        out_shape=jax.ShapeDtypeStruct((b, s, NSA_HEADS * NSA_DH), BF16),
        scratch_shapes=[pltpu.VMEM((1, r, LANES), F32), pltpu.VMEM((1, r, LANES), F32)],
        compiler_params=_cparams(3),
    )(q, ckcv_t, ctab, sk, sv, wk, wv, ktab, gates)


def _xattn_prompt_kernel(q_ref, kv_ref, o_ref):
    hw = X_HEADS * X_DH
    for h in range(X_HEADS):
        cs = slice(h * X_DH, (h + 1) * X_DH)
        s = _dot_nt(q_ref[0, :, cs], kv_ref[0, :, cs])
        e = jnp.exp(s - jnp.max(s, axis=-1, keepdims=True))
        p = e / jnp.sum(e, axis=-1, keepdims=True)
        o_ref[0, :, cs] = _dot(p.astype(BF16), kv_ref[0, :, hw + h * X_DH:hw + (h + 1) * X_DH]).astype(o_ref.dtype)


def _xattn_prompt(q, kv):
    b, s, w = q.shape
    m = kv.shape[1]
    tq = min(s, 512)
    return pl.pallas_call(
        _xattn_prompt_kernel,
        grid=(b, s // tq),
        in_specs=[pl.BlockSpec((1, tq, w), lambda bi, i: (bi, i, 0)),
                  pl.BlockSpec((1, m, 2 * w), lambda bi, i: (bi, 0, 0))],
        out_specs=pl.BlockSpec((1, tq, w), lambda bi, i: (bi, i, 0)),
        out_shape=jax.ShapeDtypeStruct((b, s, w), BF16),
        compiler_params=_cparams(2),
    )(q, kv)


def _qabs_kernel(q_ref, w_ref, o_ref):
    o_ref[0] = _dot(q_ref[...], w_ref[0]).astype(o_ref.dtype)


def _mla_absorb_q(q_mla, w_abs):
    t = q_mla.shape[0]
    n = w_abs.shape[2]
    return pl.pallas_call(
        _qabs_kernel,
        grid=(MLA_HEADS,),
        in_specs=[pl.BlockSpec((t, LANES), lambda h: (0, h)), pl.BlockSpec((1, LANES, n), lambda h: (h, 0, 0))],
        out_specs=pl.BlockSpec((1, t, n), lambda h: (h, 0, 0)),
        out_shape=jax.ShapeDtypeStruct((MLA_HEADS, t, n), BF16),
        compiler_params=_cparams(1),
    )(q_mla, w_abs)


def _double_buffered(step, n_steps, copies):
    @pl.when(step == 0)
    def _():
        for cp in copies(step, 0):
            cp.start()

    @pl.when(step + 1 < n_steps)
    def _():
        for cp in copies(step + 1, (step + 1) & 1):
            cp.start()

    slot = step & 1
    for cp in copies(step, slot):
        cp.wait()
    return slot


def _mla_sample_kernel(pt_ref, qa_ref, cn_ref, krn_ref, c_hbm, kr_hbm, o_ref, cbuf, krbuf, csem, krsem,
                       *, n_pages, db, nq):
    def copies(bb, slot):
        out = []
        for p in range(n_pages):
            page = pt_ref[p * db + bb]
            out.append(pltpu.make_async_copy(c_hbm.at[page], cbuf.at[slot, pl.ds(p * PAGE, PAGE), :], csem.at[slot]))
            out.append(pltpu.make_async_copy(kr_hbm.at[page], krbuf.at[slot, :, pl.ds(p * PAGE, PAGE)], krsem.at[slot]))
        return out

    slot = _double_buffered(pl.program_id(0), pl.num_programs(0), copies)
    qa = qa_ref[0]
    qc, qr = qa[:, :MLA_KV_LORA], qa[:, MLA_KV_LORA:MLA_KV_LORA + MLA_ROPE]
    c = cbuf[slot].astype(BF16)
    kr_t = krbuf[slot].astype(BF16)
    s1 = _dot_nt(qc, c) + _dot(qr, kr_t)
    cn = cn_ref[0].astype(BF16)
    s2 = _dot_nt(qc, cn) + _dot_nt(qr, krn_ref[0].astype(BF16))
    qi = _iota((qa.shape[0], 1), 0) & (nq - 1)
    s2 = jnp.where(_iota((1, cn.shape[0]), 1) <= qi, s2, NEG_INF)
    mx = jnp.maximum(jnp.max(s1, axis=-1, keepdims=True), jnp.max(s2, axis=-1, keepdims=True))
    e1 = jnp.exp2(s1 - mx)
    e2 = jnp.exp2(s2 - mx)
    den = jnp.sum(e1, axis=-1, keepdims=True) + jnp.sum(e2, axis=-1, keepdims=True)
    o_ref[0] = (_dot(e1.astype(BF16), c) + _dot(e2.astype(BF16), cn)) / den


def _mla_sample(pt_flat, db, n_pages, qa, c_new, kr_new, ckv_pool, kr_pool_t, *, nq):
    r = qa.shape[1]
    past = n_pages * PAGE
    per_b = lambda a: pl.BlockSpec((1,) + a.shape[1:], lambda b, pt: (b, 0, 0))
    hbm = pl.BlockSpec(memory_space=pl.ANY)
    grid_spec = pltpu.PrefetchScalarGridSpec(
        num_scalar_prefetch=1,
        grid=(db,),
        in_specs=[per_b(qa), per_b(c_new), per_b(kr_new), hbm, hbm],
        out_specs=pl.BlockSpec((1, r, MLA_KV_LORA), lambda b, pt: (b, 0, 0)),
        scratch_shapes=[pltpu.VMEM((2, past, MLA_KV_LORA), F32), pltpu.VMEM((2, MLA_ROPE, past), F32),
                        pltpu.SemaphoreType.DMA((2,)), pltpu.SemaphoreType.DMA((2,))],
    )
    return pl.pallas_call(
        functools.partial(_mla_sample_kernel, n_pages=n_pages, db=db, nq=nq),
        grid_spec=grid_spec,
        out_shape=jax.ShapeDtypeStruct((db, r, MLA_KV_LORA), F32),
        compiler_params=_cparams(1),
    )(pt_flat, qa, c_new, kr_new, ckv_pool, kr_pool_t)


def _cmp_sample_kernel(pt_ref, a_ref, phi_ref, pool_hbm, o_ref, buf, sem, *, n_pages, db):
    def copies(bb, slot):
        return [pltpu.make_async_copy(pool_hbm.at[pt_ref[p * db + bb]], buf.at[slot, :, pl.ds(p * PAGE, PAGE)],
                                      sem.at[slot]) for p in range(n_pages)]

    slot = _double_buffered(pl.program_id(0), pl.num_programs(0), copies)
    keys = a_ref.shape[1] // 2
    blocks = a_ref.shape[0]
    for c in range(n_pages * PAGE // keys):
        hi, lo = _split_hi_lo(buf[slot, :, c * keys:(c + 1) * keys])
        means = _dot_nt(a_ref[...], jnp.concatenate([hi, lo], axis=1))
        o_ref[0, c * blocks:(c + 1) * blocks, :] = _dot(means.astype(BF16), phi_ref[...]).astype(o_ref.dtype)


def _compress_sample(pt_flat, db, n_pages, cmp_pool_t, avg_t, phi):
    w = cmp_pool_t.shape[1]
    past = n_pages * PAGE
    grid_spec = pltpu.PrefetchScalarGridSpec(
        num_scalar_prefetch=1,
        grid=(db,),
        in_specs=[pl.BlockSpec(avg_t.shape, lambda b, pt: (0, 0)), pl.BlockSpec(phi.shape, lambda b, pt: (0, 0)),
                  pl.BlockSpec(memory_space=pl.ANY)],
        out_specs=pl.BlockSpec((1, past // CMP_BLOCK, w), lambda b, pt: (b, 0, 0)),
        scratch_shapes=[pltpu.VMEM((2, w, past), F32), pltpu.SemaphoreType.DMA((2,))],
    )
    return pl.pallas_call(
        functools.partial(_cmp_sample_kernel, n_pages=n_pages, db=db),
        grid_spec=grid_spec,
        out_shape=jax.ShapeDtypeStruct((db, past // CMP_BLOCK, w), BF16),
        compiler_params=_cparams(1),
    )(pt_flat, avg_t, phi, cmp_pool_t)


def _nsa_cmp_sample_kernel(q_ref, c_ref, ctab_ref, o_ref, imp_ref, *, nq, past):
    r = q_ref.shape[2]
    nc = c_ref.shape[1]
    row = _iota((r, LANES), 0)
    lane_r = _iota((r, LANES), 1)
    qpos = past + (row & (nq - 1))
    cmask = (_iota((1, nc), 1) * CMP_BLOCK + (CMP_BLOCK - 1)) <= (past + (_iota((r, nc), 0) & (nq - 1)))
    for g in range(NSA_G):
        slope = _pow2_neg(NSA_HPG * g + (row >> (nq.bit_length() - 1)) + 1)
        q2 = jnp.concatenate([q_ref[0, g], _query_aug(slope, qpos, lane_r).astype(BF16)], axis=1)
        kc = c_ref[0, :, g * LANES:(g + 1) * LANES]
        s = _dot_nt(q2, jnp.concatenate([kc, ctab_ref[...]], axis=1))
        s = jnp.where(cmask, s, NEG_INF)
        mx = jnp.max(s, axis=-1, keepdims=True)
        mx = jnp.where(mx == NEG_INF, 0.0, mx)
        e = jnp.where(cmask, jnp.exp(s - mx), 0.0)
        p = e / jnp.maximum(jnp.sum(e, axis=-1, keepdims=True), 1e-30)
        o_ref[0, g] = _dot(p.astype(BF16), kc)
        ps = p
        for h in range(1, NSA_HPG):
            ps = ps + pltpu.roll(p, h * nq, 0)
        imp_ref[0, g] = ps + pltpu.roll(ps, nc - 1, 1)


def _nsa_cmp_sample(q, ckcv, ctab, *, nq, past):
    db, g, r, _ = q.shape
    nc = ckcv.shape[1]
    return pl.pallas_call(
        functools.partial(_nsa_cmp_sample_kernel, nq=nq, past=past),
        grid=(db,),
        in_specs=[pl.BlockSpec((1,) + q.shape[1:], lambda b: (b, 0, 0, 0)),
                  pl.BlockSpec((1,) + ckcv.shape[1:], lambda b: (b, 0, 0)),
                  pl.BlockSpec(ctab.shape, lambda b: (0, 0))],
        out_specs=[pl.BlockSpec((1, g, r, LANES), lambda b: (b, 0, 0, 0)),
                   pl.BlockSpec((1, g, r, nc), lambda b: (b, 0, 0, 0))],
        out_shape=[jax.ShapeDtypeStruct((db, g, r, LANES), F32), jax.ShapeDtypeStruct((db, g, r, nc), F32)],
        compiler_params=_cparams(1),
    )(q, ckcv, ctab)


def _topk_sample_kernel(imp_ref, o_ref, *, n_pick, last_blk):
    s = imp_ref[...]
    lane_i = _iota(s.shape, 1)
    lane = lane_i.astype(F32)
    s = jnp.where(((lane_i & 1) == 0) & (lane_i > 0), s, -1.0)
    olane = _iota(o_ref.shape, 1)
    out = jnp.where(olane == n_pick + 1, float(last_blk), 0.0)
    for it in range(n_pick):
        mx = jnp.max(s, axis=-1, keepdims=True)
        idx = jnp.min(jnp.where(s == mx, lane, float(s.shape[1])), axis=-1, keepdims=True)
        s = jnp.where(lane == idx, -1.0, s)
        out = jnp.where(olane == it, idx * 0.5, out)
    o_ref[...] = out.astype(I32)


def _topk_sample(imp, *, n_pick, last_blk):
    rows = imp.shape[0]
    return pl.pallas_call(
        functools.partial(_topk_sample_kernel, n_pick=n_pick, last_blk=last_blk),
        grid=(1,),
        in_specs=[pl.BlockSpec(imp.shape, lambda i: (0, 0))],
        out_specs=pl.BlockSpec((rows, LANES), lambda i: (0, 0)),
        out_shape=jax.ShapeDtypeStruct((rows, LANES), I32),
        compiler_params=_cparams(1),
    )(imp)


def _nsa_sel_sample_kernel(pt_ref, ids_ref, q_ref, new_ref, pool_hbm, o_ref, buf, sem, *, nblk, nq, db, past):
    def block_id(t, qq, c):
        return ids_ref[(t * nq + qq) * SEL_TOPK + c]

    def copies(t, slot):
        b = t >> 1
        rows = pl.ds(pl.multiple_of((t & 1) * LANES, LANES), LANES)
        out = []
        for qq in range(nq):
            for c in range(nblk):
                page = pt_ref[(block_id(t, qq, c) >> 1) * db + b]
                dst = buf.at[slot, :, pl.ds((qq * nblk + c) * PAGE, PAGE)]
                out.append(pltpu.make_async_copy(pool_hbm.at[page, rows, :], dst, sem.at[slot]))
        return out

    t = pl.program_id(0)
    slot = _double_buffered(t, pl.num_programs(0), copies)
    g = t & 1
    q = q_ref[0]
    r = q.shape[0]
    kv_t = buf[slot].astype(BF16)
    lane = _iota((1, PAGE), 1)
    kpos, keep, owner = [], [], []
    for qq in range(nq):
        for c in range(nblk):
            blk = block_id(t, qq, c)
            kpos.append((blk >> 1) * PAGE + lane)
            keep.append((lane >> 6) == (blk & 1))
        owner.append(jnp.full((1, nblk * PAGE), qq, I32))
    kpos = jnp.concatenate(kpos, axis=1)
    keep = jnp.concatenate(keep, axis=1)
    owner = jnp.concatenate(owner, axis=1)
    row = _iota((r, 1), 0)
    qi = row & (nq - 1)
    slope = _pow2_neg(NSA_HPG * g + (row >> (nq.bit_length() - 1)) + 1)
    s1 = _dot(q, kv_t) - slope * (past + qi - kpos).astype(F32)
    s1 = jnp.where(owner == qi, jnp.where(keep, s1, NEG_INF), NEG_INF)
    nw = new_ref[0].astype(BF16)
    tnew = _iota((1, nw.shape[0]), 1)
    s2 = _dot_nt(q, nw) - slope * (qi - tnew).astype(F32)
    s2 = jnp.where(tnew <= qi, s2, NEG_INF)
    mx = jnp.maximum(jnp.max(s1, axis=-1, keepdims=True), jnp.max(s2, axis=-1, keepdims=True))
    e1 = jnp.exp(s1 - mx)
    e2 = jnp.exp(s2 - mx)
    den = jnp.sum(e1, axis=-1, keepdims=True) + jnp.sum(e2, axis=-1, keepdims=True)
    o_ref[0] = (_dot_nt(e1.astype(BF16), kv_t) + _dot(e2.astype(BF16), nw)) / den


def _nsa_sel_sample(pt_flat, ids, db, q, sel_new, sel_pool_t, *, nblk, nq, past):
    steps, r, _ = q.shape
    grid_spec = pltpu.PrefetchScalarGridSpec(
        num_scalar_prefetch=2,
        grid=(steps,),
        in_specs=[pl.BlockSpec((1, r, LANES), lambda t, pt, idr: (t, 0, 0)),
                  pl.BlockSpec((1, sel_new.shape[1], LANES), lambda t, pt, idr: (t // NSA_G, 0, t % NSA_G)),
                  pl.BlockSpec(memory_space=pl.ANY)],
        out_specs=pl.BlockSpec((1, r, LANES), lambda t, pt, idr: (t, 0, 0)),
        scratch_shapes=[pltpu.VMEM((2, LANES, nq * nblk * PAGE), F32), pltpu.SemaphoreType.DMA((2,))],
    )
    return pl.pallas_call(
        functools.partial(_nsa_sel_sample_kernel, nblk=nblk, nq=nq, db=db, past=past),
        grid_spec=grid_spec,
        out_shape=jax.ShapeDtypeStruct((steps, r, LANES), F32),
        compiler_params=_cparams(1),
    )(pt_flat, ids.reshape(-1), q, sel_new, sel_pool_t)


def _nsa_win_sample_kernel(q_ref, st_ref, wtab_ref, new_ref, o_ref, *, nq, past):
    r = q_ref.shape[2]
    wbuf = st_ref.shape[2]
    row = _iota((r, LANES), 0)
    lane_r = _iota((r, LANES), 1)
    qpos = past + (row & (nq - 1))
    qi = _iota((r, 1), 0) & (nq - 1)
    dist_st = qi + wbuf - _iota((1, wbuf), 1)
    tnew = _iota((1, new_ref.shape[1]), 1)
    for g in range(NSA_G):
        slope = _pow2_neg(NSA_HPG * g + (row >> (nq.bit_length() - 1)) + 1)
        q = q_ref[0, g]
        q2 = jnp.concatenate([q, _query_aug(slope, qpos, lane_r).astype(BF16)], axis=1)
        st = st_ref[0, g * LANES:(g + 1) * LANES, :].astype(BF16)
        nw = new_ref[0, :, g * LANES:(g + 1) * LANES].astype(BF16)
        s1 = _dot(q2, jnp.concatenate([st, wtab_ref[...]], axis=0))
        s1 = jnp.where(dist_st <= WINDOW, s1, NEG_INF)
        s2 = _dot_nt(q, nw) - slope[:, :1] * (qi - tnew).astype(F32)
        s2 = jnp.where(tnew <= qi, s2, NEG_INF)
        mx = jnp.maximum(jnp.max(s1, axis=-1, keepdims=True), jnp.max(s2, axis=-1, keepdims=True))
        e1 = jnp.exp(s1 - mx)
        e2 = jnp.exp(s2 - mx)
        den = jnp.sum(e1, axis=-1, keepdims=True) + jnp.sum(e2, axis=-1, keepdims=True)
        o_ref[0, g] = (_dot_nt(e1.astype(BF16), st) + _dot(e2.astype(BF16), nw)) / den


def _nsa_win_sample(q, state_t, wtab, win_new, *, nq, past):
    db, g, r, _ = q.shape
    return pl.pallas_call(
        functools.partial(_nsa_win_sample_kernel, nq=nq, past=past),
        grid=(db,),
        in_specs=[pl.BlockSpec((1,) + q.shape[1:], lambda b: (b, 0, 0, 0)),
                  pl.BlockSpec((1,) + state_t.shape[1:], lambda b: (b, 0, 0)),
                  pl.BlockSpec(wtab.shape, lambda b: (0, 0)),
                  pl.BlockSpec((1,) + win_new.shape[1:], lambda b: (b, 0, 0))],
        out_specs=pl.BlockSpec((1, g, r, LANES), lambda b: (b, 0, 0, 0)),
        out_shape=jax.ShapeDtypeStruct((db, g, r, LANES), F32),
        compiler_params=_cparams(1),
    )(q, state_t, wtab, win_new)


def _gate_kernel(g_ref, a_ref, b_ref, c_ref, o_ref):
    o_ref[...] = (g_ref[0] * a_ref[...] + g_ref[1] * b_ref[...] + g_ref[2] * c_ref[...]).astype(o_ref.dtype)


def _nsa_gate_sample(gexp, o_cmp, o_sel, o_win):
    t, w = o_cmp.shape
    row = pl.BlockSpec((t, w), lambda i: (0, 0))
    return pl.pallas_call(
        _gate_kernel,
        grid=(1,),
        in_specs=[pl.BlockSpec((3, t, w), lambda i: (0, 0, 0)), row, row, row],
        out_specs=row,
        out_shape=jax.ShapeDtypeStruct((t, w), BF16),
        compiler_params=_cparams(1),
    )(gexp, o_cmp, o_sel, o_win)


def _xattn_sample_kernel(q_ref, kv_ref, o_ref, *, bc, n_mem):
    stride = 2 * X_HEADS
    for bi in range(bc):
        for h in range(X_HEADS):
            cs = slice(h * X_DH, (h + 1) * X_DH)
            k = kv_ref[bi, pl.ds(h, n_mem, stride=stride), :].astype(BF16)
            v = kv_ref[bi, pl.ds(X_HEADS + h, n_mem, stride=stride), :].astype(BF16)
            s = _dot_nt(q_ref[bi, :, cs], k)
            e = jnp.exp(s - jnp.max(s, axis=-1, keepdims=True))
            p = e / jnp.sum(e, axis=-1, keepdims=True)
            o_ref[bi, :, cs] = _dot(p.astype(BF16), v).astype(o_ref.dtype)


def _xattn_sample(q, kv, *, n_mem):
    db, r, w = q.shape
    bc = 4 if db % 4 == 0 else 1
    return pl.pallas_call(
        functools.partial(_xattn_sample_kernel, bc=bc, n_mem=n_mem),
        grid=(db // bc,),
        in_specs=[pl.BlockSpec((bc, r, w), lambda i: (i, 0, 0)),
                  pl.BlockSpec((bc,) + kv.shape[1:], lambda i: (i, 0, 0))],
        out_specs=pl.BlockSpec((bc, r, w), lambda i: (i, 0, 0)),
        out_shape=jax.ShapeDtypeStruct((db, r, w), BF16),
        compiler_params=_cparams(1),
    )(q, kv)


def _prep_weights(w_in, mla_w_uq, mla_w_uk, mla_w_uv, nsa_phi_k, nsa_phi_v):
    d = w_in.shape[0]
    z = lambda n: jnp.zeros((d, n), F32)
    o = 0
    cq, o = w_in[:, o:o + 384], o + 384
    ckv, o = w_in[:, o:o + 256], o + 256
    kr, o = w_in[:, o:o + 32], o + 32
    qn, o = w_in[:, o:o + 512], o + 512
    kvn, o = w_in[:, o:o + 768], o + 768
    gn, o = w_in[:, o:o + 24], o + 24
    mg = w_in[:, o:]
    qn_p = jnp.pad(qn.reshape(d, NSA_HEADS, NSA_DH), ((0, 0), (0, 0), (0, LANES - NSA_DH))).reshape(d, -1)
    w_in_p = jnp.concatenate([cq, ckv, z(64), kr, z(32), qn_p, gn[:, :12], z(116), gn[:, 12:], z(116), mg, kvn],
                             axis=1).astype(BF16)
    w_kv_t = jnp.transpose(kvn).astype(BF16)
    w_uq_p = jnp.pad(mla_w_uq, ((0, 0), (0, 0), (0, LANES - MLA_NOPE - MLA_ROPE))).reshape(MLA_Q_LORA, -1).astype(BF16)
    pad_t = lambda w: jnp.pad(jnp.transpose(w, (1, 2, 0)), ((0, 0), (0, LANES - w.shape[2]), (0, 0))).reshape(
        MLA_HEADS * LANES, MLA_KV_LORA).astype(BF16)
    w_uk_t, w_uv_t = pad_t(mla_w_uk), pad_t(mla_w_uv)
    w_abs = jnp.zeros((MLA_HEADS, LANES, 384), F32)
    w_abs = w_abs.at[:, :MLA_NOPE, :MLA_KV_LORA].set(jnp.transpose(mla_w_uk, (1, 2, 0)))
    w_abs = w_abs.at[:, MLA_NOPE:MLA_NOPE + MLA_ROPE, MLA_KV_LORA:MLA_KV_LORA + MLA_ROPE].set(jnp.eye(MLA_ROPE, dtype=F32))
    w_ov = jnp.zeros((MLA_HEADS, MLA_KV_LORA, MLA_HEADS, MLA_V), F32)
    for h in range(MLA_HEADS):
        w_ov = w_ov.at[h, :, h, :].set(mla_w_uv[:, h, :])
    w_ov = w_ov.reshape(MLA_HEADS * MLA_KV_LORA, MLA_HEADS * MLA_V)
    phi_t = jnp.zeros((4, NSA_DH, 4, NSA_DH), F32)
    for g in range(NSA_G):
        phi_t = phi_t.at[2 * g, :, 2 * g, :].set(jnp.transpose(nsa_phi_k[g]))
        phi_t = phi_t.at[2 * g + 1, :, 2 * g + 1, :].set(jnp.transpose(nsa_phi_v[g]))
    phi_t = phi_t.reshape(4 * NSA_DH, 4 * NSA_DH)
    return w_in_p, w_kv_t, w_uq_p, w_uk_t, w_uv_t, w_abs.astype(BF16), w_ov.astype(BF16), phi_t.astype(BF16)


def _rope_tables(pos):
    half = MLA_ROPE // 2
    inv = ROPE_BASE ** (-jnp.arange(half, dtype=F32) / half)
    ang = pos.astype(F32)[:, None] * inv[None, :]
    cos, sin = jnp.cos(ang), jnp.sin(ang)
    n = pos.shape[0]
    one, zero = jnp.ones((n, MLA_NOPE), F32), jnp.zeros((n, half), F32)
    tail = jnp.zeros((n, LANES - MLA_NOPE - MLA_ROPE), F32)
    rc = jnp.concatenate([one, cos, cos, tail], axis=1)
    rs1 = jnp.concatenate([0 * one, -sin, zero, tail], axis=1)
    rs2 = jnp.concatenate([0 * one, zero, sin, tail], axis=1)
    return rc, rs1, rs2


def _key_rows(pos):
    rows = jnp.arange(LANES, dtype=I32)[:, None]
    blk = (pos >> 6)[None, :]
    t = jnp.where((rows == blk) & (rows < SEL_BLOCK), 1.0, 0.0)
    t = jnp.where(rows == A_BLK, blk.astype(F32), t)
    t = jnp.where(rows == A_OFF, (pos & (SEL_BLOCK - 1))[None, :].astype(F32), t)
    t = jnp.where((rows == A_ONE0) | (rows == A_ONE1), 1.0, t)
    return t.astype(BF16)


def _avg_matrix(n_keys, order):
    blk_of_key = jnp.tile(jnp.arange(n_keys, dtype=I32) // CMP_BLOCK, 2)[:, None]
    return jnp.where(blk_of_key == order[None, :], 1.0 / CMP_BLOCK, 0.0).astype(BF16)


def _even_odd_order(n):
    return jnp.concatenate([jnp.arange(0, n, 2, dtype=I32), jnp.arange(1, n, 2, dtype=I32)])


def kernel(x_prompt, x_sample, mem_prompt, cache_mla_ckv, cache_mla_krope, cache_nsa_cmp_kv, cache_nsa_sel_kv,
           state_nsa_win_kv, cache_xattn_kv, page_table, ffn1_norm, ffn1_w_gate, ffn1_w_up, ffn1_w_down, mix_norm,
           w_in, mla_q_norm, mla_w_uq, mla_kv_norm, mla_w_uk, mla_w_uv, nsa_phi_k, nsa_phi_v, w_br_mla, w_br_nsa,
           w_out, xattn_norm, xattn_mem_norm, xattn_w_q, xattn_w_kv, xattn_w_o, ffn2_norm, ffn2_w_gate, ffn2_w_up,
           ffn2_w_down, final_norm):
    assert x_prompt.shape[2] == D_MODEL and ffn1_norm.shape[0] == 1
    b, s, d = x_prompt.shape
    db, nq, _ = x_sample.shape
    n_pages = page_table.shape[1]
    past = n_pages * PAGE
    n_mem = mem_prompt.shape[1]
    assert s % QTILE == 0 and past % SEL_BLOCK == 0 and nq <= 8 and NSA_G == 2
    g2 = lambda a: a.reshape(1, -1)
    bf = lambda a: a.astype(BF16)

    w_in_p, w_kv_t, w_uq_p, w_uk_t, w_uv_t, w_abs, w_ov, phi_t = _prep_weights(
        w_in[0], mla_w_uq[0], mla_w_uk[0], mla_w_uv[0], nsa_phi_k[0], nsa_phi_v[0])
    f1 = (g2(ffn1_norm[0]), bf(ffn1_w_gate[0]), bf(ffn1_w_up[0]), bf(ffn1_w_down[0]))
    f2 = (g2(ffn2_norm[0]), bf(ffn2_w_gate[0]), bf(ffn2_w_up[0]), bf(ffn2_w_down[0]))
    fg = g2(final_norm)
    w_a, w_b, w_o = bf(w_br_mla[0]), bf(w_br_nsa[0]), bf(w_out[0])
    w_xq, w_xo = bf(xattn_w_q[0]), bf(xattn_w_o[0])
    w_xkv = bf(xattn_w_kv[0].reshape(d, -1))
    mixg, qng, kvg = g2(mix_norm[0]), g2(mla_q_norm[0]), g2(mla_kv_norm[0])

    t = b * s
    nt = s // TILE
    hp = _ffn(x_prompt.reshape(t, d), *f1, fg, final=False)
    (q_mla, k_t, v_t, p_ckv, p_kr_t, q_nsa, p_cmp_t, p_sel_t, p_win_t, sk, sv, wk, wv, gates, merge) = (
        _mixer_project_prompt(hp, b, s, mixg, w_in_p[:, :C_KVN], w_kv_t, qng, w_uq_p, kvg, w_uk_t, w_uv_t,
                              *_rope_tables(jnp.arange(s))))
    o_a = _mla_prompt(q_mla.reshape(b, s, -1), k_t, v_t).reshape(t, -1)

    nc = s // CMP_BLOCK
    order = _even_odd_order(nc)
    ckcv_t = _compress_prompt(p_cmp_t, _avg_matrix(s, order), phi_t)
    ctab = _key_rows(order * CMP_BLOCK + (CMP_BLOCK - 1))
    ktab = jnp.transpose(_key_rows(jnp.arange(s, dtype=I32)).reshape(LANES, nt, TILE), (1, 0, 2))
    o_b = _nsa_prompt(q_nsa.reshape(b, s, -1), ckcv_t, ctab, sk, sv, wk, wv, ktab, gates.reshape(b, s, -1)).reshape(t, -1)
    hp = _merge(o_a, o_b, merge, hp, w_a, w_b, w_o)

    kv_mem = _norm_matmul(mem_prompt.reshape(b * n_mem, d), g2(xattn_mem_norm[0]), w_xkv)
    xq = _norm_matmul(hp, g2(xattn_norm[0]), w_xq, scale=X_SCALE, out_dtype=BF16)
    xo = _xattn_prompt(xq.reshape(b, s, -1), bf(kv_mem).reshape(b, n_mem, -1))
    hp = _matmul_residual(xo.reshape(t, -1), w_xo, hp)
    y_prompt = _ffn(hp, *f2, fg, final=True).reshape(b, s, d)

    wlen = min(WINDOW, s)
    kv_out = lambda a: jnp.transpose(a.reshape(1, b, NSA_G, 2, NSA_DH, a.shape[-1]), (0, 1, 5, 2, 3, 4))
    prompt_caches = (p_ckv.reshape(1, b, s, -1), jnp.transpose(p_kr_t, (0, 2, 1))[None], kv_out(p_cmp_t),
                     kv_out(p_sel_t), kv_out(p_win_t[:, :, s - wlen:]),
                     kv_mem.reshape(1, b, n_mem, 2, X_HEADS, X_DH))

    ts = db * nq
    hs = _ffn(x_sample.reshape(ts, d), *f1, fg, final=False)
    pos_s = jnp.tile(past + jnp.arange(nq), db)
    (q_mla, s_ckv, s_krb, q_nsa, s_cmp, s_sel, s_win, gates, merge) = _mixer_project_sample(
        hs, mixg, w_in_p, qng, w_uq_p, kvg, *_rope_tables(pos_s))
    s_kr = s_krb[:, MLA_NOPE:MLA_NOPE + MLA_ROPE]
    pad_rows = lambda a: jnp.pad(a.reshape(db, nq, -1), ((0, 0), (0, 8 - nq), (0, 0)))
    pt_flat = jnp.transpose(page_table).reshape(-1)

    qa = _mla_absorb_q(q_mla, w_abs)
    qa = jnp.transpose(qa.reshape(MLA_HEADS, db, nq, -1), (1, 0, 2, 3)).reshape(db, MLA_HEADS * nq, -1)
    kr_pool_t = jnp.transpose(cache_mla_krope[0], (0, 2, 1))
    o_lat = _mla_sample(pt_flat, db, n_pages, qa, pad_rows(s_ckv), pad_rows(s_kr), cache_mla_ckv[0], kr_pool_t, nq=nq)
    o_lat = jnp.transpose(o_lat.reshape(db, MLA_HEADS, nq, -1), (0, 2, 1, 3)).reshape(ts, -1)
    o_a = _matmul(bf(o_lat), w_ov, out_dtype=BF16)

    fm_pool = lambda c: jnp.transpose(c[0], (0, 2, 3, 4, 1)).reshape(c.shape[1], -1, PAGE)
    cmp_pool_t, sel_pool_t = fm_pool(cache_nsa_cmp_kv), fm_pool(cache_nsa_sel_kv)
    keys_per_dot = min(past, 32 * PAGE)
    avg_t = jnp.transpose(_avg_matrix(keys_per_dot, jnp.arange(keys_per_dot // CMP_BLOCK, dtype=I32)))
    ckcv_s = _compress_sample(pt_flat, db, n_pages, cmp_pool_t, avg_t, jnp.transpose(phi_t))
    ncs = past // CMP_BLOCK
    ctab_s = jnp.transpose(_key_rows(jnp.arange(ncs, dtype=I32) * CMP_BLOCK + (CMP_BLOCK - 1)))
    qn = jnp.transpose(q_nsa.reshape(db, nq, NSA_G, NSA_HPG, LANES), (0, 2, 3, 1, 4))
    q_rows = qn.reshape(db, NSA_G, NSA_HPG * nq, LANES)
    o_cmp, imp = _nsa_cmp_sample(q_rows, ckcv_s, ctab_s, nq=nq, past=past)
    n_sel = -(-(past + nq) // SEL_BLOCK)
    n_pick = min(SEL_TOPK, n_sel) - 2
    ids = _topk_sample(imp[:, :, :nq].reshape(db * NSA_G * nq, -1), n_pick=n_pick, last_blk=n_sel - 1)
    o_sel = _nsa_sel_sample(pt_flat, ids[:, :SEL_TOPK], db, q_rows.reshape(db * NSA_G, -1, LANES), pad_rows(s_sel),
                            sel_pool_t, nblk=n_pick + 1, nq=nq, past=past).reshape(db, NSA_G, -1, LANES)
    wbuf = state_nsa_win_kv.shape[2]
    state_t = jnp.transpose(state_nsa_win_kv[0], (0, 2, 3, 4, 1)).reshape(db, -1, wbuf)
    wtab = _key_rows(past - wbuf + jnp.arange(wbuf, dtype=I32))
    o_win = _nsa_win_sample(q_rows, state_t, wtab, pad_rows(s_win), nq=nq, past=past)
    to_tok = lambda a: jnp.transpose(a[..., NSA_DH:].reshape(db, NSA_G, NSA_HPG, nq, NSA_DH), (0, 3, 1, 2, 4)).reshape(ts, -1)
    gts =jnp.stack([gates[:, :12], gates[:, LANES:LANES + 12]], axis=1).reshape(ts, NSA_HEADS, 3)
    gexp = jnp.transpose(jnp.broadcast_to(gts[:, :, :, None], (ts, NSA_HEADS, 3, NSA_DH)), (2, 0, 1, 3)).reshape(3, ts, -1)
    o_b = _nsa_gate_sample(gexp, to_tok(o_cmp), to_tok(o_sel), to_tok(o_win))
    hs = _merge(o_a, o_b, merge, hs, w_a, w_b, w_o)

    xq = _norm_matmul(hs, g2(xattn_norm[0]), w_xq, scale=X_SCALE, out_dtype=BF16)
    xo = _xattn_sample(pad_rows(xq), cache_xattn_kv[0].reshape(db, -1, X_DH), n_mem=n_mem)
    hs = _matmul_residual(xo[:, :nq].reshape(ts, -1), w_xo, hs)
    y_sample = _ffn(hs, *f2, fg, final=True).reshape(db, nq, d)

    kvs = (1, db, nq, NSA_G, 2, NSA_DH)
    win_upd = jnp.concatenate([state_nsa_win_kv[0], s_win.reshape(kvs[1:])], axis=1)[:, nq:]
    sample_caches = (s_ckv.reshape(1, db, nq, -1), s_kr.reshape(1, db, nq, -1), s_cmp.reshape(kvs),
                     s_sel.reshape(kvs), win_upd[None])
    return (y_prompt, y_sample) + prompt_caches + sample_caches
```

```python
import functools
import math

import jax
import jax.numpy as jnp
from jax import lax
from jax.experimental import pallas as pl
from jax.experimental.pallas import tpu as pltpu

F32 = jnp.float32
BF16 = jnp.bfloat16
I32 = jnp.int32

D_MODEL = 1024
EPS = 1e-6
PAGE = 128
MLA_HEADS = 8
MLA_Q_LORA = 384
MLA_KV_LORA = 256
MLA_NOPE = 64
MLA_ROPE = 32
MLA_V = 64
MLA_SCALE = (MLA_NOPE + MLA_ROPE) ** -0.5
ROPE_BASE = 10000.0
NSA_HEADS = 8
NSA_G = 2
NSA_HPG = 4
NSA_DH = 64
NSA_SCALE = NSA_DH ** -0.5
CMP_BLOCK = 32
SEL_BLOCK = 64
SEL_TOPK = 16
WINDOW = 512
FORCE_BONUS = 4.0 * NSA_HPG
X_HEADS = 4
X_DH = 128
X_SCALE = X_DH ** -0.5
LANES = 128
LOG2E = math.log2(math.e)
NEG_INF = float("-inf")
MASK_BIG = 2.0 ** 60
VMEM_LIMIT = 56 * 1024 * 1024
TILE = 256
QTILE = 512

C_CQ = 0
C_CKV = 384
C_KR = 640
C_QN = 768
C_GT = 1792
C_MG = 2048
C_KVN = 4096
C_END = 4864
A_BLK, A_OFF, A_ONE0, A_ONE1 = 64, 65, 66, 67


def _cparams(n_axes):
    return pltpu.CompilerParams(dimension_semantics=("arbitrary",) * n_axes, vmem_limit_bytes=VMEM_LIMIT)


def _dot(a, b):
    return jnp.dot(a, b, preferred_element_type=F32)


def _dot_nt(a, b):
    return lax.dot_general(a, b, (((1,), (1,)), ((), ())), preferred_element_type=F32)


def _rms(x, g):
    ms = jnp.mean(x * x, axis=-1, keepdims=True)
    return x * lax.rsqrt(ms + EPS) * g


def _iota(shape, dim):
    return lax.broadcasted_iota(I32, shape, dim)


def _pow2_neg(e):
    return lax.bitcast_convert_type((127 - e) << 23, F32)


def _query_aug(slope, qpos, lane):
    s64 = slope * float(SEL_BLOCK)
    return jnp.where(lane == A_BLK, s64,
           jnp.where(lane == A_OFF, slope,
           jnp.where(lane == A_ONE0, -s64 * (qpos >> 6).astype(F32),
           jnp.where(lane == A_ONE1, -slope * (qpos & (SEL_BLOCK - 1)).astype(F32), 0.0))))


def _flash_step(s, v_t, m_ref, acc_ref, idx, exp_fn):
    m_prev = m_ref[idx]
    m_new = jnp.maximum(m_prev, jnp.max(s, axis=-1, keepdims=True))
    alpha = exp_fn(m_prev - m_new)
    p = exp_fn(s - jnp.concatenate([m_new] * (s.shape[1] // LANES), axis=1))
    acc_ref[idx] = alpha * acc_ref[idx] + _dot_nt(p.astype(BF16), v_t)
    m_ref[idx] = m_new


def _ffn_kernel(x_ref, g_ref, wg_ref, wu_ref, wd_ref, fg_ref, o_ref, h_ref, acc_ref, *, final):
    j = pl.program_id(1)

    @pl.when(j == 0)
    def _():
        h_ref[...] = _rms(x_ref[...], g_ref[...]).astype(BF16)
        acc_ref[...] = jnp.zeros(acc_ref.shape, F32)

    h = h_ref[...]
    a = _dot(h, wg_ref[...])
    u = _dot(h, wu_ref[...])
    act = (a * jax.nn.sigmoid(a)) * u
    acc_ref[...] += _dot(act.astype(BF16), wd_ref[...])

    @pl.when(j == pl.num_programs(1) - 1)
    def _():
        y = x_ref[...] + 0.5 * acc_ref[...]
        if final:
            y = _rms(y, fg_ref[...])
        o_ref[...] = y


def _ffn(x, g, wg, wu, wd, fg, *, final):
    t, d = x.shape
    ff = wg.shape[1]
    tm = min(t, 1024)
    tf = 256
    return pl.pallas_call(
        functools.partial(_ffn_kernel, final=final),
        grid=(t // tm, ff // tf),
        in_specs=[
            pl.BlockSpec((tm, d), lambda i, j: (i, 0)),
            pl.BlockSpec((1, d), lambda i, j: (0, 0)),
            pl.BlockSpec((d, tf), lambda i, j: (0, j)),
            pl.BlockSpec((d, tf), lambda i, j: (0, j)),
            pl.BlockSpec((tf, d), lambda i, j: (j, 0)),
            pl.BlockSpec((1, d), lambda i, j: (0, 0)),
        ],
        out_specs=pl.BlockSpec((tm, d), lambda i, j: (i, 0)),
        out_shape=jax.ShapeDtypeStruct((t, d), F32),
        scratch_shapes=[pltpu.VMEM((tm, d), BF16), pltpu.VMEM((tm, d), F32)],
        compiler_params=_cparams(2),
    )(x, g, wg, wu, wd, fg)


def _proj_common(x_ref, g_ref, win_ref, qn_ref, wuq_ref, kvn_ref, rc_ref, rs1_ref, rs2_ref,
                 qmla_ref, ckv_ref, qnsa_ref, gates_ref, merge_ref):
    n = _rms(x_ref[...], g_ref[...]).astype(BF16)

    def seg(a, b):
        return _dot(n, win_ref[:, a:b])

    rc, rs1, rs2 = rc_ref[...], rs1_ref[...], rs2_ref[...]

    def rope(blk):
        return blk * rc + pltpu.roll(blk, 112, 1) * rs1 + pltpu.roll(blk, 16, 1) * rs2

    cq = _rms(seg(C_CQ, C_CKV), qn_ref[...]).astype(BF16)
    q = _dot(cq, wuq_ref[...]) * (MLA_SCALE * LOG2E)
    for h in range(MLA_HEADS):
        qmla_ref[:, h * LANES:(h + 1) * LANES] = rope(q[:, h * LANES:(h + 1) * LANES]).astype(BF16)

    c_kv = _rms(seg(C_CKV, C_KR), kvn_ref[...])
    ckv_ref[...] = c_kv
    krb = rope(seg(C_KR, C_QN))
    qnsa_ref[...] = (seg(C_QN, C_GT) * NSA_SCALE).astype(BF16)
    gates_ref[...] = jax.nn.sigmoid(seg(C_GT, C_MG))
    merge_ref[...] = jax.nn.sigmoid(seg(C_MG, C_KVN))
    return n, seg, c_kv, krb


def _proj_prompt_kernel(x_ref, g_ref, win_ref, wkv_t_ref, qn_ref, wuq_ref, kvn_ref, wuk_t_ref, wuv_t_ref,
                        rc_ref, rs1_ref, rs2_ref,
                        qmla_ref, k_t_ref, v_t_ref, ckv_ref, kr_t_ref, qnsa_ref, cmp_t_ref, sel_t_ref, win_t_ref,
                        sk_ref, sv_ref, wk_ref, wv_ref, gates_ref, merge_ref):
    n, _, c_kv, krb = _proj_common(x_ref, g_ref, win_ref, qn_ref, wuq_ref, kvn_ref, rc_ref, rs1_ref, rs2_ref,
                                   qmla_ref, ckv_ref, qnsa_ref, gates_ref, merge_ref)
    tm = n.shape[0]
    cb = c_kv.astype(BF16)
    kr_t = jnp.transpose(krb)
    kr_t_ref[0] = kr_t[MLA_NOPE:MLA_NOPE + MLA_ROPE]
    rows = _iota((LANES, tm), 0)
    k_t = _dot_nt(wuk_t_ref[...], cb)
    v_t = _dot_nt(wuv_t_ref[...], cb)
    for h in range(MLA_HEADS):
        hs = slice(h * LANES, (h + 1) * LANES)
        k_t_ref[0, h, 0] = (k_t[hs] + kr_t).astype(BF16)
        v_t_ref[0, h, 0] = jnp.where(rows >= MLA_V, 1.0, v_t[hs]).astype(BF16)
    kv_t = _dot_nt(wkv_t_ref[...], n)
    cmp_t_ref[0] = kv_t[0:256]
    sel_t_ref[0] = kv_t[256:512]
    win_t_ref[0] = kv_t[512:768]
    for g in range(NSA_G):
        for base, k_out, v_out in ((256, sk_ref, sv_ref), (512, wk_ref, wv_ref)):
            blk = kv_t[base + g * LANES:base + (g + 1) * LANES]
            k_out[0, g, 0] = blk.astype(BF16)
            v_out[0, g, 0] = jnp.where(rows < NSA_DH, 1.0, blk).astype(BF16)


def _proj_sample_kernel(x_ref, g_ref, win_ref, qn_ref, wuq_ref, kvn_ref, rc_ref, rs1_ref, rs2_ref,
                        qmla_ref, ckv_ref, kr_ref, qnsa_ref, cmp_ref, sel_ref, win_o_ref, gates_ref, merge_ref):
    _, seg, _, krb = _proj_common(x_ref, g_ref, win_ref, qn_ref, wuq_ref, kvn_ref, rc_ref, rs1_ref, rs2_ref,
                                  qmla_ref, ckv_ref, qnsa_ref, gates_ref, merge_ref)
    kr_ref[...] = krb
    cmp_ref[...] = seg(C_KVN, C_KVN + 256)
    sel_ref[...] = seg(C_KVN + 256, C_KVN + 512)
    win_o_ref[...] = seg(C_KVN + 512, C_END)


def _mixer_project_prompt(x, b, s, g, w_in_p, w_kv_t, q_norm, w_uq_p, kv_norm, w_uk_t, w_uv_t, rc, rs1, rs2):
    t, d = x.shape
    tm = TILE
    nt = s // tm
    full = lambda a: pl.BlockSpec(a.shape, lambda i: (0,) * a.ndim)
    row = lambda w: pl.BlockSpec((tm, w), lambda i: (i, 0))
    tab = pl.BlockSpec((tm, LANES), lambda i: (i % nt, 0))
    fm = lambda r: pl.BlockSpec((1, r, tm), lambda i: (i // nt, 0, i % nt))
    tiles = lambda h: pl.BlockSpec((1, h, 1, LANES, tm), lambda i: (i // nt, 0, i % nt, 0, 0))
    tok = lambda w, dt: jax.ShapeDtypeStruct((t, w), dt)
    fms = lambda r: jax.ShapeDtypeStruct((b, r, s), F32)
    til = lambda h: jax.ShapeDtypeStruct((b, h, nt, LANES, tm), BF16)
    outs = [(row(1024), tok(1024, BF16)), (tiles(MLA_HEADS), til(MLA_HEADS)), (tiles(MLA_HEADS), til(MLA_HEADS)),
            (row(256), tok(256, F32)), (fm(MLA_ROPE), fms(MLA_ROPE)), (row(1024), tok(1024, BF16)),
            (fm(256), fms(256)), (fm(256), fms(256)), (fm(256), fms(256)),
            (tiles(NSA_G), til(NSA_G)), (tiles(NSA_G), til(NSA_G)), (tiles(NSA_G), til(NSA_G)), (tiles(NSA_G), til(NSA_G)),
            (row(256), tok(256, F32)), (row(2048), tok(2048, F32))]
    ins = (x, g, w_in_p, w_kv_t, q_norm, w_uq_p, kv_norm, w_uk_t, w_uv_t)
    return pl.pallas_call(
        _proj_prompt_kernel,
        grid=(t // tm,),
        in_specs=[row(d)] + [full(a) for a in ins[1:]] + [tab, tab, tab],
        out_specs=[o[0] for o in outs],
        out_shape=[o[1] for o in outs],
        compiler_params=_cparams(1),
    )(*ins, rc, rs1, rs2)


def _mixer_project_sample(x, g, w_in_p, q_norm, w_uq_p, kv_norm, rc, rs1, rs2):
    t, d = x.shape
    tm = min(t, TILE)
    full = lambda a: pl.BlockSpec(a.shape, lambda i: (0,) * a.ndim)
    row = lambda w: pl.BlockSpec((tm, w), lambda i: (i, 0))
    widths = [(1024, BF16), (256, F32), (128, F32), (1024, BF16), (256, F32), (256, F32), (256, F32), (256, F32),
              (2048, F32)]
    ins = (x, g, w_in_p, q_norm, w_uq_p, kv_norm)
    return pl.pallas_call(
        _proj_sample_kernel,
        grid=(t // tm,),
        in_specs=[row(d)] + [full(a) for a in ins[1:]] + [row(LANES)] * 3,
        out_specs=[row(w) for w, _ in widths],
        out_shape=[jax.ShapeDtypeStruct((t, w), dt) for w, dt in widths],
        compiler_params=_cparams(1),
    )(*ins, rc, rs1, rs2)


def _norm_mm_kernel(x_ref, g_ref, w_ref, o_ref, *, scale):
    y = _dot(_rms(x_ref[...], g_ref[...]).astype(BF16), w_ref[...])
    if scale != 1.0:
        y = y * scale
    o_ref[...] = y.astype(o_ref.dtype)


def _norm_matmul(x, g, w, *, scale=1.0, out_dtype=F32):
    t, d = x.shape
    n = w.shape[1]
    tm = min(t, 512)
    return pl.pallas_call(
        functools.partial(_norm_mm_kernel, scale=scale),
        grid=(t // tm,),
        in_specs=[pl.BlockSpec((tm, d), lambda i: (i, 0)), pl.BlockSpec((1, d), lambda i: (0, 0)),
                  pl.BlockSpec((d, n), lambda i: (0, 0))],
        out_specs=pl.BlockSpec((tm, n), lambda i: (i, 0)),
        out_shape=jax.ShapeDtypeStruct((t, n), out_dtype),
        compiler_params=_cparams(1),
    )(x, g, w)


def _mm_kernel(a_ref, w_ref, o_ref):
    o_ref[...] = _dot(a_ref[...], w_ref[...]).astype(o_ref.dtype)


def _matmul(a, w, *, out_dtype=F32):
    t, k = a.shape
    n = w.shape[1]
    tm = min(t, 512)
    return pl.pallas_call(
        _mm_kernel,
        grid=(t // tm,),
        in_specs=[pl.BlockSpec((tm, k), lambda i: (i, 0)), pl.BlockSpec((k, n), lambda i: (0, 0))],
        out_specs=pl.BlockSpec((tm, n), lambda i: (i, 0)),
        out_shape=jax.ShapeDtypeStruct((t, n), out_dtype),
        compiler_params=_cparams(1),
    )(a, w)


def _mm_res_kernel(a_ref, w_ref, x_ref, o_ref):
    o_ref[...] = x_ref[...] + _dot(a_ref[...], w_ref[...])


def _matmul_residual(a, w, x):
    t, k = a.shape
    n = w.shape[1]
    tm = min(t, 512)
    return pl.pallas_call(
        _mm_res_kernel,
        grid=(t // tm,),
        in_specs=[pl.BlockSpec((tm, k), lambda i: (i, 0)), pl.BlockSpec((k, n), lambda i: (0, 0)),
                  pl.BlockSpec((tm, n), lambda i: (i, 0))],
        out_specs=pl.BlockSpec((tm, n), lambda i: (i, 0)),
        out_shape=jax.ShapeDtypeStruct((t, n), F32),
        compiler_params=_cparams(1),
    )(a, w, x)


def _merge_kernel(oa_ref, ob_ref, mg_ref, x_ref, wa_ref, wb_ref, wo_ref, o_ref):
    d = x_ref.shape[1]
    mix = mg_ref[:, :d] * _dot(oa_ref[...], wa_ref[...]) + mg_ref[:, d:] * _dot(ob_ref[...], wb_ref[...])
    o_ref[...] = x_ref[...] + _dot(mix.astype(BF16), wo_ref[...])


def _merge(o_a, o_b, mg, x, w_a, w_b, w_o):
    t, d = x.shape
    tm = min(t, 512)
    row = lambda w: pl.BlockSpec((tm, w), lambda i: (i, 0))
    full = lambda a: pl.BlockSpec(a.shape, lambda i: (0, 0))
    return pl.pallas_call(
        _merge_kernel,
        grid=(t // tm,),
        in_specs=[row(o_a.shape[1]), row(o_b.shape[1]), row(2 * d), row(d), full(w_a), full(w_b), full(w_o)],
        out_specs=row(d),
        out_shape=jax.ShapeDtypeStruct((t, d), F32),
        compiler_params=_cparams(1),
    )(o_a, o_b, mg, x, w_a, w_b, w_o)


MLA_HPS = 4


def _mla_prompt_kernel(q_ref, k_ref, v_ref, o_ref, m_ref, acc_ref, *, tq, tk):
    i = pl.program_id(2)
    q = q_ref[0]
    qs = [q[:, h * LANES:(h + 1) * LANES] for h in range(MLA_HPS)]
    per_q = tq // tk
    m_ref[...] = jnp.full(m_ref.shape, NEG_INF, F32)
    acc_ref[...] = jnp.zeros(acc_ref.shape, F32)

    def tile(j, causal):
        if causal:
            ok = (j * tk + _iota((tq, tk), 1)) <= (i * tq + _iota((tq, tk), 0))
        for h in range(MLA_HPS):
            s = _dot(qs[h], k_ref[0, h, j])
            if causal:
                s = jnp.where(ok, s, -MASK_BIG)
            _flash_step(s, v_ref[0, h, j], m_ref, acc_ref, h, jnp.exp2)

    def body(j, c):
        tile(j, False)
        return c

    lax.fori_loop(0, per_q * i, body, 0)
    for d in range(per_q):
        tile(per_q * i + d, True)
    lane = _iota((1, LANES), 1)
    outs = []
    for h in range(MLA_HPS):
        a = acc_ref[h]
        outs.append(a / pltpu.roll(a, MLA_V, 1))
    for p in range(MLA_HPS // 2):
        pair = jnp.where(lane < MLA_V, outs[2 * p], pltpu.roll(outs[2 * p + 1], MLA_V, 1))
        o_ref[0, :, p * LANES:(p + 1) * LANES] = pair.astype(o_ref.dtype)


def _mla_prompt(q, k_t, v_t):
    b, s, _ = q.shape
    tq = QTILE
    nt = s // TILE
    kv_spec = pl.BlockSpec((1, MLA_HPS, nt, LANES, TILE), lambda bi, hq, i: (bi, hq, 0, 0, 0))
    return pl.pallas_call(
        functools.partial(_mla_prompt_kernel, tq=tq, tk=TILE),
        grid=(b, MLA_HEADS // MLA_HPS, s // tq),
        in_specs=[pl.BlockSpec((1, tq, MLA_HPS * LANES), lambda bi, hq, i: (bi, i, hq)), kv_spec, kv_spec],
        out_specs=pl.BlockSpec((1, tq, MLA_HPS * MLA_V), lambda bi, hq, i: (bi, i, hq)),
        out_shape=jax.ShapeDtypeStruct((b, s, MLA_HEADS * MLA_V), BF16),
        scratch_shapes=[pltpu.VMEM((MLA_HPS, tq, LANES), F32), pltpu.VMEM((MLA_HPS, tq, LANES), F32)],
        compiler_params=_cparams(3),
    )(q, k_t, v_t)


def _split_hi_lo(x):
    hi = x.astype(BF16)
    return hi, (x - hi.astype(F32)).astype(BF16)


def _compress_prompt_kernel(x_ref, a_ref, phi_ref, o_ref):
    hi, lo = _split_hi_lo(x_ref[0])
    means = _dot(jnp.concatenate([hi, lo], axis=1), a_ref[...])
    o_ref[0] = _dot(phi_ref[...], means.astype(BF16)).astype(o_ref.dtype)


def _compress_prompt(cmp_t, avg2, phi_t):
    b, w, s = cmp_t.shape
    nc = avg2.shape[1]
    return pl.pallas_call(
        _compress_prompt_kernel,
        grid=(b,),
        in_specs=[pl.BlockSpec((1, w, s), lambda i: (i, 0, 0)), pl.BlockSpec(avg2.shape, lambda i: (0, 0)),
                  pl.BlockSpec(phi_t.shape, lambda i: (0, 0))],
        out_specs=pl.BlockSpec((1, w, nc), lambda i: (i, 0, 0)),
        out_shape=jax.ShapeDtypeStruct((b, w, nc), BF16),
        compiler_params=_cparams(1),
    )(cmp_t, avg2, phi_t)


def _nsa_prompt_kernel(q_ref, c_ref, ctab_ref, sk_ref, sv_ref, wk_ref, wv_ref, ktab_ref, gt_ref, o_ref,
                       m_ref, acc_ref, *, tq, tk, n_sel):
    g = pl.program_id(1)
    i = pl.program_id(2)
    r = NSA_HPG * tq
    shift = tq.bit_length() - 1
    row = _iota((r, LANES), 0)
    lane_r = _iota((r, LANES), 1)
    qpos = i * tq + (row & (tq - 1))
    slope = _pow2_neg(NSA_HPG * g + (row >> shift) + 1)
    aq = _query_aug(slope, qpos, lane_r)
    q_st = jnp.concatenate([q_ref[0, :, h * LANES:(h + 1) * LANES] for h in range(NSA_HPG)], axis=0)
    q_plain = jnp.concatenate([q_st, aq.astype(BF16)], axis=1)

    kc = c_ref[0]
    nc = kc.shape[1]
    half = nc // 2
    lane_c = _iota((1, nc), 1)
    cblk = jnp.where(lane_c < half, 2 * lane_c, 2 * (lane_c - half) + 1)
    cmask = (cblk * CMP_BLOCK + (CMP_BLOCK - 1)) <= (i * tq + (_iota((r, nc), 0) & (tq - 1)))
    s = _dot(q_plain, jnp.concatenate([kc, ctab_ref[...]], axis=0))
    s = jnp.where(cmask, s, NEG_INF)
    mx = jnp.max(s, axis=-1, keepdims=True)
    mx = jnp.where(mx == NEG_INF, 0.0, mx)
    e = jnp.where(cmask, jnp.exp(s - mx), 0.0)
    p = e / jnp.maximum(jnp.sum(e, axis=-1, keepdims=True), 1e-30)
    o_cmp = _dot_nt(p.astype(BF16), kc)

    psum = p[0:tq] + p[tq:2 * tq] + p[2 * tq:3 * tq] + p[3 * tq:4 * tq]
    imp = psum + pltpu.roll(psum, half, 1)
    qp = i * tq + _iota((tq, 1), 0)
    forced = (lane_c == 0) | (lane_c == (qp >> 6))
    score = jnp.where(lane_c * SEL_BLOCK <= qp, imp + jnp.where(forced, FORCE_BONUS, 0.0), -1.0)
    st = jnp.transpose(score)[0:n_sel]
    jidx = _iota((n_sel, 1), 0)
    rank = jnp.zeros((n_sel, tq), F32)
    for ii in range(n_sel):
        ri = st[ii:ii + 1, :]
        first = jnp.where(jidx > ii, 1.0, 0.0)
        rank = rank + jnp.where(ri > st, 1.0, jnp.where(ri == st, first, 0.0))
    valid_t = (jidx * SEL_BLOCK) <= (i * tq + _iota((1, tq), 1))
    mt = jnp.where(valid_t, jnp.where(rank < float(min(SEL_TOPK, n_sel)), 1.0, 0.0), 0.0)
    if n_sel < LANES:
        mt = jnp.concatenate([mt, jnp.zeros((LANES - n_sel, tq), F32)], axis=0)
    msel = jnp.transpose(mt)
    mst = jnp.concatenate([msel] * NSA_HPG, axis=0)
    q_sel = jnp.concatenate([q_st, jnp.where(lane_r < n_sel, (mst - 1.0) * MASK_BIG, aq).astype(BF16)], axis=1)

    lane = _iota((1, LANES), 1)
    m_ref[...] = jnp.full(m_ref.shape, NEG_INF, F32)
    acc_ref[...] = jnp.zeros(acc_ref.shape, F32)
    per_q = tq // tk
    rel = (_iota((r, tk), 0) & (tq - 1)) - _iota((r, tk), 1)

    def dist_to(j):
        return rel + (i * tq - j * tk)

    def sel_tile(j, causal):
        sc = _dot(q_sel, jnp.concatenate([sk_ref[0, 0, j], ktab_ref[j]], axis=0))
        if causal:
            sc = jnp.where(dist_to(j) >= 0, sc, -MASK_BIG)
        _flash_step(sc, sv_ref[0, 0, j], m_ref, acc_ref, 0, jnp.exp)

    def sel_body(j, c):
        sel_tile(j, False)
        return c

    lax.fori_loop(0, per_q * i, sel_body, 0)
    for dd in range(per_q):
        sel_tile(per_q * i + dd, True)

    def win_body(j, c):
        in_window = lax.bitcast_convert_type(dist_to(j), jnp.uint32) <= jnp.uint32(WINDOW)
        sc = _dot(q_plain, jnp.concatenate([wk_ref[0, 0, j], ktab_ref[j]], axis=0))
        _flash_step(jnp.where(in_window, sc, -MASK_BIG), wv_ref[0, 0, j], m_ref, acc_ref, 1, jnp.exp)
        return c

    lax.fori_loop(jnp.maximum(per_q * i - WINDOW // tk, 0), per_q * (i + 1), win_body, 0)

    gt = gt_ref[0]
    heads = []
    for h in range(NSA_HPG):
        rs = slice(h * tq, (h + 1) * tq)
        a_sel, a_win = acc_ref[0, rs], acc_ref[1, rs]
        heads.append(gt[:, 3 * h:3 * h + 1] * o_cmp[rs]
                     + (gt[:, 3 * h + 1:3 * h + 2] / a_sel[:, 0:1]) * a_sel
                     + (gt[:, 3 * h + 2:3 * h + 3] / a_win[:, 0:1]) * a_win)
    for p2 in range(NSA_HPG // 2):
        pair = jnp.where(lane < NSA_DH, pltpu.roll(heads[2 * p2], NSA_DH, 1), heads[2 * p2 + 1])
        o_ref[0, :, p2 * LANES:(p2 + 1) * LANES] = pair.astype(o_ref.dtype)


def _nsa_prompt(q, ckcv_t, ctab, sk, sv, wk, wv, ktab, gates):
    b, s, _ = q.shape
    tq = QTILE
    nt = s // TILE
    nc = ckcv_t.shape[2]
    r = NSA_HPG * tq
    kv = pl.BlockSpec((1, 1, nt, LANES, TILE), lambda bi, gi, i: (bi, gi, 0, 0, 0))
    return pl.pallas_call(
        functools.partial(_nsa_prompt_kernel, tq=tq, tk=TILE, n_sel=s // SEL_BLOCK),
        grid=(b, NSA_G, s // tq),
        in_specs=[pl.BlockSpec((1, tq, NSA_HPG * LANES), lambda bi, gi, i: (bi, i, gi)),
                  pl.BlockSpec((1, LANES, nc), lambda bi, gi, i: (bi, gi, 0)),
                  pl.BlockSpec(ctab.shape, lambda bi, gi, i: (0, 0)),
                  kv, kv, kv, kv,
                  pl.BlockSpec(ktab.shape, lambda bi, gi, i: (0, 0, 0)),
                  pl.BlockSpec((1, tq, LANES), lambda bi, gi, i: (bi, i, gi))],
        out_specs=pl.BlockSpec((1, tq, NSA_HPG * NSA_DH), lambda bi, gi, i: (bi, i, gi)),
        out_shape=jax.ShapeDtypeStruct((b, s, NSA_HEADS * NSA_DH), BF16),
        scratch_shapes=[pltpu.VMEM((2, r, LANES), F32), pltpu.VMEM((2, r, LANES), F32)],
        compiler_params=_cparams(3),
    )(q, ckcv_t, ctab, sk, sv, wk, wv, ktab, gates)


def _xattn_prompt_kernel(q_ref, kv_ref, o_ref):
    hw = X_HEADS * X_DH
    for h in range(X_HEADS):
        cs = slice(h * X_DH, (h + 1) * X_DH)
        s = _dot_nt(q_ref[0, :, cs], kv_ref[0, :, cs])
        e = jnp.exp(s - jnp.max(s, axis=-1, keepdims=True))
        p = e / jnp.sum(e, axis=-1, keepdims=True)
        o_ref[0, :, cs] = _dot(p.astype(BF16), kv_ref[0, :, hw + h * X_DH:hw + (h + 1) * X_DH]).astype(o_ref.dtype)


def _xattn_prompt(q, kv):
    b, s, w = q.shape
    m = kv.shape[1]
    tq = min(s, 512)
    return pl.pallas_call(
        _xattn_prompt_kernel,
        grid=(b, s // tq),
        in_specs=[pl.BlockSpec((1, tq, w), lambda bi, i: (bi, i, 0)),
                  pl.BlockSpec((1, m, 2 * w), lambda bi, i: (bi, 0, 0))],
        out_specs=pl.BlockSpec((1, tq, w), lambda bi, i: (bi, i, 0)),
        out_shape=jax.ShapeDtypeStruct((b, s, w), BF16),
        compiler_params=_cparams(2),
    )(q, kv)


def _qabs_kernel(q_ref, w_ref, o_ref):
    o_ref[0] = _dot(q_ref[...], w_ref[0]).astype(o_ref.dtype)


def _mla_absorb_q(q_mla, w_abs):
    t = q_mla.shape[0]
    n = w_abs.shape[2]
    return pl.pallas_call(
        _qabs_kernel,
        grid=(MLA_HEADS,),
        in_specs=[pl.BlockSpec((t, LANES), lambda h: (0, h)), pl.BlockSpec((1, LANES, n), lambda h: (h, 0, 0))],
        out_specs=pl.BlockSpec((1, t, n), lambda h: (h, 0, 0)),
        out_shape=jax.ShapeDtypeStruct((MLA_HEADS, t, n), BF16),
        compiler_params=_cparams(1),
    )(q_mla, w_abs)


def _double_buffered(step, n_steps, copies):
    @pl.when(step == 0)
    def _():
        for cp in copies(step, 0):
            cp.start()

    @pl.when(step + 1 < n_steps)
    def _():
        for cp in copies(step + 1, (step + 1) & 1):
            cp.start()

    slot = step & 1
    for cp in copies(step, slot):
        cp.wait()
    return slot


def _pages_on_lanes(buf, slot, first, n):
    return jnp.concatenate([buf[slot, first + p] for p in range(n)], axis=1)


def _mla_sample_kernel(pt_ref, qa_ref, cn_ref, krn_ref, c_hbm, kr_hbm, o_ref, cbuf, krbuf, csem, krsem,
                       *, n_pages, db, nq):
    def copies(bb, slot):
        out = []
        for p in range(n_pages):
            page = pt_ref[p * db + bb]
            out.append(pltpu.make_async_copy(c_hbm.at[page], cbuf.at[slot, pl.ds(p * PAGE, PAGE), :], csem.at[slot]))
            out.append(pltpu.make_async_copy(kr_hbm.at[page], krbuf.at[slot, p], krsem.at[slot]))
        return out

    slot = _double_buffered(pl.program_id(0), pl.num_programs(0), copies)
    qa = qa_ref[0]
    qc, qr = qa[:, :MLA_KV_LORA], qa[:, MLA_KV_LORA:MLA_KV_LORA + MLA_ROPE]
    c = cbuf[slot].astype(BF16)
    kr_t = _pages_on_lanes(krbuf, slot, 0, n_pages).astype(BF16)
    s1 = _dot_nt(qc, c) + _dot(qr, kr_t)
    cn = cn_ref[0].astype(BF16)
    s2 = _dot_nt(qc, cn) + _dot_nt(qr, krn_ref[0].astype(BF16))
    qi = _iota((qa.shape[0], 1), 0) & (nq - 1)
    s2 = jnp.where(_iota((1, cn.shape[0]), 1) <= qi, s2, NEG_INF)
    mx = jnp.maximum(jnp.max(s1, axis=-1, keepdims=True), jnp.max(s2, axis=-1, keepdims=True))
    e1 = jnp.exp2(s1 - mx)
    e2 = jnp.exp2(s2 - mx)
    den = jnp.sum(e1, axis=-1, keepdims=True) + jnp.sum(e2, axis=-1, keepdims=True)
    o_ref[0] = (_dot(e1.astype(BF16), c) + _dot(e2.astype(BF16), cn)) / den


def _mla_sample(pt_flat, db, n_pages, qa, c_new, kr_new, ckv_pool, kr_pool_t, *, nq):
    r = qa.shape[1]
    past = n_pages * PAGE
    per_b = lambda a: pl.BlockSpec((1,) + a.shape[1:], lambda b, pt: (b, 0, 0))
    hbm = pl.BlockSpec(memory_space=pl.ANY)
    grid_spec = pltpu.PrefetchScalarGridSpec(
        num_scalar_prefetch=1,
        grid=(db,),
        in_specs=[per_b(qa), per_b(c_new), per_b(kr_new), hbm, hbm],
        out_specs=pl.BlockSpec((1, r, MLA_KV_LORA), lambda b, pt: (b, 0, 0)),
        scratch_shapes=[pltpu.VMEM((2, past, MLA_KV_LORA), F32), pltpu.VMEM((2, n_pages, MLA_ROPE, PAGE), F32),
                        pltpu.SemaphoreType.DMA((2,)), pltpu.SemaphoreType.DMA((2,))],
    )
    return pl.pallas_call(
        functools.partial(_mla_sample_kernel, n_pages=n_pages, db=db, nq=nq),
        grid_spec=grid_spec,
        out_shape=jax.ShapeDtypeStruct((db, r, MLA_KV_LORA), F32),
        compiler_params=_cparams(1),
    )(pt_flat, qa, c_new, kr_new, ckv_pool, kr_pool_t)


def _cmp_sample_kernel(pt_ref, a_ref, phi_ref, pool_hbm, o_ref, buf, sem, *, n_pages, db):
    def copies(bb, slot):
        return [pltpu.make_async_copy(pool_hbm.at[pt_ref[p * db + bb]], buf.at[slot, p], sem.at[slot])
                for p in range(n_pages)]

    slot = _double_buffered(pl.program_id(0), pl.num_programs(0), copies)
    pages = a_ref.shape[0] // (2 * PAGE)
    blocks = a_ref.shape[1]
    for c in range(n_pages // pages):
        hi, lo = _split_hi_lo(_pages_on_lanes(buf, slot, c * pages, pages))
        lhs = jnp.concatenate([hi, lo], axis=1)
        half = lhs.shape[0] // 2
        means = jnp.concatenate([_dot(lhs[:half], a_ref[...]), _dot(lhs[half:], a_ref[...])], axis=0)
        means = jnp.transpose(means).astype(BF16)
        o_ref[0, c * blocks:(c + 1) * blocks, :] = _dot(means, phi_ref[...]).astype(o_ref.dtype)


def _compress_sample(pt_flat, db, n_pages, cmp_pool_t, avg2, phi):
    w = cmp_pool_t.shape[1]
    past = n_pages * PAGE
    grid_spec = pltpu.PrefetchScalarGridSpec(
        num_scalar_prefetch=1,
        grid=(db,),
        in_specs=[pl.BlockSpec(avg2.shape, lambda b, pt: (0, 0)), pl.BlockSpec(phi.shape, lambda b, pt: (0, 0)),
                  pl.BlockSpec(memory_space=pl.ANY)],
        out_specs=pl.BlockSpec((1, past // CMP_BLOCK, w), lambda b, pt: (b, 0, 0)),
        scratch_shapes=[pltpu.VMEM((2, n_pages, w, PAGE), F32), pltpu.SemaphoreType.DMA((2,))],
    )
    return pl.pallas_call(
        functools.partial(_cmp_sample_kernel, n_pages=n_pages, db=db),
        grid_spec=grid_spec,
        out_shape=jax.ShapeDtypeStruct((db, past // CMP_BLOCK, w), BF16),
        compiler_params=_cparams(1),
    )(pt_flat, avg2, phi, cmp_pool_t)


def _nsa_cmp_sample_kernel(q_ref, c_ref, ctab_ref, o_ref, imp_ref, *, nq, past):
    r = q_ref.shape[2]
    nc = c_ref.shape[1]
    row = _iota((r, LANES), 0)
    lane_r = _iota((r, LANES), 1)
    qpos = past + (row & (nq - 1))
    cmask = (_iota((1, nc), 1) * CMP_BLOCK + (CMP_BLOCK - 1)) <= (past + (_iota((r, nc), 0) & (nq - 1)))
    for g in range(NSA_G):
        slope = _pow2_neg(NSA_HPG * g + (row >> (nq.bit_length() - 1)) + 1)
        q2 = jnp.concatenate([q_ref[0, g], _query_aug(slope, qpos, lane_r).astype(BF16)], axis=1)
        kc = c_ref[0, :, g * LANES:(g + 1) * LANES]
        s = _dot_nt(q2, jnp.concatenate([kc, ctab_ref[...]], axis=1))
        s = jnp.where(cmask, s, NEG_INF)
        mx = jnp.max(s, axis=-1, keepdims=True)
        mx = jnp.where(mx == NEG_INF, 0.0, mx)
        e = jnp.where(cmask, jnp.exp(s - mx), 0.0)
        p = e / jnp.maximum(jnp.sum(e, axis=-1, keepdims=True), 1e-30)
        o_ref[0, g] = _dot(p.astype(BF16), kc)
        ps = p
        for h in range(1, NSA_HPG):
            ps = ps + pltpu.roll(p, h * nq, 0)
        imp_ref[0, g] = ps + pltpu.roll(ps, nc - 1, 1)


def _nsa_cmp_sample(q, ckcv, ctab, *, nq, past):
    db, g, r, _ = q.shape
    nc = ckcv.shape[1]
    return pl.pallas_call(
        functools.partial(_nsa_cmp_sample_kernel, nq=nq, past=past),
        grid=(db,),
        in_specs=[pl.BlockSpec((1,) + q.shape[1:], lambda b: (b, 0, 0, 0)),
                  pl.BlockSpec((1,) + ckcv.shape[1:], lambda b: (b, 0, 0)),
                  pl.BlockSpec(ctab.shape, lambda b: (0, 0))],
        out_specs=[pl.BlockSpec((1, g, r, LANES), lambda b: (b, 0, 0, 0)),
                   pl.BlockSpec((1, g, r, nc), lambda b: (b, 0, 0, 0))],
        out_shape=[jax.ShapeDtypeStruct((db, g, r, LANES), F32), jax.ShapeDtypeStruct((db, g, r, nc), F32)],
        compiler_params=_cparams(1),
    )(q, ckcv, ctab)


def _topk_sample_kernel(imp_ref, o_ref, *, n_pick, last_blk):
    s = imp_ref[...]
    lane_i = _iota(s.shape, 1)
    lane = lane_i.astype(F32)
    s = jnp.where(((lane_i & 1) == 0) & (lane_i > 0), s, -1.0)
    olane = _iota(o_ref.shape, 1)
    out = jnp.where(olane == n_pick + 1, float(last_blk), 0.0)
    for it in range(n_pick):
        mx = jnp.max(s, axis=-1, keepdims=True)
        idx = jnp.min(jnp.where(s == mx, lane, float(s.shape[1])), axis=-1, keepdims=True)
        s = jnp.where(lane == idx, -1.0, s)
        out = jnp.where(olane == it, idx * 0.5, out)
    o_ref[...] = out.astype(I32)


def _topk_sample(imp, *, n_pick, last_blk):
    rows = imp.shape[0]
    return pl.pallas_call(
        functools.partial(_topk_sample_kernel, n_pick=n_pick, last_blk=last_blk),
        grid=(1,),
        in_specs=[pl.BlockSpec(imp.shape, lambda i: (0, 0))],
        out_specs=pl.BlockSpec((rows, LANES), lambda i: (0, 0)),
        out_shape=jax.ShapeDtypeStruct((rows, LANES), I32),
        compiler_params=_cparams(1),
    )(imp)


def _nsa_sel_sample_kernel(pt_ref, ids_ref, q_ref, new_ref, pool_hbm, o_ref, buf, sem, *, nblk, nq, db, past):
    def block_id(t, qq, c):
        return ids_ref[(t * nq + qq) * SEL_TOPK + c]

    def copies(t, slot):
        b = t >> 1
        rows = pl.ds(pl.multiple_of((t & 1) * LANES, LANES), LANES)
        out = []
        for qq in range(nq):
            for c in range(nblk):
                page = pt_ref[(block_id(t, qq, c) >> 1) * db + b]
                out.append(pltpu.make_async_copy(pool_hbm.at[page, rows, :], buf.at[slot, qq * nblk + c], sem.at[slot]))
        return out

    t = pl.program_id(0)
    slot = _double_buffered(t, pl.num_programs(0), copies)
    g = t & 1
    q = q_ref[0]
    r = q.shape[0]
    kv_t = _pages_on_lanes(buf, slot, 0, nq * nblk).astype(BF16)
    lane = _iota((1, PAGE), 1)
    kpos, keep, owner = [], [], []
    for qq in range(nq):
        for c in range(nblk):
            blk = block_id(t, qq, c)
            kpos.append((blk >> 1) * PAGE + lane)
            keep.append((lane >> 6) == (blk & 1))
        owner.append(jnp.full((1, nblk * PAGE), qq, I32))
    kpos = jnp.concatenate(kpos, axis=1)
    keep = jnp.concatenate(keep, axis=1)
    owner = jnp.concatenate(owner, axis=1)
    row = _iota((r, 1), 0)
    qi = row & (nq - 1)
    slope = _pow2_neg(NSA_HPG * g + (row >> (nq.bit_length() - 1)) + 1)
    s1 = _dot(q, kv_t) - slope * (past + qi - kpos).astype(F32)
    s1 = jnp.where(owner == qi, jnp.where(keep, s1, NEG_INF), NEG_INF)
    nw = new_ref[0].astype(BF16)
    tnew = _iota((1, nw.shape[0]), 1)
    s2 = _dot_nt(q, nw) - slope * (qi - tnew).astype(F32)
    s2 = jnp.where(tnew <= qi, s2, NEG_INF)
    mx = jnp.maximum(jnp.max(s1, axis=-1, keepdims=True), jnp.max(s2, axis=-1, keepdims=True))
    e1 = jnp.exp(s1 - mx)
    e2 = jnp.exp(s2 - mx)
    den = jnp.sum(e1, axis=-1, keepdims=True) + jnp.sum(e2, axis=-1, keepdims=True)
    o_ref[0] = (_dot_nt(e1.astype(BF16), kv_t) + _dot(e2.astype(BF16), nw)) / den


def _nsa_sel_sample(pt_flat, ids, db, q, sel_new, sel_pool_t, *, nblk, nq, past):
    steps, r, _ = q.shape
    grid_spec = pltpu.PrefetchScalarGridSpec(
        num_scalar_prefetch=2,
        grid=(steps,),
        in_specs=[pl.BlockSpec((1, r, LANES), lambda t, pt, idr: (t, 0, 0)),
                  pl.BlockSpec((1, sel_new.shape[1], LANES), lambda t, pt, idr: (t // NSA_G, 0, t % NSA_G)),
                  pl.BlockSpec(memory_space=pl.ANY)],
        out_specs=pl.BlockSpec((1, r, LANES), lambda t, pt, idr: (t, 0, 0)),
        scratch_shapes=[pltpu.VMEM((2, nq * nblk, LANES, PAGE), F32), pltpu.SemaphoreType.DMA((2,))],
    )
    return pl.pallas_call(
        functools.partial(_nsa_sel_sample_kernel, nblk=nblk, nq=nq, db=db, past=past),
        grid_spec=grid_spec,
        out_shape=jax.ShapeDtypeStruct((steps, r, LANES), F32),
        compiler_params=_cparams(1),
    )(pt_flat, ids.reshape(-1), q, sel_new, sel_pool_t)


def _nsa_win_sample_kernel(q_ref, st_ref, wtab_ref, new_ref, o_ref, *, nq, past):
    r = q_ref.shape[2]
    wbuf = st_ref.shape[2]
    row = _iota((r, LANES), 0)
    lane_r = _iota((r, LANES), 1)
    qpos = past + (row & (nq - 1))
    qi = _iota((r, 1), 0) & (nq - 1)
    dist_st = qi + wbuf - _iota((1, wbuf), 1)
    tnew = _iota((1, new_ref.shape[1]), 1)
    for g in range(NSA_G):
        slope = _pow2_neg(NSA_HPG * g + (row >> (nq.bit_length() - 1)) + 1)
        q = q_ref[0, g]
        q2 = jnp.concatenate([q, _query_aug(slope, qpos, lane_r).astype(BF16)], axis=1)
        st = st_ref[0, g * LANES:(g + 1) * LANES, :].astype(BF16)
        nw = new_ref[0, :, g * LANES:(g + 1) * LANES].astype(BF16)
        s1 = _dot(q2, jnp.concatenate([st, wtab_ref[...]], axis=0))
        s1 = jnp.where(dist_st <= WINDOW, s1, NEG_INF)
        s2 = _dot_nt(q, nw) - slope[:, :1] * (qi - tnew).astype(F32)
        s2 = jnp.where(tnew <= qi, s2, NEG_INF)
        mx = jnp.maximum(jnp.max(s1, axis=-1, keepdims=True), jnp.max(s2, axis=-1, keepdims=True))
        e1 = jnp.exp(s1 - mx)
        e2 = jnp.exp(s2 - mx)
        den = jnp.sum(e1, axis=-1, keepdims=True) + jnp.sum(e2, axis=-1, keepdims=True)
        o_ref[0, g] = (_dot_nt(e1.astype(BF16), st) + _dot(e2.astype(BF16), nw)) / den


def _nsa_win_sample(q, state_t, wtab, win_new, *, nq, past):
    db, g, r, _ = q.shape
    return pl.pallas_call(
        functools.partial(_nsa_win_sample_kernel, nq=nq, past=past),
        grid=(db,),
        in_specs=[pl.BlockSpec((1,) + q.shape[1:], lambda b: (b, 0, 0, 0)),
                  pl.BlockSpec((1,) + state_t.shape[1:], lambda b: (b, 0, 0)),
                  pl.BlockSpec(wtab.shape, lambda b: (0, 0)),
                  pl.BlockSpec((1,) + win_new.shape[1:], lambda b: (b, 0, 0))],
        out_specs=pl.BlockSpec((1, g, r, LANES), lambda b: (b, 0, 0, 0)),
        out_shape=jax.ShapeDtypeStruct((db, g, r, LANES), F32),
        compiler_params=_cparams(1),
    )(q, state_t, wtab, win_new)


def _gate_kernel(g_ref, a_ref, b_ref, c_ref, o_ref):
    o_ref[...] = (g_ref[0] * a_ref[...] + g_ref[1] * b_ref[...] + g_ref[2] * c_ref[...]).astype(o_ref.dtype)


def _nsa_gate_sample(gexp, o_cmp, o_sel, o_win):
    t, w = o_cmp.shape
    row = pl.BlockSpec((t, w), lambda i: (0, 0))
    return pl.pallas_call(
        _gate_kernel,
        grid=(1,),
        in_specs=[pl.BlockSpec((3, t, w), lambda i: (0, 0, 0)), row, row, row],
        out_specs=row,
        out_shape=jax.ShapeDtypeStruct((t, w), BF16),
        compiler_params=_cparams(1),
    )(gexp, o_cmp, o_sel, o_win)


def _xattn_sample_kernel(q_ref, kv_ref, o_ref, *, bc, n_mem):
    r = q_ref.shape[1]
    rows = 2 * X_HEADS * n_mem
    is_key = (_iota((X_HEADS * r, rows), 1) & (2 * X_HEADS - 1)) == (_iota((X_HEADS * r, rows), 0) >> (r.bit_length() - 1))
    for bi in range(bc):
        x = kv_ref[bi].astype(BF16)
        q = jnp.concatenate([q_ref[bi, :, h * X_DH:(h + 1) * X_DH] for h in range(X_HEADS)], axis=0)
        s = jnp.where(is_key, _dot_nt(q, x), NEG_INF)
        e = jnp.exp(s - jnp.max(s, axis=-1, keepdims=True))
        p = e / jnp.sum(e, axis=-1, keepdims=True)
        o = _dot(pltpu.roll(p, X_HEADS, 1).astype(BF16), x)
        for h in range(X_HEADS):
            o_ref[bi, :, h * X_DH:(h + 1) * X_DH] = o[h * r:(h + 1) * r].astype(o_ref.dtype)


def _xattn_sample(q, kv, *, n_mem):
    db, r, w = q.shape
    bc = 4 if db % 4 == 0 else 1
    return pl.pallas_call(
        functools.partial(_xattn_sample_kernel, bc=bc, n_mem=n_mem),
        grid=(db // bc,),
        in_specs=[pl.BlockSpec((bc, r, w), lambda i: (i, 0, 0)),
                  pl.BlockSpec((bc,) + kv.shape[1:], lambda i: (i, 0, 0))],
        out_specs=pl.BlockSpec((bc, r, w), lambda i: (i, 0, 0)),
        out_shape=jax.ShapeDtypeStruct((db, r, w), BF16),
        compiler_params=_cparams(1),
    )(q, kv)


def _prep_weights(w_in, mla_w_uq, mla_w_uk, mla_w_uv, nsa_phi_k, nsa_phi_v):
    d = w_in.shape[0]
    z = lambda n: jnp.zeros((d, n), F32)
    o = 0
    cq, o = w_in[:, o:o + 384], o + 384
    ckv, o = w_in[:, o:o + 256], o + 256
    kr, o = w_in[:, o:o + 32], o + 32
    qn, o = w_in[:, o:o + 512], o + 512
    kvn, o = w_in[:, o:o + 768], o + 768
    gn, o = w_in[:, o:o + 24], o + 24
    mg = w_in[:, o:]
    qn_p = jnp.pad(qn.reshape(d, NSA_HEADS, NSA_DH), ((0, 0), (0, 0), (0, LANES - NSA_DH))).reshape(d, -1)
    w_in_p = jnp.concatenate([cq, ckv, z(64), kr, z(32), qn_p, gn[:, :12], z(116), gn[:, 12:], z(116), mg, kvn],
                             axis=1).astype(BF16)
    w_kv_t = jnp.transpose(kvn).astype(BF16)
    w_uq_p = jnp.pad(mla_w_uq, ((0, 0), (0, 0), (0, LANES - MLA_NOPE - MLA_ROPE))).reshape(MLA_Q_LORA, -1).astype(BF16)
    pad_t = lambda w: jnp.pad(jnp.transpose(w, (1, 2, 0)), ((0, 0), (0, LANES - w.shape[2]), (0, 0))).reshape(
        MLA_HEADS * LANES, MLA_KV_LORA).astype(BF16)
    w_uk_t, w_uv_t = pad_t(mla_w_uk), pad_t(mla_w_uv)
    w_abs = jnp.zeros((MLA_HEADS, LANES, 384), F32)
    w_abs = w_abs.at[:, :MLA_NOPE, :MLA_KV_LORA].set(jnp.transpose(mla_w_uk, (1, 2, 0)))
    w_abs = w_abs.at[:, MLA_NOPE:MLA_NOPE + MLA_ROPE, MLA_KV_LORA:MLA_KV_LORA + MLA_ROPE].set(jnp.eye(MLA_ROPE, dtype=F32))
    w_ov = jnp.zeros((MLA_HEADS, MLA_KV_LORA, MLA_HEADS, MLA_V), F32)
    for h in range(MLA_HEADS):
        w_ov = w_ov.at[h, :, h, :].set(mla_w_uv[:, h, :])
    w_ov = w_ov.reshape(MLA_HEADS * MLA_KV_LORA, MLA_HEADS * MLA_V)
    phi_t = jnp.zeros((4, NSA_DH, 4, NSA_DH), F32)
    for g in range(NSA_G):
        phi_t = phi_t.at[2 * g, :, 2 * g, :].set(jnp.transpose(nsa_phi_k[g]))
        phi_t = phi_t.at[2 * g + 1, :, 2 * g + 1, :].set(jnp.transpose(nsa_phi_v[g]))
    phi_t = phi_t.reshape(4 * NSA_DH, 4 * NSA_DH)
    return w_in_p, w_kv_t, w_uq_p, w_uk_t, w_uv_t, w_abs.astype(BF16), w_ov.astype(BF16), phi_t.astype(BF16)


def _rope_tables(pos):
    half = MLA_ROPE // 2
    inv = ROPE_BASE ** (-jnp.arange(half, dtype=F32) / half)
    ang = pos.astype(F32)[:, None] * inv[None, :]
    cos, sin = jnp.cos(ang), jnp.sin(ang)
    n = pos.shape[0]
    one, zero = jnp.ones((n, MLA_NOPE), F32), jnp.zeros((n, half), F32)
    tail = jnp.zeros((n, LANES - MLA_NOPE - MLA_ROPE), F32)
    rc = jnp.concatenate([one, cos, cos, tail], axis=1)
    rs1 = jnp.concatenate([0 * one, -sin, zero, tail], axis=1)
    rs2 = jnp.concatenate([0 * one, zero, sin, tail], axis=1)
    return rc, rs1, rs2


def _key_rows(pos):
    rows = jnp.arange(LANES, dtype=I32)[:, None]
    blk = (pos >> 6)[None, :]
    t = jnp.where((rows == blk) & (rows < SEL_BLOCK), 1.0, 0.0)
    t = jnp.where(rows == A_BLK, blk.astype(F32), t)
    t = jnp.where(rows == A_OFF, (pos & (SEL_BLOCK - 1))[None, :].astype(F32), t)
    t = jnp.where((rows == A_ONE0) | (rows == A_ONE1), 1.0, t)
    return t.astype(BF16)


def _avg_matrix(n_keys, order):
    blk_of_key = jnp.tile(jnp.arange(n_keys, dtype=I32) // CMP_BLOCK, 2)[:, None]
    return jnp.where(blk_of_key == order[None, :], 1.0 / CMP_BLOCK, 0.0).astype(BF16)


def _even_odd_order(n):
    return jnp.concatenate([jnp.arange(0, n, 2, dtype=I32), jnp.arange(1, n, 2, dtype=I32)])


def kernel(x_prompt, x_sample, mem_prompt, cache_mla_ckv, cache_mla_krope, cache_nsa_cmp_kv, cache_nsa_sel_kv,
           state_nsa_win_kv, cache_xattn_kv, page_table, ffn1_norm, ffn1_w_gate, ffn1_w_up, ffn1_w_down, mix_norm,
           w_in, mla_q_norm, mla_w_uq, mla_kv_norm, mla_w_uk, mla_w_uv, nsa_phi_k, nsa_phi_v, w_br_mla, w_br_nsa,
           w_out, xattn_norm, xattn_mem_norm, xattn_w_q, xattn_w_kv, xattn_w_o, ffn2_norm, ffn2_w_gate, ffn2_w_up,
           ffn2_w_down, final_norm):
    assert x_prompt.shape[2] == D_MODEL and ffn1_norm.shape[0] == 1
    b, s, d = x_prompt.shape
    db, nq, _ = x_sample.shape
    n_pages = page_table.shape[1]
    past = n_pages * PAGE
    n_mem = mem_prompt.shape[1]
    assert s % QTILE == 0 and past % SEL_BLOCK == 0 and nq <= 8 and NSA_G == 2
    g2 = lambda a: a.reshape(1, -1)
    bf = lambda a: a.astype(BF16)

    w_in_p, w_kv_t, w_uq_p, w_uk_t, w_uv_t, w_abs, w_ov, phi_t = _prep_weights(
        w_in[0], mla_w_uq[0], mla_w_uk[0], mla_w_uv[0], nsa_phi_k[0], nsa_phi_v[0])
    f1 = (g2(ffn1_norm[0]), bf(ffn1_w_gate[0]), bf(ffn1_w_up[0]), bf(ffn1_w_down[0]))
    f2 = (g2(ffn2_norm[0]), bf(ffn2_w_gate[0]), bf(ffn2_w_up[0]), bf(ffn2_w_down[0]))
    fg = g2(final_norm)
    w_a, w_b, w_o = bf(w_br_mla[0]), bf(w_br_nsa[0]), bf(w_out[0])
    w_xq, w_xo = bf(xattn_w_q[0]), bf(xattn_w_o[0])
    w_xkv = bf(xattn_w_kv[0].reshape(d, -1))
    mixg, qng, kvg = g2(mix_norm[0]), g2(mla_q_norm[0]), g2(mla_kv_norm[0])

    t = b * s
    nt = s // TILE
    hp = _ffn(x_prompt.reshape(t, d), *f1, fg, final=False)
    (q_mla, k_t, v_t, p_ckv, p_kr_t, q_nsa, p_cmp_t, p_sel_t, p_win_t, sk, sv, wk, wv, gates, merge) = (
        _mixer_project_prompt(hp, b, s, mixg, w_in_p[:, :C_KVN], w_kv_t, qng, w_uq_p, kvg, w_uk_t, w_uv_t,
                              *_rope_tables(jnp.arange(s))))
    o_a = _mla_prompt(q_mla.reshape(b, s, -1), k_t, v_t).reshape(t, -1)

    nc = s // CMP_BLOCK
    order = _even_odd_order(nc)
    ckcv_t = _compress_prompt(p_cmp_t, _avg_matrix(s, order), phi_t)
    ctab = _key_rows(order * CMP_BLOCK + (CMP_BLOCK - 1))
    ktab = jnp.transpose(_key_rows(jnp.arange(s, dtype=I32)).reshape(LANES, nt, TILE), (1, 0, 2))
    o_b = _nsa_prompt(q_nsa.reshape(b, s, -1), ckcv_t, ctab, sk, sv, wk, wv, ktab, gates.reshape(b, s, -1)).reshape(t, -1)
    hp = _merge(o_a, o_b, merge, hp, w_a, w_b, w_o)

    kv_mem = _norm_matmul(mem_prompt.reshape(b * n_mem, d), g2(xattn_mem_norm[0]), w_xkv)
    xq = _norm_matmul(hp, g2(xattn_norm[0]), w_xq, scale=X_SCALE, out_dtype=BF16)
    xo = _xattn_prompt(xq.reshape(b, s, -1), bf(kv_mem).reshape(b, n_mem, -1))
    hp = _matmul_residual(xo.reshape(t, -1), w_xo, hp)
    y_prompt = _ffn(hp, *f2, fg, final=True).reshape(b, s, d)

    wlen = min(WINDOW, s)
    kv_out = lambda a: jnp.transpose(a.reshape(1, b, NSA_G, 2, NSA_DH, a.shape[-1]), (0, 1, 5, 2, 3, 4))
    prompt_caches = (p_ckv.reshape(1, b, s, -1), jnp.transpose(p_kr_t, (0, 2, 1))[None], kv_out(p_cmp_t),
                     kv_out(p_sel_t), kv_out(p_win_t[:, :, s - wlen:]),
                     kv_mem.reshape(1, b, n_mem, 2, X_HEADS, X_DH))

    ts = db * nq
    hs = _ffn(x_sample.reshape(ts, d), *f1, fg, final=False)
    pos_s = jnp.tile(past + jnp.arange(nq), db)
    (q_mla, s_ckv, s_krb, q_nsa, s_cmp, s_sel, s_win, gates, merge) = _mixer_project_sample(
        hs, mixg, w_in_p, qng, w_uq_p, kvg, *_rope_tables(pos_s))
    s_kr = s_krb[:, MLA_NOPE:MLA_NOPE + MLA_ROPE]
    pad_rows = lambda a: jnp.pad(a.reshape(db, nq, -1), ((0, 0), (0, 8 - nq), (0, 0)))
    pt_flat = jnp.transpose(page_table).reshape(-1)

    qa = _mla_absorb_q(q_mla, w_abs)
    qa = jnp.transpose(qa.reshape(MLA_HEADS, db, nq, -1), (1, 0, 2, 3)).reshape(db, MLA_HEADS * nq, -1)
    kr_pool_t = jnp.transpose(cache_mla_krope[0], (0, 2, 1))
    o_lat = _mla_sample(pt_flat, db, n_pages, qa, pad_rows(s_ckv), pad_rows(s_kr), cache_mla_ckv[0], kr_pool_t, nq=nq)
    o_lat = jnp.transpose(o_lat.reshape(db, MLA_HEADS, nq, -1), (0, 2, 1, 3)).reshape(ts, -1)
    o_a = _matmul(bf(o_lat), w_ov, out_dtype=BF16)

    fm_pool = lambda c: jnp.transpose(c[0], (0, 2, 3, 4, 1)).reshape(c.shape[1], -1, PAGE)
    cmp_pool_t, sel_pool_t = fm_pool(cache_nsa_cmp_kv), fm_pool(cache_nsa_sel_kv)
    keys_per_dot = min(past, 32 * PAGE)
    avg2_s = _avg_matrix(keys_per_dot, jnp.arange(keys_per_dot // CMP_BLOCK, dtype=I32))
    ckcv_s = _compress_sample(pt_flat, db, n_pages, cmp_pool_t, avg2_s, jnp.transpose(phi_t))
    ncs = past // CMP_BLOCK
    ctab_s = jnp.transpose(_key_rows(jnp.arange(ncs, dtype=I32) * CMP_BLOCK + (CMP_BLOCK - 1)))
    qn = jnp.transpose(q_nsa.reshape(db, nq, NSA_G, NSA_HPG, LANES), (0, 2, 3, 1, 4))
    q_rows = qn.reshape(db, NSA_G, NSA_HPG * nq, LANES)
    o_cmp, imp = _nsa_cmp_sample(q_rows, ckcv_s, ctab_s, nq=nq, past=past)
    n_sel = -(-(past + nq) // SEL_BLOCK)
    n_pick = min(SEL_TOPK, n_sel) - 2
    ids = _topk_sample(imp[:, :, :nq].reshape(db * NSA_G * nq, -1), n_pick=n_pick, last_blk=n_sel - 1)
    o_sel = _nsa_sel_sample(pt_flat, ids[:, :SEL_TOPK], db, q_rows.reshape(db * NSA_G, -1, LANES), pad_rows(s_sel),
                            sel_pool_t, nblk=n_pick + 1, nq=nq, past=past).reshape(db, NSA_G, -1, LANES)
    wbuf = state_nsa_win_kv.shape[2]
    state_t = jnp.transpose(state_nsa_win_kv[0], (0, 2, 3, 4, 1)).reshape(db, -1, wbuf)
    wtab = _key_rows(past - wbuf + jnp.arange(wbuf, dtype=I32))
    o_win = _nsa_win_sample(q_rows, state_t, wtab, pad_rows(s_win), nq=nq, past=past)
    to_tok = lambda a: jnp.transpose(a[..., NSA_DH:].reshape(db, NSA_G, NSA_HPG, nq, NSA_DH), (0, 3, 1, 2, 4)).reshape(ts, -1)
    gts =jnp.stack([gates[:, :12], gates[:, LANES:LANES + 12]], axis=1).reshape(ts, NSA_HEADS, 3)
    gexp = jnp.transpose(jnp.broadcast_to(gts[:, :, :, None], (ts, NSA_HEADS, 3, NSA_DH)), (2, 0, 1, 3)).reshape(3, ts, -1)
    o_b = _nsa_gate_sample(gexp, to_tok(o_cmp), to_tok(o_sel), to_tok(o_win))
    hs = _merge(o_a, o_b, merge, hs, w_a, w_b, w_o)

    xq = _norm_matmul(hs, g2(xattn_norm[0]), w_xq, scale=X_SCALE, out_dtype=BF16)
    xo = _xattn_sample(pad_rows(xq), cache_xattn_kv[0].reshape(db, -1, X_DH), n_mem=n_mem)
    hs = _matmul_residual(xo[:, :nq].reshape(ts, -1), w_xo, hs)
    y_sample = _ffn(hs, *f2, fg, final=True).reshape(db, nq, d)

    kvs = (1, db, nq, NSA_G, 2, NSA_DH)
    win_upd = jnp.concatenate([state_nsa_win_kv[0], s_win.reshape(kvs[1:])], axis=1)[:, nq:]
    sample_caches = (s_ckv.reshape(1, db, nq, -1), s_kr.reshape(1, db, nq, -1), s_cmp.reshape(kvs),
                     s_sel.reshape(kvs), win_upd[None])
    return (y_prompt, y_sample) + prompt_caches + sample_caches
```

```python
import functools
import math

import jax
import jax.numpy as jnp
import numpy as np
from jax import lax
from jax.experimental import pallas as pl
from jax.experimental.pallas import tpu as pltpu

F32 = jnp.float32
BF16 = jnp.bfloat16
I32 = jnp.int32

D_MODEL = 1024
EPS = 1e-6
PAGE = 128
MLA_HEADS = 8
MLA_Q_LORA = 384
MLA_KV_LORA = 256
MLA_NOPE = 64
MLA_ROPE = 32
MLA_V = 64
MLA_SCALE = (MLA_NOPE + MLA_ROPE) ** -0.5
ROPE_BASE = 10000.0
NSA_HEADS = 8
NSA_G = 2
NSA_HPG = 4
NSA_DH = 64
NSA_SCALE = NSA_DH ** -0.5
CMP_BLOCK = 32
SEL_BLOCK = 64
SEL_TOPK = 16
WINDOW = 512
FORCE_BONUS = 4.0 * NSA_HPG
X_HEADS = 4
X_DH = 128
X_SCALE = X_DH ** -0.5
LANES = 128
LOG2E = math.log2(math.e)
NEG_INF = float("-inf")
MASK_BIG = 2.0 ** 60
VMEM_LIMIT = 56 * 1024 * 1024
TILE = 256
QTILE = 512

C_CQ = 0
C_CKV = 384
C_KR = 640
C_QN = 768
C_GT = 1792
C_MG = 2048
C_KVN = 4096
C_END = 4864
A_BLK, A_OFF, A_ONE0, A_ONE1 = 64, 65, 66, 67


def _cparams(n_axes):
    return pltpu.CompilerParams(dimension_semantics=("arbitrary",) * n_axes, vmem_limit_bytes=VMEM_LIMIT)


def _dot(a, b):
    return jnp.dot(a, b, preferred_element_type=F32)


def _dot_nt(a, b):
    return lax.dot_general(a, b, (((1,), (1,)), ((), ())), preferred_element_type=F32)


def _rms(x, g):
    ms = jnp.mean(x * x, axis=-1, keepdims=True)
    return x * lax.rsqrt(ms + EPS) * g


def _iota(shape, dim):
    return lax.broadcasted_iota(I32, shape, dim)


def _pow2_neg(e):
    return lax.bitcast_convert_type((127 - e) << 23, F32)


def _query_aug(slope, qpos, lane):
    s64 = slope * float(SEL_BLOCK)
    return jnp.where(lane == A_BLK, s64,
           jnp.where(lane == A_OFF, slope,
           jnp.where(lane == A_ONE0, -s64 * (qpos >> 6).astype(F32),
           jnp.where(lane == A_ONE1, -slope * (qpos & (SEL_BLOCK - 1)).astype(F32), 0.0))))


def _flash_step(s, v_t, m_ref, acc_ref, idx, exp_fn):
    m_prev = m_ref[idx]
    m_new = jnp.maximum(m_prev, jnp.max(s, axis=-1, keepdims=True))
    alpha = exp_fn(m_prev - m_new)
    p = exp_fn(s - jnp.concatenate([m_new] * (s.shape[1] // LANES), axis=1))
    acc_ref[idx] = alpha * acc_ref[idx] + _dot_nt(p.astype(BF16), v_t)
    m_ref[idx] = m_new


def _ffn_kernel(x_ref, g_ref, wg_ref, wu_ref, wd_ref, fg_ref, o_ref, h_ref, acc_ref, *, final):
    j = pl.program_id(1)

    @pl.when(j == 0)
    def _():
        h_ref[...] = _rms(x_ref[...], g_ref[...]).astype(BF16)
        acc_ref[...] = jnp.zeros(acc_ref.shape, F32)

    h = h_ref[...]
    a = _dot(h, wg_ref[...])
    u = _dot(h, wu_ref[...])
    act = (a * jax.nn.sigmoid(a)) * u
    acc_ref[...] += _dot(act.astype(BF16), wd_ref[...])

    @pl.when(j == pl.num_programs(1) - 1)
    def _():
        y = x_ref[...] + 0.5 * acc_ref[...]
        if final:
            y = _rms(y, fg_ref[...])
        o_ref[...] = y


def _ffn(x, g, wg, wu, wd, fg, *, final):
    t, d = x.shape
    ff = wg.shape[1]
    tm = min(t, 1024)
    tf = 256
    return pl.pallas_call(
        functools.partial(_ffn_kernel, final=final),
        grid=(t // tm, ff // tf),
        in_specs=[
            pl.BlockSpec((tm, d), lambda i, j: (i, 0)),
            pl.BlockSpec((1, d), lambda i, j: (0, 0)),
            pl.BlockSpec((d, tf), lambda i, j: (0, j)),
            pl.BlockSpec((d, tf), lambda i, j: (0, j)),
            pl.BlockSpec((tf, d), lambda i, j: (j, 0)),
            pl.BlockSpec((1, d), lambda i, j: (0, 0)),
        ],
        out_specs=pl.BlockSpec((tm, d), lambda i, j: (i, 0)),
        out_shape=jax.ShapeDtypeStruct((t, d), F32),
        scratch_shapes=[pltpu.VMEM((tm, d), BF16), pltpu.VMEM((tm, d), F32)],
        compiler_params=_cparams(2),
    )(x, g, wg, wu, wd, fg)


def _proj_common(x_ref, g_ref, win_ref, qn_ref, wuq_ref, kvn_ref, rc_ref, rs1_ref, rs2_ref,
                 qmla_ref, ckv_ref, qnsa_ref, gates_ref, merge_ref):
    n = _rms(x_ref[...], g_ref[...]).astype(BF16)

    def seg(a, b):
        return _dot(n, win_ref[:, a:b])

    rc, rs1, rs2 = rc_ref[...], rs1_ref[...], rs2_ref[...]

    def rope(blk):
        return blk * rc + pltpu.roll(blk, 112, 1) * rs1 + pltpu.roll(blk, 16, 1) * rs2

    cq = _rms(seg(C_CQ, C_CKV), qn_ref[...]).astype(BF16)
    q = _dot(cq, wuq_ref[...]) * (MLA_SCALE * LOG2E)
    for h in range(MLA_HEADS):
        qmla_ref[:, h * LANES:(h + 1) * LANES] = rope(q[:, h * LANES:(h + 1) * LANES]).astype(BF16)

    c_kv = _rms(seg(C_CKV, C_KR), kvn_ref[...])
    ckv_ref[...] = c_kv
    krb = rope(seg(C_KR, C_QN))
    qnsa_ref[...] = (seg(C_QN, C_GT) * NSA_SCALE).astype(BF16)
    gates_ref[...] = jax.nn.sigmoid(seg(C_GT, C_MG))
    merge_ref[...] = jax.nn.sigmoid(seg(C_MG, C_KVN))
    return n, seg, c_kv, krb


def _proj_prompt_kernel(x_ref, g_ref, win_ref, wkv_t_ref, qn_ref, wuq_ref, kvn_ref, wuk_t_ref, wuv_t_ref,
                        rc_ref, rs1_ref, rs2_ref,
                        qmla_ref, k_t_ref, v_t_ref, ckv_ref, kr_t_ref, qnsa_ref, cmp_t_ref, sel_t_ref, win_t_ref,
                        sk_ref, sv_ref, wk_ref, wv_ref, gates_ref, merge_ref):
    n, _, c_kv, krb = _proj_common(x_ref, g_ref, win_ref, qn_ref, wuq_ref, kvn_ref, rc_ref, rs1_ref, rs2_ref,
                                   qmla_ref, ckv_ref, qnsa_ref, gates_ref, merge_ref)
    tm = n.shape[0]
    cb = c_kv.astype(BF16)
    kr_t = jnp.transpose(krb)
    kr_t_ref[0] = kr_t[MLA_NOPE:MLA_NOPE + MLA_ROPE]
    rows = _iota((LANES, tm), 0)
    k_t = _dot_nt(wuk_t_ref[...], cb)
    v_t = _dot_nt(wuv_t_ref[...], cb)
    for h in range(MLA_HEADS):
        hs = slice(h * LANES, (h + 1) * LANES)
        k_t_ref[0, h, 0] = (k_t[hs] + kr_t).astype(BF16)
        v_t_ref[0, h, 0] = jnp.where(rows >= MLA_V, 1.0, v_t[hs]).astype(BF16)
    kv_t = _dot_nt(wkv_t_ref[...], n)
    cmp_t_ref[0] = kv_t[0:256]
    sel_t_ref[0] = kv_t[256:512]
    win_t_ref[0] = kv_t[512:768]
    for g in range(NSA_G):
        for base, k_out, v_out in ((256, sk_ref, sv_ref), (512, wk_ref, wv_ref)):
            blk = kv_t[base + g * LANES:base + (g + 1) * LANES]
            k_out[0, g, 0] = blk.astype(BF16)
            v_out[0, g, 0] = jnp.where(rows < NSA_DH, 1.0, blk).astype(BF16)


def _proj_sample_kernel(x_ref, g_ref, win_ref, qn_ref, wuq_ref, kvn_ref, rc_ref, rs1_ref, rs2_ref,
                        qmla_ref, ckv_ref, kr_ref, qnsa_ref, cmp_ref, sel_ref, win_o_ref, gates_ref, merge_ref):
    _, seg, _, krb = _proj_common(x_ref, g_ref, win_ref, qn_ref, wuq_ref, kvn_ref, rc_ref, rs1_ref, rs2_ref,
                                  qmla_ref, ckv_ref, qnsa_ref, gates_ref, merge_ref)
    kr_ref[...] = krb
    cmp_ref[...] = seg(C_KVN, C_KVN + 256)
    sel_ref[...] = seg(C_KVN + 256, C_KVN + 512)
    win_o_ref[...] = seg(C_KVN + 512, C_END)


def _mixer_project_prompt(x, b, s, g, w_in_p, w_kv_t, q_norm, w_uq_p, kv_norm, w_uk_t, w_uv_t, rc, rs1, rs2):
    t, d = x.shape
    tm = TILE
    nt = s // tm
    full = lambda a: pl.BlockSpec(a.shape, lambda i: (0,) * a.ndim)
    row = lambda w: pl.BlockSpec((tm, w), lambda i: (i, 0))
    tab = pl.BlockSpec((tm, LANES), lambda i: (i % nt, 0))
    fm = lambda r: pl.BlockSpec((1, r, tm), lambda i: (i // nt, 0, i % nt))
    tiles = lambda h: pl.BlockSpec((1, h, 1, LANES, tm), lambda i: (i // nt, 0, i % nt, 0, 0))
    tok = lambda w, dt: jax.ShapeDtypeStruct((t, w), dt)
    fms = lambda r: jax.ShapeDtypeStruct((b, r, s), F32)
    til = lambda h: jax.ShapeDtypeStruct((b, h, nt, LANES, tm), BF16)
    outs = [(row(1024), tok(1024, BF16)), (tiles(MLA_HEADS), til(MLA_HEADS)), (tiles(MLA_HEADS), til(MLA_HEADS)),
            (row(256), tok(256, F32)), (fm(MLA_ROPE), fms(MLA_ROPE)), (row(1024), tok(1024, BF16)),
            (fm(256), fms(256)), (fm(256), fms(256)), (fm(256), fms(256)),
            (tiles(NSA_G), til(NSA_G)), (tiles(NSA_G), til(NSA_G)), (tiles(NSA_G), til(NSA_G)), (tiles(NSA_G), til(NSA_G)),
            (row(256), tok(256, F32)), (row(2048), tok(2048, F32))]
    ins = (x, g, w_in_p, w_kv_t, q_norm, w_uq_p, kv_norm, w_uk_t, w_uv_t)
    return pl.pallas_call(
        _proj_prompt_kernel,
        grid=(t // tm,),
        in_specs=[row(d)] + [full(a) for a in ins[1:]] + [tab, tab, tab],
        out_specs=[o[0] for o in outs],
        out_shape=[o[1] for o in outs],
        compiler_params=_cparams(1),
    )(*ins, rc, rs1, rs2)


def _mixer_project_sample(x, g, w_in_p, q_norm, w_uq_p, kv_norm, rc, rs1, rs2):
    t, d = x.shape
    tm = min(t, TILE)
    full = lambda a: pl.BlockSpec(a.shape, lambda i: (0,) * a.ndim)
    row = lambda w: pl.BlockSpec((tm, w), lambda i: (i, 0))
    widths = [(1024, BF16), (256, F32), (128, F32), (1024, BF16), (256, F32), (256, F32), (256, F32), (256, F32),
              (2048, F32)]
    ins = (x, g, w_in_p, q_norm, w_uq_p, kv_norm)
    return pl.pallas_call(
        _proj_sample_kernel,
        grid=(t // tm,),
        in_specs=[row(d)] + [full(a) for a in ins[1:]] + [row(LANES)] * 3,
        out_specs=[row(w) for w, _ in widths],
        out_shape=[jax.ShapeDtypeStruct((t, w), dt) for w, dt in widths],
        compiler_params=_cparams(1),
    )(*ins, rc, rs1, rs2)


def _norm_mm_kernel(x_ref, g_ref, w_ref, o_ref, *, scale):
    y = _dot(_rms(x_ref[...], g_ref[...]).astype(BF16), w_ref[...])
    if scale != 1.0:
        y = y * scale
    o_ref[...] = y.astype(o_ref.dtype)


def _norm_matmul(x, g, w, *, scale=1.0, out_dtype=F32):
    t, d = x.shape
    n = w.shape[1]
    tm = min(t, 512)
    return pl.pallas_call(
        functools.partial(_norm_mm_kernel, scale=scale),
        grid=(t // tm,),
        in_specs=[pl.BlockSpec((tm, d), lambda i: (i, 0)), pl.BlockSpec((1, d), lambda i: (0, 0)),
                  pl.BlockSpec((d, n), lambda i: (0, 0))],
        out_specs=pl.BlockSpec((tm, n), lambda i: (i, 0)),
        out_shape=jax.ShapeDtypeStruct((t, n), out_dtype),
        compiler_params=_cparams(1),
    )(x, g, w)


def _mm_kernel(a_ref, w_ref, o_ref):
    o_ref[...] = _dot(a_ref[...], w_ref[...]).astype(o_ref.dtype)


def _matmul(a, w, *, out_dtype=F32):
    t, k = a.shape
    n = w.shape[1]
    tm = min(t, 512)
    return pl.pallas_call(
        _mm_kernel,
        grid=(t // tm,),
        in_specs=[pl.BlockSpec((tm, k), lambda i: (i, 0)), pl.BlockSpec((k, n), lambda i: (0, 0))],
        out_specs=pl.BlockSpec((tm, n), lambda i: (i, 0)),
        out_shape=jax.ShapeDtypeStruct((t, n), out_dtype),
        compiler_params=_cparams(1),
    )(a, w)


def _mm_res_kernel(a_ref, w_ref, x_ref, o_ref):
    o_ref[...] = x_ref[...] + _dot(a_ref[...], w_ref[...])


def _matmul_residual(a, w, x):
    t, k = a.shape
    n = w.shape[1]
    tm = min(t, 512)
    return pl.pallas_call(
        _mm_res_kernel,
        grid=(t // tm,),
        in_specs=[pl.BlockSpec((tm, k), lambda i: (i, 0)), pl.BlockSpec((k, n), lambda i: (0, 0)),
                  pl.BlockSpec((tm, n), lambda i: (i, 0))],
        out_specs=pl.BlockSpec((tm, n), lambda i: (i, 0)),
        out_shape=jax.ShapeDtypeStruct((t, n), F32),
        compiler_params=_cparams(1),
    )(a, w, x)


def _merge_kernel(oa_ref, ob_ref, mg_ref, x_ref, wa_ref, wb_ref, wo_ref, o_ref):
    d = x_ref.shape[1]
    mix = mg_ref[:, :d] * _dot(oa_ref[...], wa_ref[...]) + mg_ref[:, d:] * _dot(ob_ref[...], wb_ref[...])
    o_ref[...] = x_ref[...] + _dot(mix.astype(BF16), wo_ref[...])


def _merge(o_a, o_b, mg, x, w_a, w_b, w_o):
    t, d = x.shape
    tm = min(t, 512)
    row = lambda w: pl.BlockSpec((tm, w), lambda i: (i, 0))
    full = lambda a: pl.BlockSpec(a.shape, lambda i: (0, 0))
    return pl.pallas_call(
        _merge_kernel,
        grid=(t // tm,),
        in_specs=[row(o_a.shape[1]), row(o_b.shape[1]), row(2 * d), row(d), full(w_a), full(w_b), full(w_o)],
        out_specs=row(d),
        out_shape=jax.ShapeDtypeStruct((t, d), F32),
        compiler_params=_cparams(1),
    )(o_a, o_b, mg, x, w_a, w_b, w_o)


MLA_HPS = 4


def _mla_prompt_kernel(q_ref, k_ref, v_ref, o_ref, m_ref, acc_ref, *, tq, tk):
    i = pl.program_id(2)
    q = q_ref[0]
    qs = [q[:, h * LANES:(h + 1) * LANES] for h in range(MLA_HPS)]
    per_q = tq // tk
    m_ref[...] = jnp.full(m_ref.shape, NEG_INF, F32)
    acc_ref[...] = jnp.zeros(acc_ref.shape, F32)

    def tile(j, causal):
        if causal:
            ok = (j * tk + _iota((tq, tk), 1)) <= (i * tq + _iota((tq, tk), 0))
        for h in range(MLA_HPS):
            s = _dot(qs[h], k_ref[0, h, j])
            if causal:
                s = jnp.where(ok, s, -MASK_BIG)
            _flash_step(s, v_ref[0, h, j], m_ref, acc_ref, h, jnp.exp2)

    def body(j, c):
        tile(j, False)
        return c

    lax.fori_loop(0, per_q * i, body, 0)
    for d in range(per_q):
        tile(per_q * i + d, True)
    lane = _iota((1, LANES), 1)
    outs = []
    for h in range(MLA_HPS):
        a = acc_ref[h]
        outs.append(a / pltpu.roll(a, MLA_V, 1))
    for p in range(MLA_HPS // 2):
        pair = jnp.where(lane < MLA_V, outs[2 * p], pltpu.roll(outs[2 * p + 1], MLA_V, 1))
        o_ref[0, :, p * LANES:(p + 1) * LANES] = pair.astype(o_ref.dtype)


def _mla_prompt(q, k_t, v_t):
    b, s, _ = q.shape
    tq = QTILE
    nt = s // TILE
    kv_spec = pl.BlockSpec((1, MLA_HPS, nt, LANES, TILE), lambda bi, hq, i: (bi, hq, 0, 0, 0))
    return pl.pallas_call(
        functools.partial(_mla_prompt_kernel, tq=tq, tk=TILE),
        grid=(b, MLA_HEADS // MLA_HPS, s // tq),
        in_specs=[pl.BlockSpec((1, tq, MLA_HPS * LANES), lambda bi, hq, i: (bi, i, hq)), kv_spec, kv_spec],
        out_specs=pl.BlockSpec((1, tq, MLA_HPS * MLA_V), lambda bi, hq, i: (bi, i, hq)),
        out_shape=jax.ShapeDtypeStruct((b, s, MLA_HEADS * MLA_V), BF16),
        scratch_shapes=[pltpu.VMEM((MLA_HPS, tq, LANES), F32), pltpu.VMEM((MLA_HPS, tq, LANES), F32)],
        compiler_params=_cparams(3),
    )(q, k_t, v_t)


def _split_hi_lo(x):
    hi = x.astype(BF16)
    return hi, (x - hi.astype(F32)).astype(BF16)


def _compress_prompt_kernel(x_ref, a_ref, phi_ref, o_ref):
    hi, lo = _split_hi_lo(x_ref[0])
    means = _dot(jnp.concatenate([hi, lo], axis=1), a_ref[...])
    o_ref[0] = _dot(phi_ref[...], means.astype(BF16)).astype(o_ref.dtype)


def _compress_prompt(cmp_t, avg2, phi_t):
    b, w, s = cmp_t.shape
    nc = avg2.shape[1]
    return pl.pallas_call(
        _compress_prompt_kernel,
        grid=(b,),
        in_specs=[pl.BlockSpec((1, w, s), lambda i: (i, 0, 0)), pl.BlockSpec(avg2.shape, lambda i: (0, 0)),
                  pl.BlockSpec(phi_t.shape, lambda i: (0, 0))],
        out_specs=pl.BlockSpec((1, w, nc), lambda i: (i, 0, 0)),
        out_shape=jax.ShapeDtypeStruct((b, w, nc), BF16),
        compiler_params=_cparams(1),
    )(cmp_t, avg2, phi_t)


def _nsa_prompt_kernel(q_ref, c_ref, ctab_ref, sk_ref, sv_ref, wk_ref, wv_ref, ktab_ref, gt_ref, o_ref,
                       m_ref, acc_ref, *, tq, tk, n_sel):
    g = pl.program_id(1)
    i = pl.program_id(2)
    r = NSA_HPG * tq
    shift = tq.bit_length() - 1
    row = _iota((r, LANES), 0)
    lane_r = _iota((r, LANES), 1)
    qpos = i * tq + (row & (tq - 1))
    slope = _pow2_neg(NSA_HPG * g + (row >> shift) + 1)
    aq = _query_aug(slope, qpos, lane_r)
    q_st = jnp.concatenate([q_ref[0, :, h * LANES:(h + 1) * LANES] for h in range(NSA_HPG)], axis=0)
    q_plain = jnp.concatenate([q_st, aq.astype(BF16)], axis=1)

    kc = c_ref[0]
    nc = kc.shape[1]
    half = nc // 2
    lane_c = _iota((1, nc), 1)
    cblk = jnp.where(lane_c < half, 2 * lane_c, 2 * (lane_c - half) + 1)
    cmask = (cblk * CMP_BLOCK + (CMP_BLOCK - 1)) <= (i * tq + (_iota((r, nc), 0) & (tq - 1)))
    s = _dot(q_plain, jnp.concatenate([kc, ctab_ref[...]], axis=0))
    s = jnp.where(cmask, s, NEG_INF)
    mx = jnp.max(s, axis=-1, keepdims=True)
    mx = jnp.where(mx == NEG_INF, 0.0, mx)
    e = jnp.where(cmask, jnp.exp(s - mx), 0.0)
    p = e / jnp.maximum(jnp.sum(e, axis=-1, keepdims=True), 1e-30)
    o_cmp = _dot_nt(p.astype(BF16), kc)

    psum = p[0:tq] + p[tq:2 * tq] + p[2 * tq:3 * tq] + p[3 * tq:4 * tq]
    imp = psum + pltpu.roll(psum, half, 1)
    qp = i * tq + _iota((tq, 1), 0)
    forced = (lane_c == 0) | (lane_c == (qp >> 6))
    score = jnp.where(lane_c * SEL_BLOCK <= qp, imp + jnp.where(forced, FORCE_BONUS, 0.0), -1.0)
    st = jnp.transpose(score)[0:n_sel]
    jidx = _iota((n_sel, 1), 0)
    rank = jnp.zeros((n_sel, tq), F32)
    for ii in range(n_sel):
        ri = st[ii:ii + 1, :]
        first = jnp.where(jidx > ii, 1.0, 0.0)
        rank = rank + jnp.where(ri > st, 1.0, jnp.where(ri == st, first, 0.0))
    valid_t = (jidx * SEL_BLOCK) <= (i * tq + _iota((1, tq), 1))
    mt = jnp.where(valid_t, jnp.where(rank < float(min(SEL_TOPK, n_sel)), 1.0, 0.0), 0.0)
    if n_sel < LANES:
        mt = jnp.concatenate([mt, jnp.zeros((LANES - n_sel, tq), F32)], axis=0)
    msel = jnp.transpose(mt)
    mst = jnp.concatenate([msel] * NSA_HPG, axis=0)
    q_sel = jnp.concatenate([q_st, jnp.where(lane_r < n_sel, (mst - 1.0) * MASK_BIG, aq).astype(BF16)], axis=1)

    lane = _iota((1, LANES), 1)
    m_ref[...] = jnp.full(m_ref.shape, NEG_INF, F32)
    acc_ref[...] = jnp.zeros(acc_ref.shape, F32)
    per_q = tq // tk
    rel = (_iota((r, tk), 0) & (tq - 1)) - _iota((r, tk), 1)

    def dist_to(j):
        return rel + (i * tq - j * tk)

    def sel_tile(j, causal):
        sc = _dot(q_sel, jnp.concatenate([sk_ref[0, 0, j], ktab_ref[j]], axis=0))
        if causal:
            sc = jnp.where(dist_to(j) >= 0, sc, -MASK_BIG)
        _flash_step(sc, sv_ref[0, 0, j], m_ref, acc_ref, 0, jnp.exp)

    def sel_body(j, c):
        sel_tile(j, False)
        return c

    lax.fori_loop(0, per_q * i, sel_body, 0)
    for dd in range(per_q):
        sel_tile(per_q * i + dd, True)

    def win_body(j, c):
        in_window = lax.bitcast_convert_type(dist_to(j), jnp.uint32) <= jnp.uint32(WINDOW)
        sc = _dot(q_plain, jnp.concatenate([wk_ref[0, 0, j], ktab_ref[j]], axis=0))
        _flash_step(jnp.where(in_window, sc, -MASK_BIG), wv_ref[0, 0, j], m_ref, acc_ref, 1, jnp.exp)
        return c

    lax.fori_loop(jnp.maximum(per_q * i - WINDOW // tk, 0), per_q * (i + 1), win_body, 0)

    gt = gt_ref[0]
    heads = []
    for h in range(NSA_HPG):
        rs = slice(h * tq, (h + 1) * tq)
        a_sel, a_win = acc_ref[0, rs], acc_ref[1, rs]
        heads.append(gt[:, 3 * h:3 * h + 1] * o_cmp[rs]
                     + (gt[:, 3 * h + 1:3 * h + 2] / a_sel[:, 0:1]) * a_sel
                     + (gt[:, 3 * h + 2:3 * h + 3] / a_win[:, 0:1]) * a_win)
    for p2 in range(NSA_HPG // 2):
        pair = jnp.where(lane < NSA_DH, pltpu.roll(heads[2 * p2], NSA_DH, 1), heads[2 * p2 + 1])
        o_ref[0, :, p2 * LANES:(p2 + 1) * LANES] = pair.astype(o_ref.dtype)


def _nsa_prompt(q, ckcv_t, ctab, sk, sv, wk, wv, ktab, gates):
    b, s, _ = q.shape
    tq = QTILE
    nt = s // TILE
    nc = ckcv_t.shape[2]
    r = NSA_HPG * tq
    kv = pl.BlockSpec((1, 1, nt, LANES, TILE), lambda bi, gi, i: (bi, gi, 0, 0, 0))
    return pl.pallas_call(
        functools.partial(_nsa_prompt_kernel, tq=tq, tk=TILE, n_sel=s // SEL_BLOCK),
        grid=(b, NSA_G, s // tq),
        in_specs=[pl.BlockSpec((1, tq, NSA_HPG * LANES), lambda bi, gi, i: (bi, i, gi)),
                  pl.BlockSpec((1, LANES, nc), lambda bi, gi, i: (bi, gi, 0)),
                  pl.BlockSpec(ctab.shape, lambda bi, gi, i: (0, 0)),
                  kv, kv, kv, kv,
                  pl.BlockSpec(ktab.shape, lambda bi, gi, i: (0, 0, 0)),
                  pl.BlockSpec((1, tq, LANES), lambda bi, gi, i: (bi, i, gi))],
        out_specs=pl.BlockSpec((1, tq, NSA_HPG * NSA_DH), lambda bi, gi, i: (bi, i, gi)),
        out_shape=jax.ShapeDtypeStruct((b, s, NSA_HEADS * NSA_DH), BF16),
        scratch_shapes=[pltpu.VMEM((2, r, LANES), F32), pltpu.VMEM((2, r, LANES), F32)],
        compiler_params=_cparams(3),
    )(q, ckcv_t, ctab, sk, sv, wk, wv, ktab, gates)


def _xattn_prompt_kernel(q_ref, kv_ref, o_ref):
    hw = X_HEADS * X_DH
    for h in range(X_HEADS):
        cs = slice(h * X_DH, (h + 1) * X_DH)
        s = _dot_nt(q_ref[0, :, cs], kv_ref[0, :, cs])
        e = jnp.exp(s - jnp.max(s, axis=-1, keepdims=True))
        p = e / jnp.sum(e, axis=-1, keepdims=True)
        o_ref[0, :, cs] = _dot(p.astype(BF16), kv_ref[0, :, hw + h * X_DH:hw + (h + 1) * X_DH]).astype(o_ref.dtype)


def _xattn_prompt(q, kv):
    b, s, w = q.shape
    m = kv.shape[1]
    tq = min(s, 512)
    return pl.pallas_call(
        _xattn_prompt_kernel,
        grid=(b, s // tq),
        in_specs=[pl.BlockSpec((1, tq, w), lambda bi, i: (bi, i, 0)),
                  pl.BlockSpec((1, m, 2 * w), lambda bi, i: (bi, 0, 0))],
        out_specs=pl.BlockSpec((1, tq, w), lambda bi, i: (bi, i, 0)),
        out_shape=jax.ShapeDtypeStruct((b, s, w), BF16),
        compiler_params=_cparams(2),
    )(q, kv)


def _qabs_kernel(q_ref, w_ref, o_ref):
    o_ref[0] = _dot(q_ref[...], w_ref[0]).astype(o_ref.dtype)


def _mla_absorb_q(q_mla, w_abs):
    t = q_mla.shape[0]
    n = w_abs.shape[2]
    return pl.pallas_call(
        _qabs_kernel,
        grid=(MLA_HEADS,),
        in_specs=[pl.BlockSpec((t, LANES), lambda h: (0, h)), pl.BlockSpec((1, LANES, n), lambda h: (h, 0, 0))],
        out_specs=pl.BlockSpec((1, t, n), lambda h: (h, 0, 0)),
        out_shape=jax.ShapeDtypeStruct((MLA_HEADS, t, n), BF16),
        compiler_params=_cparams(1),
    )(q_mla, w_abs)


def _double_buffered(step, n_steps, copies, per_page=1):
    def start_all(cps):
        for k, cp in enumerate(cps):
            cp.start(priority=(k // per_page) % 2)

    @pl.when(step == 0)
    def _():
        start_all(copies(step, 0))

    @pl.when(step + 1 < n_steps)
    def _():
        start_all(copies(step + 1, (step + 1) & 1))

    slot = step & 1
    for cp in copies(step, slot):
        cp.wait()
    return slot


def _pages_on_lanes(buf, slot, first, n):
    return jnp.concatenate([buf[slot, first + p] for p in range(n)], axis=1)


def _mla_sample_kernel(pt_ref, qa_ref, cn_ref, krn_ref, c_hbm, kr_hbm, o_ref, cbuf, krbuf, csem, krsem,
                       *, n_pages, db, nq):
    def copies(bb, slot):
        out = []
        for p in range(n_pages):
            page = pt_ref[p * db + bb]
            out.append(pltpu.make_async_copy(c_hbm.at[page], cbuf.at[slot, pl.ds(p * PAGE, PAGE), :], csem.at[slot]))
            out.append(pltpu.make_async_copy(kr_hbm.at[page], krbuf.at[slot, p], krsem.at[slot]))
        return out

    slot = _double_buffered(pl.program_id(0), pl.num_programs(0), copies, per_page=2)
    qa = qa_ref[0]
    qc, qr = qa[:, :MLA_KV_LORA], qa[:, MLA_KV_LORA:MLA_KV_LORA + MLA_ROPE]
    c = cbuf[slot].astype(BF16)
    kr_t = _pages_on_lanes(krbuf, slot, 0, n_pages).astype(BF16)
    s1 = _dot_nt(qc, c) + _dot(qr, kr_t)
    cn = cn_ref[0].astype(BF16)
    s2 = _dot_nt(qc, cn) + _dot_nt(qr, krn_ref[0].astype(BF16))
    qi = _iota((qa.shape[0], 1), 0) & (nq - 1)
    s2 = jnp.where(_iota((1, cn.shape[0]), 1) <= qi, s2, NEG_INF)
    mx = jnp.maximum(jnp.max(s1, axis=-1, keepdims=True), jnp.max(s2, axis=-1, keepdims=True))
    e1 = jnp.exp2(s1 - mx)
    e2 = jnp.exp2(s2 - mx)
    den = jnp.sum(e1, axis=-1, keepdims=True) + jnp.sum(e2, axis=-1, keepdims=True)
    o_ref[0] = (_dot(e1.astype(BF16), c) + _dot(e2.astype(BF16), cn)) / den


def _mla_sample(pt_flat, db, n_pages, qa, c_new, kr_new, ckv_pool, kr_pool_t, *, nq):
    r = qa.shape[1]
    past = n_pages * PAGE
    per_b = lambda a: pl.BlockSpec((1,) + a.shape[1:], lambda b, pt: (b, 0, 0))
    hbm = pl.BlockSpec(memory_space=pl.ANY)
    grid_spec = pltpu.PrefetchScalarGridSpec(
        num_scalar_prefetch=1,
        grid=(db,),
        in_specs=[per_b(qa), per_b(c_new), per_b(kr_new), hbm, hbm],
        out_specs=pl.BlockSpec((1, r, MLA_KV_LORA), lambda b, pt: (b, 0, 0)),
        scratch_shapes=[pltpu.VMEM((2, past, MLA_KV_LORA), F32), pltpu.VMEM((2, n_pages, MLA_ROPE, PAGE), F32),
                        pltpu.SemaphoreType.DMA((2,)), pltpu.SemaphoreType.DMA((2,))],
    )
    return pl.pallas_call(
        functools.partial(_mla_sample_kernel, n_pages=n_pages, db=db, nq=nq),
        grid_spec=grid_spec,
        out_shape=jax.ShapeDtypeStruct((db, r, MLA_KV_LORA), F32),
        compiler_params=_cparams(1),
    )(pt_flat, qa, c_new, kr_new, ckv_pool, kr_pool_t)


def _cmp_sample_kernel(pt_ref, a_ref, phi_ref, pool_hbm, o_ref, buf, sem, *, n_pages, db):
    def copies(bb, slot):
        return [pltpu.make_async_copy(pool_hbm.at[pt_ref[p * db + bb]], buf.at[slot, p], sem.at[slot])
                for p in range(n_pages)]

    slot = _double_buffered(pl.program_id(0), pl.num_programs(0), copies)
    pages = a_ref.shape[0] // (2 * PAGE)
    blocks = a_ref.shape[1]
    for c in range(n_pages // pages):
        hi, lo = _split_hi_lo(_pages_on_lanes(buf, slot, c * pages, pages))
        lhs = jnp.concatenate([hi, lo], axis=1)
        half = lhs.shape[0] // 2
        means = jnp.concatenate([_dot(lhs[:half], a_ref[...]), _dot(lhs[half:], a_ref[...])], axis=0)
        means = jnp.transpose(means).astype(BF16)
        o_ref[0, c * blocks:(c + 1) * blocks, :] = _dot(means, phi_ref[...]).astype(o_ref.dtype)


def _compress_sample(pt_flat, db, n_pages, cmp_pool_t, avg2, phi):
    w = cmp_pool_t.shape[1]
    past = n_pages * PAGE
    grid_spec = pltpu.PrefetchScalarGridSpec(
        num_scalar_prefetch=1,
        grid=(db,),
        in_specs=[pl.BlockSpec(avg2.shape, lambda b, pt: (0, 0)), pl.BlockSpec(phi.shape, lambda b, pt: (0, 0)),
                  pl.BlockSpec(memory_space=pl.ANY)],
        out_specs=pl.BlockSpec((1, past // CMP_BLOCK, w), lambda b, pt: (b, 0, 0)),
        scratch_shapes=[pltpu.VMEM((2, n_pages, w, PAGE), F32), pltpu.SemaphoreType.DMA((2,))],
    )
    return pl.pallas_call(
        functools.partial(_cmp_sample_kernel, n_pages=n_pages, db=db),
        grid_spec=grid_spec,
        out_shape=jax.ShapeDtypeStruct((db, past // CMP_BLOCK, w), BF16),
        compiler_params=_cparams(1),
    )(pt_flat, avg2, phi, cmp_pool_t)


def _nsa_cmp_sample_kernel(q_ref, c_ref, ctab_ref, o_ref, imp_ref, *, nq, past):
    r = q_ref.shape[2]
    nc = c_ref.shape[1]
    row = _iota((r, LANES), 0)
    lane_r = _iota((r, LANES), 1)
    qpos = past + (row & (nq - 1))
    cmask = (_iota((1, nc), 1) * CMP_BLOCK + (CMP_BLOCK - 1)) <= (past + (_iota((r, nc), 0) & (nq - 1)))
    for bi, g in _batch_groups(q_ref.shape[0]):
        slope = _pow2_neg(NSA_HPG * g + (row >> (nq.bit_length() - 1)) + 1)
        q2 = jnp.concatenate([q_ref[bi, g], _query_aug(slope, qpos, lane_r).astype(BF16)], axis=1)
        kc = c_ref[bi, :, g * LANES:(g + 1) * LANES]
        s = _dot_nt(q2, jnp.concatenate([kc, ctab_ref[...]], axis=1))
        s = jnp.where(cmask, s, NEG_INF)
        mx = jnp.max(s, axis=-1, keepdims=True)
        mx = jnp.where(mx == NEG_INF, 0.0, mx)
        e = jnp.where(cmask, jnp.exp(s - mx), 0.0)
        p = e / jnp.maximum(jnp.sum(e, axis=-1, keepdims=True), 1e-30)
        o_ref[bi, g] = _dot(p.astype(BF16), kc)
        ps = p
        for h in range(1, NSA_HPG):
            ps = ps + pltpu.roll(p, h * nq, 0)
        imp_ref[bi, g] = ps + pltpu.roll(ps, nc - 1, 1)


def _batch_groups(bc):
    return [(bi, g) for bi in range(bc) for g in range(NSA_G)]


def _decode_batch(db):
    return 4 if db % 4 == 0 else 1


def _nsa_cmp_sample(q, ckcv, ctab, *, nq, past):
    db, g, r, _ = q.shape
    nc = ckcv.shape[1]
    bc = _decode_batch(db)
    return pl.pallas_call(
        functools.partial(_nsa_cmp_sample_kernel, nq=nq, past=past),
        grid=(db // bc,),
        in_specs=[pl.BlockSpec((bc,) + q.shape[1:], lambda b: (b, 0, 0, 0)),
                  pl.BlockSpec((bc,) + ckcv.shape[1:], lambda b: (b, 0, 0)),
                  pl.BlockSpec(ctab.shape, lambda b: (0, 0))],
        out_specs=[pl.BlockSpec((bc, g, r, LANES), lambda b: (b, 0, 0, 0)),
                   pl.BlockSpec((bc, g, r, nc), lambda b: (b, 0, 0, 0))],
        out_shape=[jax.ShapeDtypeStruct((db, g, r, LANES), F32), jax.ShapeDtypeStruct((db, g, r, nc), F32)],
        compiler_params=_cparams(1),
    )(q, ckcv, ctab)


def _topk_sample_kernel(imp_ref, o_ref, *, n_pick, last_blk):
    s = imp_ref[...]
    lane_i = _iota(s.shape, 1)
    lane = lane_i.astype(F32)
    s = jnp.where(((lane_i & 1) == 0) & (lane_i > 0), s, -1.0)
    olane = _iota(o_ref.shape, 1)
    out = jnp.where(olane == n_pick + 1, float(last_blk), 0.0)
    for it in range(n_pick):
        mx = jnp.max(s, axis=-1, keepdims=True)
        idx = jnp.min(jnp.where(s == mx, lane, float(s.shape[1])), axis=-1, keepdims=True)
        s = jnp.where(lane == idx, -1.0, s)
        out = jnp.where(olane == it, idx * 0.5, out)
    o_ref[...] = out.astype(I32)


def _topk_sample(imp, *, n_pick, last_blk):
    rows = imp.shape[0]
    return pl.pallas_call(
        functools.partial(_topk_sample_kernel, n_pick=n_pick, last_blk=last_blk),
        grid=(1,),
        in_specs=[pl.BlockSpec(imp.shape, lambda i: (0, 0))],
        out_specs=pl.BlockSpec((rows, LANES), lambda i: (0, 0)),
        out_shape=jax.ShapeDtypeStruct((rows, LANES), I32),
        compiler_params=_cparams(1),
    )(imp)


def _nsa_sel_sample_kernel(pt_ref, ids_ref, q_ref, new_ref, pool_hbm, o_ref, buf, sem, *, nblk, nq, db, past):
    def block_id(t, qq, c):
        return ids_ref[(t * nq + qq) * SEL_TOPK + c]

    def copies(t, slot):
        b = t >> 1
        rows = pl.ds(pl.multiple_of((t & 1) * LANES, LANES), LANES)
        out = []
        for qq in range(nq):
            for c in range(nblk):
                page = pt_ref[(block_id(t, qq, c) >> 1) * db + b]
                out.append(pltpu.make_async_copy(pool_hbm.at[page, rows, :], buf.at[slot, qq * nblk + c], sem.at[slot]))
        return out

    t = pl.program_id(0)
    slot = _double_buffered(t, pl.num_programs(0), copies)
    g = t & 1
    q = q_ref[0]
    r = q.shape[0]
    kv_t = _pages_on_lanes(buf, slot, 0, nq * nblk).astype(BF16)
    lane = _iota((1, PAGE), 1)
    kpos, keep, owner = [], [], []
    for qq in range(nq):
        for c in range(nblk):
            blk = block_id(t, qq, c)
            kpos.append((blk >> 1) * PAGE + lane)
            keep.append((lane >> 6) == (blk & 1))
        owner.append(jnp.full((1, nblk * PAGE), qq, I32))
    kpos = jnp.concatenate(kpos, axis=1)
    keep = jnp.concatenate(keep, axis=1)
    owner = jnp.concatenate(owner, axis=1)
    row = _iota((r, 1), 0)
    qi = row & (nq - 1)
    slope = _pow2_neg(NSA_HPG * g + (row >> (nq.bit_length() - 1)) + 1)
    s1 = _dot(q, kv_t) - slope * (past + qi - kpos).astype(F32)
    s1 = jnp.where(owner == qi, jnp.where(keep, s1, NEG_INF), NEG_INF)
    nw = new_ref[0].astype(BF16)
    tnew = _iota((1, nw.shape[0]), 1)
    s2 = _dot_nt(q, nw) - slope * (qi - tnew).astype(F32)
    s2 = jnp.where(tnew <= qi, s2, NEG_INF)
    mx = jnp.maximum(jnp.max(s1, axis=-1, keepdims=True), jnp.max(s2, axis=-1, keepdims=True))
    e1 = jnp.exp(s1 - mx)
    e2 = jnp.exp(s2 - mx)
    den = jnp.sum(e1, axis=-1, keepdims=True) + jnp.sum(e2, axis=-1, keepdims=True)
    o_ref[0] = (_dot_nt(e1.astype(BF16), kv_t) + _dot(e2.astype(BF16), nw)) / den


def _nsa_sel_sample(pt_flat, ids, db, q, sel_new, sel_pool_t, *, nblk, nq, past):
    steps, r, _ = q.shape
    grid_spec = pltpu.PrefetchScalarGridSpec(
        num_scalar_prefetch=2,
        grid=(steps,),
        in_specs=[pl.BlockSpec((1, r, LANES), lambda t, pt, idr: (t, 0, 0)),
                  pl.BlockSpec((1, sel_new.shape[1], LANES), lambda t, pt, idr: (t // NSA_G, 0, t % NSA_G)),
                  pl.BlockSpec(memory_space=pl.ANY)],
        out_specs=pl.BlockSpec((1, r, LANES), lambda t, pt, idr: (t, 0, 0)),
        scratch_shapes=[pltpu.VMEM((2, nq * nblk, LANES, PAGE), F32), pltpu.SemaphoreType.DMA((2,))],
    )
    return pl.pallas_call(
        functools.partial(_nsa_sel_sample_kernel, nblk=nblk, nq=nq, db=db, past=past),
        grid_spec=grid_spec,
        out_shape=jax.ShapeDtypeStruct((steps, r, LANES), F32),
        compiler_params=_cparams(1),
    )(pt_flat, ids.reshape(-1), q, sel_new, sel_pool_t)


def _nsa_win_sample_kernel(q_ref, st_ref, wtab_ref, new_ref, o_ref, *, nq, past):
    r = q_ref.shape[2]
    wbuf = st_ref.shape[2]
    row = _iota((r, LANES), 0)
    lane_r = _iota((r, LANES), 1)
    qpos = past + (row & (nq - 1))
    qi = _iota((r, 1), 0) & (nq - 1)
    dist_st = qi + wbuf - _iota((1, wbuf), 1)
    tnew = _iota((1, new_ref.shape[1]), 1)
    for bi, g in _batch_groups(q_ref.shape[0]):
        slope = _pow2_neg(NSA_HPG * g + (row >> (nq.bit_length() - 1)) + 1)
        q = q_ref[bi, g]
        q2 = jnp.concatenate([q, _query_aug(slope, qpos, lane_r).astype(BF16)], axis=1)
        st = st_ref[bi, g * LANES:(g + 1) * LANES, :].astype(BF16)
        nw = new_ref[bi, :, g * LANES:(g + 1) * LANES].astype(BF16)
        s1 = _dot(q2, jnp.concatenate([st, wtab_ref[...]], axis=0))
        s1 = jnp.where(dist_st <= WINDOW, s1, NEG_INF)
        s2 = _dot_nt(q, nw) - slope[:, :1] * (qi - tnew).astype(F32)
        s2 = jnp.where(tnew <= qi, s2, NEG_INF)
        mx = jnp.maximum(jnp.max(s1, axis=-1, keepdims=True), jnp.max(s2, axis=-1, keepdims=True))
        e1 = jnp.exp(s1 - mx)
        e2 = jnp.exp(s2 - mx)
        den = jnp.sum(e1, axis=-1, keepdims=True) + jnp.sum(e2, axis=-1, keepdims=True)
        o_ref[bi, g] = (_dot_nt(e1.astype(BF16), st) + _dot(e2.astype(BF16), nw)) / den


def _nsa_win_sample(q, state_t, wtab, win_new, *, nq, past):
    db, g, r, _ = q.shape
    bc = _decode_batch(db)
    return pl.pallas_call(
        functools.partial(_nsa_win_sample_kernel, nq=nq, past=past),
        grid=(db // bc,),
        in_specs=[pl.BlockSpec((bc,) + q.shape[1:], lambda b: (b, 0, 0, 0)),
                  pl.BlockSpec((bc,) + state_t.shape[1:], lambda b: (b, 0, 0)),
                  pl.BlockSpec(wtab.shape, lambda b: (0, 0)),
                  pl.BlockSpec((bc,) + win_new.shape[1:], lambda b: (b, 0, 0))],
        out_specs=pl.BlockSpec((bc, g, r, LANES), lambda b: (b, 0, 0, 0)),
        out_shape=jax.ShapeDtypeStruct((db, g, r, LANES), F32),
        compiler_params=_cparams(1),
    )(q, state_t, wtab, win_new)


def _gate_kernel(g_ref, a_ref, b_ref, c_ref, o_ref):
    o_ref[...] = (g_ref[0] * a_ref[...] + g_ref[1] * b_ref[...] + g_ref[2] * c_ref[...]).astype(o_ref.dtype)


def _nsa_gate_sample(gexp, o_cmp, o_sel, o_win):
    t, w = o_cmp.shape
    row = pl.BlockSpec((t, w), lambda i: (0, 0))
    return pl.pallas_call(
        _gate_kernel,
        grid=(1,),
        in_specs=[pl.BlockSpec((3, t, w), lambda i: (0, 0, 0)), row, row, row],
        out_specs=row,
        out_shape=jax.ShapeDtypeStruct((t, w), BF16),
        compiler_params=_cparams(1),
    )(gexp, o_cmp, o_sel, o_win)


def _xattn_sample_kernel(q_ref, kv_ref, o_ref, *, bc, n_mem):
    r = q_ref.shape[1]
    rows = 2 * X_HEADS * n_mem
    is_key = (_iota((X_HEADS * r, rows), 1) & (2 * X_HEADS - 1)) == (_iota((X_HEADS * r, rows), 0) >> (r.bit_length() - 1))
    for bi in range(bc):
        x = kv_ref[bi].astype(BF16)
        q = jnp.concatenate([q_ref[bi, :, h * X_DH:(h + 1) * X_DH] for h in range(X_HEADS)], axis=0)
        s = jnp.where(is_key, _dot_nt(q, x), NEG_INF)
        e = jnp.exp(s - jnp.max(s, axis=-1, keepdims=True))
        p = e / jnp.sum(e, axis=-1, keepdims=True)
        o = _dot(pltpu.roll(p, X_HEADS, 1).astype(BF16), x)
        for h in range(X_HEADS):
            o_ref[bi, :, h * X_DH:(h + 1) * X_DH] = o[h * r:(h + 1) * r].astype(o_ref.dtype)


def _xattn_sample(q, kv, *, n_mem):
    db, r, w = q.shape
    bc = _decode_batch(db)
    return pl.pallas_call(
        functools.partial(_xattn_sample_kernel, bc=bc, n_mem=n_mem),
        grid=(db // bc,),
        in_specs=[pl.BlockSpec((bc, r, w), lambda i: (i, 0, 0)),
                  pl.BlockSpec((bc,) + kv.shape[1:], lambda i: (i, 0, 0))],
        out_specs=pl.BlockSpec((bc, r, w), lambda i: (i, 0, 0)),
        out_shape=jax.ShapeDtypeStruct((db, r, w), BF16),
        compiler_params=_cparams(1),
    )(q, kv)


def _prep_weights(w_in, mla_w_uq, mla_w_uk, mla_w_uv, nsa_phi_k, nsa_phi_v):
    d = w_in.shape[0]
    z = lambda n: jnp.zeros((d, n), F32)
    o = 0
    cq, o = w_in[:, o:o + 384], o + 384
    ckv, o = w_in[:, o:o + 256], o + 256
    kr, o = w_in[:, o:o + 32], o + 32
    qn, o = w_in[:, o:o + 512], o + 512
    kvn, o = w_in[:, o:o + 768], o + 768
    gn, o = w_in[:, o:o + 24], o + 24
    mg = w_in[:, o:]
    qn_p = jnp.pad(qn.reshape(d, NSA_HEADS, NSA_DH), ((0, 0), (0, 0), (0, LANES - NSA_DH))).reshape(d, -1)
    w_in_p = jnp.concatenate([cq, ckv, z(64), kr, z(32), qn_p, gn[:, :12], z(116), gn[:, 12:], z(116), mg, kvn],
                             axis=1).astype(BF16)
    w_kv_t = jnp.transpose(kvn).astype(BF16)
    w_uq_p = jnp.pad(mla_w_uq, ((0, 0), (0, 0), (0, LANES - MLA_NOPE - MLA_ROPE))).reshape(MLA_Q_LORA, -1).astype(BF16)
    pad_t = lambda w: jnp.pad(jnp.transpose(w, (1, 2, 0)), ((0, 0), (0, LANES - w.shape[2]), (0, 0))).reshape(
        MLA_HEADS * LANES, MLA_KV_LORA).astype(BF16)
    w_uk_t, w_uv_t = pad_t(mla_w_uk), pad_t(mla_w_uv)
    rope_pass = np.zeros((LANES, 384), np.float32)
    rope_pass[MLA_NOPE:MLA_NOPE + MLA_ROPE, MLA_KV_LORA:MLA_KV_LORA + MLA_ROPE] = np.eye(MLA_ROPE)
    w_abs = jnp.pad(jnp.transpose(mla_w_uk, (1, 2, 0)), ((0, 0), (0, LANES - MLA_NOPE), (0, 384 - MLA_KV_LORA))) + rope_pass
    eye_h = np.eye(MLA_HEADS, dtype=np.float32)[:, None, :, None]
    w_ov = (eye_h * jnp.transpose(mla_w_uv, (1, 0, 2))[:, :, None, :]).reshape(MLA_HEADS * MLA_KV_LORA, MLA_HEADS * MLA_V)
    maps = jnp.stack([jnp.transpose(m) for g in range(NSA_G) for m in (nsa_phi_k[g], nsa_phi_v[g])])
    phi_t = (np.eye(4, dtype=np.float32)[:, None, :, None] * maps[:, :, None, :]).reshape(4 * NSA_DH, 4 * NSA_DH)
    return w_in_p, w_kv_t, w_uq_p, w_uk_t, w_uv_t, w_abs.astype(BF16), w_ov.astype(BF16), phi_t.astype(BF16)


def _rope_tables(pos):
    half = MLA_ROPE // 2
    inv = ROPE_BASE ** (-np.arange(half, dtype=np.float64) / half)
    ang = pos.astype(np.float64)[:, None] * inv[None, :]
    cos, sin = np.cos(ang).astype(np.float32), np.sin(ang).astype(np.float32)
    n = pos.shape[0]
    one, zero = np.ones((n, MLA_NOPE), np.float32), np.zeros((n, half), np.float32)
    tail = np.zeros((n, LANES - MLA_NOPE - MLA_ROPE), np.float32)
    rc = np.concatenate([one, cos, cos, tail], axis=1)
    rs1 = np.concatenate([0 * one, -sin, zero, tail], axis=1)
    rs2 = np.concatenate([0 * one, zero, sin, tail], axis=1)
    return rc, rs1, rs2


def _key_rows(pos):
    rows = np.arange(LANES)[:, None]
    blk = (pos >> 6)[None, :]
    t = np.where((rows == blk) & (rows < SEL_BLOCK), 1.0, 0.0)
    t = np.where(rows == A_BLK, blk, t)
    t = np.where(rows == A_OFF, (pos & (SEL_BLOCK - 1))[None, :], t)
    t = np.where((rows == A_ONE0) | (rows == A_ONE1), 1.0, t)
    return t.astype(BF16)


def _avg_matrix(n_keys, order):
    blk_of_key = np.tile(np.arange(n_keys) // CMP_BLOCK, 2)[:, None]
    return np.where(blk_of_key == order[None, :], 1.0 / CMP_BLOCK, 0.0).astype(BF16)


def _even_odd_order(n):
    return np.concatenate([np.arange(0, n, 2), np.arange(1, n, 2)])


def kernel(x_prompt, x_sample, mem_prompt, cache_mla_ckv, cache_mla_krope, cache_nsa_cmp_kv, cache_nsa_sel_kv,
           state_nsa_win_kv, cache_xattn_kv, page_table, ffn1_norm, ffn1_w_gate, ffn1_w_up, ffn1_w_down, mix_norm,
           w_in, mla_q_norm, mla_w_uq, mla_kv_norm, mla_w_uk, mla_w_uv, nsa_phi_k, nsa_phi_v, w_br_mla, w_br_nsa,
           w_out, xattn_norm, xattn_mem_norm, xattn_w_q, xattn_w_kv, xattn_w_o, ffn2_norm, ffn2_w_gate, ffn2_w_up,
           ffn2_w_down, final_norm):
    assert x_prompt.shape[2] == D_MODEL and ffn1_norm.shape[0] == 1
    b, s, d = x_prompt.shape
    db, nq, _ = x_sample.shape
    n_pages = page_table.shape[1]
    past = n_pages * PAGE
    n_mem = mem_prompt.shape[1]
    assert s % QTILE == 0 and past % SEL_BLOCK == 0 and nq <= 8 and NSA_G == 2
    g2 = lambda a: a.reshape(1, -1)
    bf = lambda a: a.astype(BF16)

    w_in_p, w_kv_t, w_uq_p, w_uk_t, w_uv_t, w_abs, w_ov, phi_t = _prep_weights(
        w_in[0], mla_w_uq[0], mla_w_uk[0], mla_w_uv[0], nsa_phi_k[0], nsa_phi_v[0])
    f1 = (g2(ffn1_norm[0]), bf(ffn1_w_gate[0]), bf(ffn1_w_up[0]), bf(ffn1_w_down[0]))
    f2 = (g2(ffn2_norm[0]), bf(ffn2_w_gate[0]), bf(ffn2_w_up[0]), bf(ffn2_w_down[0]))
    fg = g2(final_norm)
    w_a, w_b, w_o = bf(w_br_mla[0]), bf(w_br_nsa[0]), bf(w_out[0])
    w_xq, w_xo = bf(xattn_w_q[0]), bf(xattn_w_o[0])
    w_xkv = bf(xattn_w_kv[0].reshape(d, -1))
    mixg, qng, kvg = g2(mix_norm[0]), g2(mla_q_norm[0]), g2(mla_kv_norm[0])

    t = b * s
    nt = s // TILE
    hp = _ffn(x_prompt.reshape(t, d), *f1, fg, final=False)
    (q_mla, k_t, v_t, p_ckv, p_kr_t, q_nsa, p_cmp_t, p_sel_t, p_win_t, sk, sv, wk, wv, gates, merge) = (
        _mixer_project_prompt(hp, b, s, mixg, w_in_p[:, :C_KVN], w_kv_t, qng, w_uq_p, kvg, w_uk_t, w_uv_t,
                              *_rope_tables(np.arange(s))))
    o_a = _mla_prompt(q_mla.reshape(b, s, -1), k_t, v_t).reshape(t, -1)

    nc = s // CMP_BLOCK
    order = _even_odd_order(nc)
    ckcv_t = _compress_prompt(p_cmp_t, _avg_matrix(s, order), phi_t)
    ctab = _key_rows(order * CMP_BLOCK + (CMP_BLOCK - 1))
    ktab = np.transpose(_key_rows(np.arange(s)).reshape(LANES, nt, TILE), (1, 0, 2))
    o_b = _nsa_prompt(q_nsa.reshape(b, s, -1), ckcv_t, ctab, sk, sv, wk, wv, ktab, gates.reshape(b, s, -1)).reshape(t, -1)
    hp = _merge(o_a, o_b, merge, hp, w_a, w_b, w_o)

    kv_mem = _norm_matmul(mem_prompt.reshape(b * n_mem, d), g2(xattn_mem_norm[0]), w_xkv)
    xq = _norm_matmul(hp, g2(xattn_norm[0]), w_xq, scale=X_SCALE, out_dtype=BF16)
    xo = _xattn_prompt(xq.reshape(b, s, -1), bf(kv_mem).reshape(b, n_mem, -1))
    hp = _matmul_residual(xo.reshape(t, -1), w_xo, hp)
    y_prompt = _ffn(hp, *f2, fg, final=True).reshape(b, s, d)

    wlen = min(WINDOW, s)
    kv_out = lambda a: jnp.transpose(a.reshape(1, b, NSA_G, 2, NSA_DH, a.shape[-1]), (0, 1, 5, 2, 3, 4))
    prompt_caches = (p_ckv.reshape(1, b, s, -1), jnp.transpose(p_kr_t, (0, 2, 1))[None], kv_out(p_cmp_t),
                     kv_out(p_sel_t), kv_out(p_win_t[:, :, s - wlen:]),
                     kv_mem.reshape(1, b, n_mem, 2, X_HEADS, X_DH))

    ts = db * nq
    hs = _ffn(x_sample.reshape(ts, d), *f1, fg, final=False)
    pos_s = np.tile(past + np.arange(nq), db)
    (q_mla, s_ckv, s_krb, q_nsa, s_cmp, s_sel, s_win, gates, merge) = _mixer_project_sample(
        hs, mixg, w_in_p, qng, w_uq_p, kvg, *_rope_tables(pos_s))
    s_kr = s_krb[:, MLA_NOPE:MLA_NOPE + MLA_ROPE]
    pad_rows = lambda a: jnp.pad(a.reshape(db, nq, -1), ((0, 0), (0, 8 - nq), (0, 0)))
    pt_flat = jnp.transpose(page_table).reshape(-1)

    qa = _mla_absorb_q(q_mla, w_abs)
    qa = jnp.transpose(qa.reshape(MLA_HEADS, db, nq, -1), (1, 0, 2, 3)).reshape(db, MLA_HEADS * nq, -1)
    kr_pool_t = jnp.transpose(cache_mla_krope[0], (0, 2, 1))
    o_lat = _mla_sample(pt_flat, db, n_pages, qa, pad_rows(s_ckv), pad_rows(s_kr), cache_mla_ckv[0], kr_pool_t, nq=nq)
    o_lat = jnp.transpose(o_lat.reshape(db, MLA_HEADS, nq, -1), (0, 2, 1, 3)).reshape(ts, -1)
    o_a = _matmul(bf(o_lat), w_ov, out_dtype=BF16)

    fm_pool = lambda c: jnp.transpose(c[0], (0, 2, 3, 4, 1)).reshape(c.shape[1], -1, PAGE)
    cmp_pool_t, sel_pool_t = fm_pool(cache_nsa_cmp_kv), fm_pool(cache_nsa_sel_kv)
    keys_per_dot = min(past, 32 * PAGE)
    avg2_s = _avg_matrix(keys_per_dot, np.arange(keys_per_dot // CMP_BLOCK))
    ckcv_s = _compress_sample(pt_flat, db, n_pages, cmp_pool_t, avg2_s, jnp.transpose(phi_t))
    ncs = past // CMP_BLOCK
    ctab_s = np.transpose(_key_rows(np.arange(ncs) * CMP_BLOCK + (CMP_BLOCK - 1)))
    qn = jnp.transpose(q_nsa.reshape(db, nq, NSA_G, NSA_HPG, LANES), (0, 2, 3, 1, 4))
    q_rows = qn.reshape(db, NSA_G, NSA_HPG * nq, LANES)
    o_cmp, imp = _nsa_cmp_sample(q_rows, ckcv_s, ctab_s, nq=nq, past=past)
    n_sel = -(-(past + nq) // SEL_BLOCK)
    n_pick = min(SEL_TOPK, n_sel) - 2
    ids = _topk_sample(imp[:, :, :nq].reshape(db * NSA_G * nq, -1), n_pick=n_pick, last_blk=n_sel - 1)
    o_sel = _nsa_sel_sample(pt_flat, ids[:, :SEL_TOPK], db, q_rows.reshape(db * NSA_G, -1, LANES), pad_rows(s_sel),
                            sel_pool_t, nblk=n_pick + 1, nq=nq, past=past).reshape(db, NSA_G, -1, LANES)
    wbuf = state_nsa_win_kv.shape[2]
    state_t = jnp.transpose(state_nsa_win_kv[0], (0, 2, 3, 4, 1)).reshape(db, -1, wbuf)
    wtab = _key_rows(past - wbuf + np.arange(wbuf))
    o_win = _nsa_win_sample(q_rows, state_t, wtab, pad_rows(s_win), nq=nq, past=past)
    to_tok = lambda a: jnp.transpose(a[..., NSA_DH:].reshape(db, NSA_G, NSA_HPG, nq, NSA_DH), (0, 3, 1, 2, 4)).reshape(ts, -1)
    gts =jnp.stack([gates[:, :12], gates[:, LANES:LANES + 12]], axis=1).reshape(ts, NSA_HEADS, 3)
    gexp = jnp.transpose(jnp.broadcast_to(gts[:, :, :, None], (ts, NSA_HEADS, 3, NSA_DH)), (2, 0, 1, 3)).reshape(3, ts, -1)
    o_b = _nsa_gate_sample(gexp, to_tok(o_cmp), to_tok(o_sel), to_tok(o_win))
    hs = _merge(o_a, o_b, merge, hs, w_a, w_b, w_o)

    xq = _norm_matmul(hs, g2(xattn_norm[0]), w_xq, scale=X_SCALE, out_dtype=BF16)
    xo = _xattn_sample(pad_rows(xq), cache_xattn_kv[0].reshape(db, -1, X_DH), n_mem=n_mem)
    hs = _matmul_residual(xo[:, :nq].reshape(ts, -1), w_xo, hs)
    y_sample = _ffn(hs, *f2, fg, final=True).reshape(db, nq, d)

    kvs = (1, db, nq, NSA_G, 2, NSA_DH)
    win_upd = jnp.concatenate([state_nsa_win_kv[0], s_win.reshape(kvs[1:])], axis=1)[:, nq:]
    sample_caches = (s_ckv.reshape(1, db, nq, -1), s_kr.reshape(1, db, nq, -1), s_cmp.reshape(kvs),
                     s_sel.reshape(kvs), win_upd[None])
    return (y_prompt, y_sample) + prompt_caches + sample_caches
```

```python
import functools
import math

import jax
import jax.numpy as jnp
import numpy as np
from jax import lax
from jax.experimental import pallas as pl
from jax.experimental.pallas import tpu as pltpu

F32 = jnp.float32
BF16 = jnp.bfloat16
I32 = jnp.int32

D_MODEL = 1024
EPS = 1e-6
PAGE = 128
MLA_HEADS = 8
MLA_Q_LORA = 384
MLA_KV_LORA = 256
MLA_NOPE = 64
MLA_ROPE = 32
MLA_V = 64
MLA_SCALE = (MLA_NOPE + MLA_ROPE) ** -0.5
ROPE_BASE = 10000.0
NSA_HEADS = 8
NSA_G = 2
NSA_HPG = 4
NSA_DH = 64
NSA_SCALE = NSA_DH ** -0.5
CMP_BLOCK = 32
SEL_BLOCK = 64
SEL_TOPK = 16
WINDOW = 512
FORCE_BONUS = 4.0 * NSA_HPG
X_HEADS = 4
X_DH = 128
X_SCALE = X_DH ** -0.5
LANES = 128
LOG2E = math.log2(math.e)
NEG_INF = float("-inf")
MASK_BIG = 2.0 ** 60
VMEM_LIMIT = 56 * 1024 * 1024
TILE = 256
QTILE = 512

C_CQ = 0
C_CKV = 384
C_KR = 640
C_QN = 768
C_GT = 1792
C_MG = 2048
C_KVN = 4096
C_END = 4864
A_BLK, A_OFF, A_ONE0, A_ONE1 = 64, 65, 66, 67


def _cparams(n_axes):
    return pltpu.CompilerParams(dimension_semantics=("arbitrary",) * n_axes, vmem_limit_bytes=VMEM_LIMIT)


def _dot(a, b):
    return jnp.dot(a, b, preferred_element_type=F32)


def _dot_nt(a, b):
    return lax.dot_general(a, b, (((1,), (1,)), ((), ())), preferred_element_type=F32)


def _rms(x, g):
    ms = jnp.mean(x * x, axis=-1, keepdims=True)
    return x * lax.rsqrt(ms + EPS) * g


def _iota(shape, dim):
    return lax.broadcasted_iota(I32, shape, dim)


def _pow2_neg(e):
    return lax.bitcast_convert_type((127 - e) << 23, F32)


def _query_aug(slope, qpos, lane):
    s64 = slope * float(SEL_BLOCK)
    return jnp.where(lane == A_BLK, s64,
           jnp.where(lane == A_OFF, slope,
           jnp.where(lane == A_ONE0, -s64 * (qpos >> 6).astype(F32),
           jnp.where(lane == A_ONE1, -slope * (qpos & (SEL_BLOCK - 1)).astype(F32), 0.0))))


def _flash_step(s, v_t, m_ref, acc_ref, idx, exp_fn):
    m_prev = m_ref[idx]
    m_new = jnp.maximum(m_prev, jnp.max(s, axis=-1, keepdims=True))
    alpha = exp_fn(m_prev - m_new)
    p = exp_fn(s - jnp.concatenate([m_new] * (s.shape[1] // LANES), axis=1))
    acc_ref[idx] = alpha * acc_ref[idx] + _dot_nt(p.astype(BF16), v_t)
    m_ref[idx] = m_new


def _ffn_kernel(x_ref, g_ref, wg_ref, wu_ref, wd_ref, fg_ref, o_ref, h_ref, acc_ref, *, final):
    j = pl.program_id(1)

    @pl.when(j == 0)
    def _():
        h_ref[...] = _rms(x_ref[...], g_ref[...]).astype(BF16)
        acc_ref[...] = jnp.zeros(acc_ref.shape, F32)

    h = h_ref[...]
    a = _dot(h, wg_ref[...])
    u = _dot(h, wu_ref[...])
    act = (a * jax.nn.sigmoid(a)) * u
    acc_ref[...] += _dot(act.astype(BF16), wd_ref[...])

    @pl.when(j == pl.num_programs(1) - 1)
    def _():
        y = x_ref[...] + 0.5 * acc_ref[...]
        if final:
            y = _rms(y, fg_ref[...])
        o_ref[...] = y


def _ffn(x, g, wg, wu, wd, fg, *, final):
    t, d = x.shape
    ff = wg.shape[1]
    tm = min(t, 1024)
    tf = 256
    return pl.pallas_call(
        functools.partial(_ffn_kernel, final=final),
        grid=(t // tm, ff // tf),
        in_specs=[
            pl.BlockSpec((tm, d), lambda i, j: (i, 0)),
            pl.BlockSpec((1, d), lambda i, j: (0, 0)),
            pl.BlockSpec((d, tf), lambda i, j: (0, j)),
            pl.BlockSpec((d, tf), lambda i, j: (0, j)),
            pl.BlockSpec((tf, d), lambda i, j: (j, 0)),
            pl.BlockSpec((1, d), lambda i, j: (0, 0)),
        ],
        out_specs=pl.BlockSpec((tm, d), lambda i, j: (i, 0)),
        out_shape=jax.ShapeDtypeStruct((t, d), F32),
        scratch_shapes=[pltpu.VMEM((tm, d), BF16), pltpu.VMEM((tm, d), F32)],
        compiler_params=_cparams(2),
    )(x, g, wg, wu, wd, fg)


def _proj_common(x_ref, g_ref, win_ref, qn_ref, wuq_ref, kvn_ref, rc_ref, rs1_ref, rs2_ref,
                 qmla_ref, ckv_ref, qnsa_ref, gates_ref, merge_ref):
    n = _rms(x_ref[...], g_ref[...]).astype(BF16)

    def seg(a, b):
        return _dot(n, win_ref[:, a:b])

    rc, rs1, rs2 = rc_ref[...], rs1_ref[...], rs2_ref[...]

    def rope(blk):
        return blk * rc + pltpu.roll(blk, 112, 1) * rs1 + pltpu.roll(blk, 16, 1) * rs2

    cq = _rms(seg(C_CQ, C_CKV), qn_ref[...]).astype(BF16)
    q = _dot(cq, wuq_ref[...]) * (MLA_SCALE * LOG2E)
    for h in range(MLA_HEADS):
        qmla_ref[:, h * LANES:(h + 1) * LANES] = rope(q[:, h * LANES:(h + 1) * LANES]).astype(BF16)

    c_kv = _rms(seg(C_CKV, C_KR), kvn_ref[...])
    ckv_ref[...] = c_kv
    krb = rope(seg(C_KR, C_QN))
    qnsa_ref[...] = (seg(C_QN, C_GT) * NSA_SCALE).astype(BF16)
    gates_ref[...] = jax.nn.sigmoid(seg(C_GT, C_MG))
    merge_ref[...] = jax.nn.sigmoid(seg(C_MG, C_KVN))
    return n, seg, c_kv, krb


def _proj_prompt_kernel(x_ref, g_ref, win_ref, wkv_t_ref, qn_ref, wuq_ref, kvn_ref, wuk_t_ref, wuv_t_ref,
                        rc_ref, rs1_ref, rs2_ref,
                        qmla_ref, k_t_ref, v_t_ref, ckv_ref, kr_t_ref, qnsa_ref, cmp_t_ref, sel_t_ref, win_t_ref,
                        sk_ref, sv_ref, wk_ref, wv_ref, gates_ref, merge_ref):
    n, _, c_kv, krb = _proj_common(x_ref, g_ref, win_ref, qn_ref, wuq_ref, kvn_ref, rc_ref, rs1_ref, rs2_ref,
                                   qmla_ref, ckv_ref, qnsa_ref, gates_ref, merge_ref)
    tm = n.shape[0]
    cb = c_kv.astype(BF16)
    kr_t = jnp.transpose(krb)
    kr_t_ref[0] = kr_t[MLA_NOPE:MLA_NOPE + MLA_ROPE]
    rows = _iota((LANES, tm), 0)
    k_t = _dot_nt(wuk_t_ref[...], cb)
    v_t = _dot_nt(wuv_t_ref[...], cb)
    for h in range(MLA_HEADS):
        hs = slice(h * LANES, (h + 1) * LANES)
        k_t_ref[0, h, 0] = (k_t[hs] + kr_t).astype(BF16)
        v_t_ref[0, h, 0] = jnp.where(rows >= MLA_V, 1.0, v_t[hs]).astype(BF16)
    kv_t = _dot_nt(wkv_t_ref[...], n)
    cmp_t_ref[0] = kv_t[0:256]
    sel_t_ref[0] = kv_t[256:512]
    win_t_ref[0] = kv_t[512:768]
    for g in range(NSA_G):
        for base, k_out, v_out in ((256, sk_ref, sv_ref), (512, wk_ref, wv_ref)):
            blk = kv_t[base + g * LANES:base + (g + 1) * LANES]
            k_out[0, g, 0] = blk.astype(BF16)
            v_out[0, g, 0] = jnp.where(rows < NSA_DH, 1.0, blk).astype(BF16)


def _proj_sample_kernel(x_ref, g_ref, win_ref, qn_ref, wuq_ref, kvn_ref, rc_ref, rs1_ref, rs2_ref,
                        qmla_ref, ckv_ref, kr_ref, qnsa_ref, cmp_ref, sel_ref, win_o_ref, gates_ref, merge_ref):
    _, seg, _, krb = _proj_common(x_ref, g_ref, win_ref, qn_ref, wuq_ref, kvn_ref, rc_ref, rs1_ref, rs2_ref,
                                  qmla_ref, ckv_ref, qnsa_ref, gates_ref, merge_ref)
    kr_ref[...] = krb
    cmp_ref[...] = seg(C_KVN, C_KVN + 256)
    sel_ref[...] = seg(C_KVN + 256, C_KVN + 512)
    win_o_ref[...] = seg(C_KVN + 512, C_END)


def _mixer_project_prompt(x, b, s, g, w_in_p, w_kv_t, q_norm, w_uq_p, kv_norm, w_uk_t, w_uv_t, rc, rs1, rs2):
    t, d = x.shape
    tm = TILE
    nt = s // tm
    full = lambda a: pl.BlockSpec(a.shape, lambda i: (0,) * a.ndim)
    row = lambda w: pl.BlockSpec((tm, w), lambda i: (i, 0))
    tab = pl.BlockSpec((tm, LANES), lambda i: (i % nt, 0))
    fm = lambda r: pl.BlockSpec((1, r, tm), lambda i: (i // nt, 0, i % nt))
    tiles = lambda h: pl.BlockSpec((1, h, 1, LANES, tm), lambda i: (i // nt, 0, i % nt, 0, 0))
    tok = lambda w, dt: jax.ShapeDtypeStruct((t, w), dt)
    fms = lambda r: jax.ShapeDtypeStruct((b, r, s), F32)
    til = lambda h: jax.ShapeDtypeStruct((b, h, nt, LANES, tm), BF16)
    outs = [(row(1024), tok(1024, BF16)), (tiles(MLA_HEADS), til(MLA_HEADS)), (tiles(MLA_HEADS), til(MLA_HEADS)),
            (row(256), tok(256, F32)), (fm(MLA_ROPE), fms(MLA_ROPE)), (row(1024), tok(1024, BF16)),
            (fm(256), fms(256)), (fm(256), fms(256)), (fm(256), fms(256)),
            (tiles(NSA_G), til(NSA_G)), (tiles(NSA_G), til(NSA_G)), (tiles(NSA_G), til(NSA_G)), (tiles(NSA_G), til(NSA_G)),
            (row(256), tok(256, F32)), (row(2048), tok(2048, F32))]
    ins = (x, g, w_in_p, w_kv_t, q_norm, w_uq_p, kv_norm, w_uk_t, w_uv_t)
    return pl.pallas_call(
        _proj_prompt_kernel,
        grid=(t // tm,),
        in_specs=[row(d)] + [full(a) for a in ins[1:]] + [tab, tab, tab],
        out_specs=[o[0] for o in outs],
        out_shape=[o[1] for o in outs],
        compiler_params=_cparams(1),
    )(*ins, rc, rs1, rs2)


def _mixer_project_sample(x, g, w_in_p, q_norm, w_uq_p, kv_norm, rc, rs1, rs2):
    t, d = x.shape
    tm = min(t, TILE)
    full = lambda a: pl.BlockSpec(a.shape, lambda i: (0,) * a.ndim)
    row = lambda w: pl.BlockSpec((tm, w), lambda i: (i, 0))
    widths = [(1024, BF16), (256, F32), (128, F32), (1024, BF16), (256, F32), (256, F32), (256, F32), (256, F32),
              (2048, F32)]
    ins = (x, g, w_in_p, q_norm, w_uq_p, kv_norm)
    return pl.pallas_call(
        _proj_sample_kernel,
        grid=(t // tm,),
        in_specs=[row(d)] + [full(a) for a in ins[1:]] + [row(LANES)] * 3,
        out_specs=[row(w) for w, _ in widths],
        out_shape=[jax.ShapeDtypeStruct((t, w), dt) for w, dt in widths],
        compiler_params=_cparams(1),
    )(*ins, rc, rs1, rs2)


def _norm_mm_kernel(x_ref, g_ref, w_ref, o_ref, *, scale):
    y = _dot(_rms(x_ref[...], g_ref[...]).astype(BF16), w_ref[...])
    if scale != 1.0:
        y = y * scale
    o_ref[...] = y.astype(o_ref.dtype)


def _norm_matmul(x, g, w, *, scale=1.0, out_dtype=F32):
    t, d = x.shape
    n = w.shape[1]
    tm = min(t, 512)
    return pl.pallas_call(
        functools.partial(_norm_mm_kernel, scale=scale),
        grid=(t // tm,),
        in_specs=[pl.BlockSpec((tm, d), lambda i: (i, 0)), pl.BlockSpec((1, d), lambda i: (0, 0)),
                  pl.BlockSpec((d, n), lambda i: (0, 0))],
        out_specs=pl.BlockSpec((tm, n), lambda i: (i, 0)),
        out_shape=jax.ShapeDtypeStruct((t, n), out_dtype),
        compiler_params=_cparams(1),
    )(x, g, w)


def _mm_kernel(a_ref, w_ref, o_ref):
    o_ref[...] = _dot(a_ref[...], w_ref[...]).astype(o_ref.dtype)


def _matmul(a, w, *, out_dtype=F32):
    t, k = a.shape
    n = w.shape[1]
    tm = min(t, 512)
    return pl.pallas_call(
        _mm_kernel,
        grid=(t // tm,),
        in_specs=[pl.BlockSpec((tm, k), lambda i: (i, 0)), pl.BlockSpec((k, n), lambda i: (0, 0))],
        out_specs=pl.BlockSpec((tm, n), lambda i: (i, 0)),
        out_shape=jax.ShapeDtypeStruct((t, n), out_dtype),
        compiler_params=_cparams(1),
    )(a, w)


def _mm_res_kernel(a_ref, w_ref, x_ref, o_ref):
    o_ref[...] = x_ref[...] + _dot(a_ref[...], w_ref[...])


def _matmul_residual(a, w, x):
    t, k = a.shape
    n = w.shape[1]
    tm = min(t, 512)
    return pl.pallas_call(
        _mm_res_kernel,
        grid=(t // tm,),
        in_specs=[pl.BlockSpec((tm, k), lambda i: (i, 0)), pl.BlockSpec((k, n), lambda i: (0, 0)),
                  pl.BlockSpec((tm, n), lambda i: (i, 0))],
        out_specs=pl.BlockSpec((tm, n), lambda i: (i, 0)),
        out_shape=jax.ShapeDtypeStruct((t, n), F32),
        compiler_params=_cparams(1),
    )(a, w, x)


def _merge_kernel(oa_ref, ob_ref, mg_ref, x_ref, wa_ref, wb_ref, wo_ref, o_ref):
    d = x_ref.shape[1]
    mix = mg_ref[:, :d] * _dot(oa_ref[...], wa_ref[...]) + mg_ref[:, d:] * _dot(ob_ref[...], wb_ref[...])
    o_ref[...] = x_ref[...] + _dot(mix.astype(BF16), wo_ref[...])


def _merge(o_a, o_b, mg, x, w_a, w_b, w_o):
    t, d = x.shape
    tm = min(t, 512)
    row = lambda w: pl.BlockSpec((tm, w), lambda i: (i, 0))
    full = lambda a: pl.BlockSpec(a.shape, lambda i: (0, 0))
    return pl.pallas_call(
        _merge_kernel,
        grid=(t // tm,),
        in_specs=[row(o_a.shape[1]), row(o_b.shape[1]), row(2 * d), row(d), full(w_a), full(w_b), full(w_o)],
        out_specs=row(d),
        out_shape=jax.ShapeDtypeStruct((t, d), F32),
        compiler_params=_cparams(1),
    )(o_a, o_b, mg, x, w_a, w_b, w_o)


MLA_HPS = 4


def _mla_prompt_kernel(q_ref, k_ref, v_ref, o_ref, m_ref, acc_ref, *, tq, tk):
    i = pl.program_id(2)
    q = q_ref[0]
    qs = [q[:, h * LANES:(h + 1) * LANES] for h in range(MLA_HPS)]
    per_q = tq // tk
    m_ref[...] = jnp.full(m_ref.shape, NEG_INF, F32)
    acc_ref[...] = jnp.zeros(acc_ref.shape, F32)

    def tile(j, causal):
        if causal:
            ok = (j * tk + _iota((tq, tk), 1)) <= (i * tq + _iota((tq, tk), 0))
        for h in range(MLA_HPS):
            s = _dot(qs[h], k_ref[0, h, j])
            if causal:
                s = jnp.where(ok, s, -MASK_BIG)
            _flash_step(s, v_ref[0, h, j], m_ref, acc_ref, h, jnp.exp2)

    def body(j, c):
        tile(j, False)
        return c

    lax.fori_loop(0, per_q * i, body, 0)
    for d in range(per_q):
        tile(per_q * i + d, True)
    lane = _iota((1, LANES), 1)
    outs = []
    for h in range(MLA_HPS):
        a = acc_ref[h]
        outs.append(a / pltpu.roll(a, MLA_V, 1))
    for p in range(MLA_HPS // 2):
        pair = jnp.where(lane < MLA_V, outs[2 * p], pltpu.roll(outs[2 * p + 1], MLA_V, 1))
        o_ref[0, :, p * LANES:(p + 1) * LANES] = pair.astype(o_ref.dtype)


def _mla_prompt(q, k_t, v_t):
    b, s, _ = q.shape
    tq = QTILE
    nt = s // TILE
    kv_spec = pl.BlockSpec((1, MLA_HPS, nt, LANES, TILE), lambda bi, hq, i: (bi, hq, 0, 0, 0))
    return pl.pallas_call(
        functools.partial(_mla_prompt_kernel, tq=tq, tk=TILE),
        grid=(b, MLA_HEADS // MLA_HPS, s // tq),
        in_specs=[pl.BlockSpec((1, tq, MLA_HPS * LANES), lambda bi, hq, i: (bi, i, hq)), kv_spec, kv_spec],
        out_specs=pl.BlockSpec((1, tq, MLA_HPS * MLA_V), lambda bi, hq, i: (bi, i, hq)),
        out_shape=jax.ShapeDtypeStruct((b, s, MLA_HEADS * MLA_V), BF16),
        scratch_shapes=[pltpu.VMEM((MLA_HPS, tq, LANES), F32), pltpu.VMEM((MLA_HPS, tq, LANES), F32)],
        compiler_params=_cparams(3),
    )(q, k_t, v_t)


def _split_hi_lo(x):
    hi = x.astype(BF16)
    return hi, (x - hi.astype(F32)).astype(BF16)


def _compress_prompt_kernel(x_ref, a_ref, phi_ref, o_ref):
    hi, lo = _split_hi_lo(x_ref[0])
    means = _dot(jnp.concatenate([hi, lo], axis=1), a_ref[...])
    o_ref[0] = _dot(phi_ref[...], means.astype(BF16)).astype(o_ref.dtype)


def _compress_prompt(cmp_t, avg2, phi_t):
    b, w, s = cmp_t.shape
    nc = avg2.shape[1]
    return pl.pallas_call(
        _compress_prompt_kernel,
        grid=(b,),
        in_specs=[pl.BlockSpec((1, w, s), lambda i: (i, 0, 0)), pl.BlockSpec(avg2.shape, lambda i: (0, 0)),
                  pl.BlockSpec(phi_t.shape, lambda i: (0, 0))],
        out_specs=pl.BlockSpec((1, w, nc), lambda i: (i, 0, 0)),
        out_shape=jax.ShapeDtypeStruct((b, w, nc), BF16),
        compiler_params=_cparams(1),
    )(cmp_t, avg2, phi_t)


def _nsa_prompt_kernel(q_ref, c_ref, ctab_ref, sk_ref, sv_ref, wk_ref, wv_ref, ktab_ref, gt_ref, o_ref,
                       m_ref, acc_ref, *, tq, tk, n_sel):
    g = pl.program_id(1)
    i = pl.program_id(2)
    r = NSA_HPG * tq
    shift = tq.bit_length() - 1
    row = _iota((r, LANES), 0)
    lane_r = _iota((r, LANES), 1)
    qpos = i * tq + (row & (tq - 1))
    slope = _pow2_neg(NSA_HPG * g + (row >> shift) + 1)
    aq = _query_aug(slope, qpos, lane_r)
    q_st = jnp.concatenate([q_ref[0, :, h * LANES:(h + 1) * LANES] for h in range(NSA_HPG)], axis=0)
    q_plain = jnp.concatenate([q_st, aq.astype(BF16)], axis=1)

    kc = c_ref[0]
    nc = kc.shape[1]
    half = nc // 2
    lane_c = _iota((1, nc), 1)
    cblk = jnp.where(lane_c < half, 2 * lane_c, 2 * (lane_c - half) + 1)
    cmask = (cblk * CMP_BLOCK + (CMP_BLOCK - 1)) <= (i * tq + (_iota((r, nc), 0) & (tq - 1)))
    s = _dot(q_plain, jnp.concatenate([kc, ctab_ref[...]], axis=0))
    s = jnp.where(cmask, s, NEG_INF)
    mx = jnp.max(s, axis=-1, keepdims=True)
    mx = jnp.where(mx == NEG_INF, 0.0, mx)
    e = jnp.where(cmask, jnp.exp(s - mx), 0.0)
    p = e / jnp.maximum(jnp.sum(e, axis=-1, keepdims=True), 1e-30)
    o_cmp = _dot_nt(p.astype(BF16), kc)

    psum = p[0:tq] + p[tq:2 * tq] + p[2 * tq:3 * tq] + p[3 * tq:4 * tq]
    imp = psum + pltpu.roll(psum, half, 1)
    qp = i * tq + _iota((tq, 1), 0)
    forced = (lane_c == 0) | (lane_c == (qp >> 6))
    score = jnp.where(lane_c * SEL_BLOCK <= qp, imp + jnp.where(forced, FORCE_BONUS, 0.0), -1.0)
    st = jnp.transpose(score)[0:n_sel]
    jidx = _iota((n_sel, 1), 0)
    rank = jnp.zeros((n_sel, tq), F32)
    for ii in range(n_sel):
        ri = st[ii:ii + 1, :]
        first = jnp.where(jidx > ii, 1.0, 0.0)
        rank = rank + jnp.where(ri > st, 1.0, jnp.where(ri == st, first, 0.0))
    valid_t = (jidx * SEL_BLOCK) <= (i * tq + _iota((1, tq), 1))
    mt = jnp.where(valid_t, jnp.where(rank < float(min(SEL_TOPK, n_sel)), 1.0, 0.0), 0.0)
    if n_sel < LANES:
        mt = jnp.concatenate([mt, jnp.zeros((LANES - n_sel, tq), F32)], axis=0)
    msel = jnp.transpose(mt)
    mst = jnp.concatenate([msel] * NSA_HPG, axis=0)
    q_sel = jnp.concatenate([q_st, jnp.where(lane_r < n_sel, (mst - 1.0) * MASK_BIG, aq).astype(BF16)], axis=1)

    lane = _iota((1, LANES), 1)
    m_ref[...] = jnp.full(m_ref.shape, NEG_INF, F32)
    acc_ref[...] = jnp.zeros(acc_ref.shape, F32)
    per_q = tq // tk
    rel = (_iota((r, tk), 0) & (tq - 1)) - _iota((r, tk), 1)

    def dist_to(j):
        return rel + (i * tq - j * tk)

    def sel_tile(j, causal):
        sc = _dot(q_sel, jnp.concatenate([sk_ref[0, 0, j], ktab_ref[j]], axis=0))
        if causal:
            sc = jnp.where(dist_to(j) >= 0, sc, -MASK_BIG)
        _flash_step(sc, sv_ref[0, 0, j], m_ref, acc_ref, 0, jnp.exp)

    def sel_body(j, c):
        sel_tile(j, False)
        return c

    lax.fori_loop(0, per_q * i, sel_body, 0)
    for dd in range(per_q):
        sel_tile(per_q * i + dd, True)

    def win_body(j, c):
        in_window = lax.bitcast_convert_type(dist_to(j), jnp.uint32) <= jnp.uint32(WINDOW)
        sc = _dot(q_plain, jnp.concatenate([wk_ref[0, 0, j], ktab_ref[j]], axis=0))
        _flash_step(jnp.where(in_window, sc, -MASK_BIG), wv_ref[0, 0, j], m_ref, acc_ref, 1, jnp.exp)
        return c

    lax.fori_loop(jnp.maximum(per_q * i - WINDOW // tk, 0), per_q * (i + 1), win_body, 0)

    gt = gt_ref[0]
    heads = []
    for h in range(NSA_HPG):
        rs = slice(h * tq, (h + 1) * tq)
        a_sel, a_win = acc_ref[0, rs], acc_ref[1, rs]
        heads.append(gt[:, 3 * h:3 * h + 1] * o_cmp[rs]
                     + (gt[:, 3 * h + 1:3 * h + 2] / a_sel[:, 0:1]) * a_sel
                     + (gt[:, 3 * h + 2:3 * h + 3] / a_win[:, 0:1]) * a_win)
    for p2 in range(NSA_HPG // 2):
        pair = jnp.where(lane < NSA_DH, pltpu.roll(heads[2 * p2], NSA_DH, 1), heads[2 * p2 + 1])
        o_ref[0, :, p2 * LANES:(p2 + 1) * LANES] = pair.astype(o_ref.dtype)


def _nsa_prompt(q, ckcv_t, ctab, sk, sv, wk, wv, ktab, gates):
    b, s, _ = q.shape
    tq = QTILE
    nt = s // TILE
    nc = ckcv_t.shape[2]
    r = NSA_HPG * tq
    kv = pl.BlockSpec((1, 1, nt, LANES, TILE), lambda bi, gi, i: (bi, gi, 0, 0, 0))
    return pl.pallas_call(
        functools.partial(_nsa_prompt_kernel, tq=tq, tk=TILE, n_sel=s // SEL_BLOCK),
        grid=(b, NSA_G, s // tq),
        in_specs=[pl.BlockSpec((1, tq, NSA_HPG * LANES), lambda bi, gi, i: (bi, i, gi)),
                  pl.BlockSpec((1, LANES, nc), lambda bi, gi, i: (bi, gi, 0)),
                  pl.BlockSpec(ctab.shape, lambda bi, gi, i: (0, 0)),
                  kv, kv, kv, kv,
                  pl.BlockSpec(ktab.shape, lambda bi, gi, i: (0, 0, 0)),
                  pl.BlockSpec((1, tq, LANES), lambda bi, gi, i: (bi, i, gi))],
        out_specs=pl.BlockSpec((1, tq, NSA_HPG * NSA_DH), lambda bi, gi, i: (bi, i, gi)),
        out_shape=jax.ShapeDtypeStruct((b, s, NSA_HEADS * NSA_DH), BF16),
        scratch_shapes=[pltpu.VMEM((2, r, LANES), F32), pltpu.VMEM((2, r, LANES), F32)],
        compiler_params=_cparams(3),
    )(q, ckcv_t, ctab, sk, sv, wk, wv, ktab, gates)


def _xattn_prompt_kernel(q_ref, kv_ref, o_ref):
    hw = X_HEADS * X_DH
    for h in range(X_HEADS):
        cs = slice(h * X_DH, (h + 1) * X_DH)
        s = _dot_nt(q_ref[0, :, cs], kv_ref[0, :, cs])
        e = jnp.exp(s - jnp.max(s, axis=-1, keepdims=True))
        p = e / jnp.sum(e, axis=-1, keepdims=True)
        o_ref[0, :, cs] = _dot(p.astype(BF16), kv_ref[0, :, hw + h * X_DH:hw + (h + 1) * X_DH]).astype(o_ref.dtype)


def _xattn_prompt(q, kv):
    b, s, w = q.shape
    m = kv.shape[1]
    tq = min(s, 512)
    return pl.pallas_call(
        _xattn_prompt_kernel,
        grid=(b, s // tq),
        in_specs=[pl.BlockSpec((1, tq, w), lambda bi, i: (bi, i, 0)),
                  pl.BlockSpec((1, m, 2 * w), lambda bi, i: (bi, 0, 0))],
        out_specs=pl.BlockSpec((1, tq, w), lambda bi, i: (bi, i, 0)),
        out_shape=jax.ShapeDtypeStruct((b, s, w), BF16),
        compiler_params=_cparams(2),
    )(q, kv)


def _qabs_kernel(q_ref, w_ref, o_ref):
    o_ref[0] = _dot(q_ref[...], w_ref[0]).astype(o_ref.dtype)


def _mla_absorb_q(q_mla, w_abs):
    t = q_mla.shape[0]
    n = w_abs.shape[2]
    return pl.pallas_call(
        _qabs_kernel,
        grid=(MLA_HEADS,),
        in_specs=[pl.BlockSpec((t, LANES), lambda h: (0, h)), pl.BlockSpec((1, LANES, n), lambda h: (h, 0, 0))],
        out_specs=pl.BlockSpec((1, t, n), lambda h: (h, 0, 0)),
        out_shape=jax.ShapeDtypeStruct((MLA_HEADS, t, n), BF16),
        compiler_params=_cparams(1),
    )(q_mla, w_abs)


def _double_buffered(step, n_steps, copies, per_page=1):
    def start_all(cps):
        for k, cp in enumerate(cps):
            cp.start(priority=(k // per_page) % 2)

    @pl.when(step == 0)
    def _():
        start_all(copies(step, 0, True))

    @pl.when(step + 1 < n_steps)
    def _():
        start_all(copies(step + 1, (step + 1) & 1, True))

    slot = step & 1
    for cp in copies(step, slot, False):
        cp.wait()
    return slot


def _pages_on_lanes(buf, slot, first, n):
    return jnp.concatenate([buf[slot, first + p] for p in range(n)], axis=1)


def _mla_sample_kernel(pt_ref, qa_ref, cn_ref, krn_ref, c_hbm, kr_hbm, o_ref, cbuf, krbuf, csem, krsem,
                       *, n_pages, db, nq):
    def copies(bb, slot, lookup):
        out = []
        for p in range(n_pages):
            page = pt_ref[p * db + bb] if lookup else 0
            out.append(pltpu.make_async_copy(c_hbm.at[page], cbuf.at[slot, pl.ds(p * PAGE, PAGE), :], csem.at[slot]))
            out.append(pltpu.make_async_copy(kr_hbm.at[page], krbuf.at[slot, p], krsem.at[slot]))
        return out

    slot = _double_buffered(pl.program_id(0), pl.num_programs(0), copies, per_page=2)
    qa = qa_ref[0]
    qc, qr = qa[:, :MLA_KV_LORA], qa[:, MLA_KV_LORA:MLA_KV_LORA + MLA_ROPE]
    c = cbuf[slot].astype(BF16)
    kr_t = _pages_on_lanes(krbuf, slot, 0, n_pages).astype(BF16)
    s1 = _dot_nt(qc, c) + _dot(qr, kr_t)
    cn = cn_ref[0].astype(BF16)
    s2 = _dot_nt(qc, cn) + _dot_nt(qr, krn_ref[0].astype(BF16))
    qi = _iota((qa.shape[0], 1), 0) & (nq - 1)
    s2 = jnp.where(_iota((1, cn.shape[0]), 1) <= qi, s2, NEG_INF)
    mx = jnp.maximum(jnp.max(s1, axis=-1, keepdims=True), jnp.max(s2, axis=-1, keepdims=True))
    e1 = jnp.exp2(s1 - mx)
    e2 = jnp.exp2(s2 - mx)
    den = jnp.sum(e1, axis=-1, keepdims=True) + jnp.sum(e2, axis=-1, keepdims=True)
    o_ref[0] = (_dot(e1.astype(BF16), c) + _dot(e2.astype(BF16), cn)) / den


def _mla_sample(pt_flat, db, n_pages, qa, c_new, kr_new, ckv_pool, kr_pool_t, *, nq):
    r = qa.shape[1]
    past = n_pages * PAGE
    per_b = lambda a: pl.BlockSpec((1,) + a.shape[1:], lambda b, pt: (b, 0, 0))
    hbm = pl.BlockSpec(memory_space=pl.ANY)
    grid_spec = pltpu.PrefetchScalarGridSpec(
        num_scalar_prefetch=1,
        grid=(db,),
        in_specs=[per_b(qa), per_b(c_new), per_b(kr_new), hbm, hbm],
        out_specs=pl.BlockSpec((1, r, MLA_KV_LORA), lambda b, pt: (b, 0, 0)),
        scratch_shapes=[pltpu.VMEM((2, past, MLA_KV_LORA), F32), pltpu.VMEM((2, n_pages, MLA_ROPE, PAGE), F32),
                        pltpu.SemaphoreType.DMA((2,)), pltpu.SemaphoreType.DMA((2,))],
    )
    return pl.pallas_call(
        functools.partial(_mla_sample_kernel, n_pages=n_pages, db=db, nq=nq),
        grid_spec=grid_spec,
        out_shape=jax.ShapeDtypeStruct((db, r, MLA_KV_LORA), F32),
        compiler_params=_cparams(1),
    )(pt_flat, qa, c_new, kr_new, ckv_pool, kr_pool_t)


def _cmp_sample_kernel(pt_ref, a_ref, phi_ref, pool_hbm, o_ref, buf, sem, *, n_pages, db):
    def copies(bb, slot, lookup):
        return [pltpu.make_async_copy(pool_hbm.at[pt_ref[p * db + bb] if lookup else 0], buf.at[slot, p], sem.at[slot])
                for p in range(n_pages)]

    slot = _double_buffered(pl.program_id(0), pl.num_programs(0), copies)
    pages = a_ref.shape[0] // (2 * PAGE)
    blocks = a_ref.shape[1]
    for c in range(n_pages // pages):
        hi, lo = _split_hi_lo(_pages_on_lanes(buf, slot, c * pages, pages))
        lhs = jnp.concatenate([hi, lo], axis=1)
        half = lhs.shape[0] // 2
        means = jnp.concatenate([_dot(lhs[:half], a_ref[...]), _dot(lhs[half:], a_ref[...])], axis=0)
        means = jnp.transpose(means).astype(BF16)
        o_ref[0, c * blocks:(c + 1) * blocks, :] = _dot(means, phi_ref[...]).astype(o_ref.dtype)


def _compress_sample(pt_flat, db, n_pages, cmp_pool_t, avg2, phi):
    w = cmp_pool_t.shape[1]
    past = n_pages * PAGE
    grid_spec = pltpu.PrefetchScalarGridSpec(
        num_scalar_prefetch=1,
        grid=(db,),
        in_specs=[pl.BlockSpec(avg2.shape, lambda b, pt: (0, 0)), pl.BlockSpec(phi.shape, lambda b, pt: (0, 0)),
                  pl.BlockSpec(memory_space=pl.ANY)],
        out_specs=pl.BlockSpec((1, past // CMP_BLOCK, w), lambda b, pt: (b, 0, 0)),
        scratch_shapes=[pltpu.VMEM((2, n_pages, w, PAGE), F32), pltpu.SemaphoreType.DMA((2,))],
    )
    return pl.pallas_call(
        functools.partial(_cmp_sample_kernel, n_pages=n_pages, db=db),
        grid_spec=grid_spec,
        out_shape=jax.ShapeDtypeStruct((db, past // CMP_BLOCK, w), BF16),
        compiler_params=_cparams(1),
    )(pt_flat, avg2, phi, cmp_pool_t)


def _nsa_cmp_sample_kernel(q_ref, c_ref, ctab_ref, o_ref, imp_ref, *, nq, past):
    r = q_ref.shape[2]
    nc = c_ref.shape[1]
    row = _iota((r, LANES), 0)
    lane_r = _iota((r, LANES), 1)
    qpos = past + (row & (nq - 1))
    cmask = (_iota((1, nc), 1) * CMP_BLOCK + (CMP_BLOCK - 1)) <= (past + (_iota((r, nc), 0) & (nq - 1)))
    for bi, g in _batch_groups(q_ref.shape[0]):
        slope = _pow2_neg(NSA_HPG * g + (row >> (nq.bit_length() - 1)) + 1)
        q2 = jnp.concatenate([q_ref[bi, g], _query_aug(slope, qpos, lane_r).astype(BF16)], axis=1)
        kc = c_ref[bi, :, g * LANES:(g + 1) * LANES]
        s = _dot_nt(q2, jnp.concatenate([kc, ctab_ref[...]], axis=1))
        s = jnp.where(cmask, s, NEG_INF)
        mx = jnp.max(s, axis=-1, keepdims=True)
        mx = jnp.where(mx == NEG_INF, 0.0, mx)
        e = jnp.where(cmask, jnp.exp(s - mx), 0.0)
        p = e / jnp.maximum(jnp.sum(e, axis=-1, keepdims=True), 1e-30)
        o_ref[bi, g] = _dot(p.astype(BF16), kc)
        ps = p
        for h in range(1, NSA_HPG):
            ps = ps + pltpu.roll(p, h * nq, 0)
        imp_ref[bi, g] = ps + pltpu.roll(ps, nc - 1, 1)


def _batch_groups(bc):
    return [(bi, g) for bi in range(bc) for g in range(NSA_G)]


def _decode_batch(db):
    return 4 if db % 4 == 0 else 1


def _nsa_cmp_sample(q, ckcv, ctab, *, nq, past):
    db, g, r, _ = q.shape
    nc = ckcv.shape[1]
    bc = _decode_batch(db)
    return pl.pallas_call(
        functools.partial(_nsa_cmp_sample_kernel, nq=nq, past=past),
        grid=(db // bc,),
        in_specs=[pl.BlockSpec((bc,) + q.shape[1:], lambda b: (b, 0, 0, 0)),
                  pl.BlockSpec((bc,) + ckcv.shape[1:], lambda b: (b, 0, 0)),
                  pl.BlockSpec(ctab.shape, lambda b: (0, 0))],
        out_specs=[pl.BlockSpec((bc, g, r, LANES), lambda b: (b, 0, 0, 0)),
                   pl.BlockSpec((bc, g, r, nc), lambda b: (b, 0, 0, 0))],
        out_shape=[jax.ShapeDtypeStruct((db, g, r, LANES), F32), jax.ShapeDtypeStruct((db, g, r, nc), F32)],
        compiler_params=_cparams(1),
    )(q, ckcv, ctab)


def _topk_sample_kernel(imp_ref, o_ref, *, n_pick, last_blk):
    s = imp_ref[...]
    lane_i = _iota(s.shape, 1)
    lane = lane_i.astype(F32)
    s = jnp.where(((lane_i & 1) == 0) & (lane_i > 0), s, -1.0)
    olane = _iota(o_ref.shape, 1)
    out = jnp.where(olane == n_pick + 1, float(last_blk), 0.0)
    for it in range(n_pick):
        mx = jnp.max(s, axis=-1, keepdims=True)
        idx = jnp.min(jnp.where(s == mx, lane, float(s.shape[1])), axis=-1, keepdims=True)
        s = jnp.where(lane == idx, -1.0, s)
        out = jnp.where(olane == it, idx * 0.5, out)
    o_ref[...] = out.astype(I32)


def _topk_sample(imp, *, n_pick, last_blk):
    rows = imp.shape[0]
    return pl.pallas_call(
        functools.partial(_topk_sample_kernel, n_pick=n_pick, last_blk=last_blk),
        grid=(1,),
        in_specs=[pl.BlockSpec(imp.shape, lambda i: (0, 0))],
        out_specs=pl.BlockSpec((rows, LANES), lambda i: (0, 0)),
        out_shape=jax.ShapeDtypeStruct((rows, LANES), I32),
        compiler_params=_cparams(1),
    )(imp)


def _nsa_sel_sample_kernel(pt_ref, ids_ref, q_ref, new_ref, pool_hbm, o_ref, buf, sem, *, nblk, nq, db, past):
    def block_id(t, qq, c):
        return ids_ref[(t * nq + qq) * SEL_TOPK + c]

    def copies(t, slot, lookup):
        b = t >> 1
        rows = pl.ds(pl.multiple_of((t & 1) * LANES, LANES), LANES)
        out = []
        for qq in range(nq):
            for c in range(nblk if qq == 0 else nblk - 1):
                page = pt_ref[(block_id(t, qq, c) >> 1) * db + b] if lookup else 0
                out.append(pltpu.make_async_copy(pool_hbm.at[page, rows, :], buf.at[slot, qq * nblk + c], sem.at[slot]))
        return out

    t = pl.program_id(0)
    slot = _double_buffered(t, pl.num_programs(0), copies)
    g = t & 1
    q = q_ref[0]
    r = q.shape[0]
    pages = [buf[slot, (0 if c == nblk - 1 else qq) * nblk + c] for qq in range(nq) for c in range(nblk)]
    kv_t = jnp.concatenate(pages, axis=1).astype(BF16)
    lane = _iota((1, PAGE), 1)
    kpos, keep, owner = [], [], []
    for qq in range(nq):
        for c in range(nblk):
            blk = block_id(t, qq, c)
            kpos.append((blk >> 1) * PAGE + lane)
            keep.append((lane >> 6) == (blk & 1))
        owner.append(jnp.full((1, nblk * PAGE), qq, I32))
    kpos = jnp.concatenate(kpos, axis=1)
    keep = jnp.concatenate(keep, axis=1)
    owner = jnp.concatenate(owner, axis=1)
    row = _iota((r, 1), 0)
    qi = row & (nq - 1)
    slope = _pow2_neg(NSA_HPG * g + (row >> (nq.bit_length() - 1)) + 1)
    s1 = _dot(q, kv_t) - slope * (past + qi - kpos).astype(F32)
    s1 = jnp.where(owner == qi, jnp.where(keep, s1, NEG_INF), NEG_INF)
    nw = new_ref[0].astype(BF16)
    tnew = _iota((1, nw.shape[0]), 1)
    s2 = _dot_nt(q, nw) - slope * (qi - tnew).astype(F32)
    s2 = jnp.where(tnew <= qi, s2, NEG_INF)
    mx = jnp.maximum(jnp.max(s1, axis=-1, keepdims=True), jnp.max(s2, axis=-1, keepdims=True))
    e1 = jnp.exp(s1 - mx)
    e2 = jnp.exp(s2 - mx)
    den = jnp.sum(e1, axis=-1, keepdims=True) + jnp.sum(e2, axis=-1, keepdims=True)
    o_ref[0] = (_dot_nt(e1.astype(BF16), kv_t) + _dot(e2.astype(BF16), nw)) / den


def _nsa_sel_sample(pt_flat, ids, db, q, sel_new, sel_pool_t, *, nblk, nq, past):
    steps, r, _ = q.shape
    grid_spec = pltpu.PrefetchScalarGridSpec(
        num_scalar_prefetch=2,
        grid=(steps,),
        in_specs=[pl.BlockSpec((1, r, LANES), lambda t, pt, idr: (t, 0, 0)),
                  pl.BlockSpec((1, sel_new.shape[1], LANES), lambda t, pt, idr: (t // NSA_G, 0, t % NSA_G)),
                  pl.BlockSpec(memory_space=pl.ANY)],
        out_specs=pl.BlockSpec((1, r, LANES), lambda t, pt, idr: (t, 0, 0)),
        scratch_shapes=[pltpu.VMEM((2, nq * nblk, LANES, PAGE), F32), pltpu.SemaphoreType.DMA((2,))],
    )
    return pl.pallas_call(
        functools.partial(_nsa_sel_sample_kernel, nblk=nblk, nq=nq, db=db, past=past),
        grid_spec=grid_spec,
        out_shape=jax.ShapeDtypeStruct((steps, r, LANES), F32),
        compiler_params=_cparams(1),
    )(pt_flat, ids.reshape(-1), q, sel_new, sel_pool_t)


def _nsa_win_sample_kernel(q_ref, st_ref, wtab_ref, new_ref, o_ref, *, nq, past):
    r = q_ref.shape[2]
    wbuf = st_ref.shape[2]
    row = _iota((r, LANES), 0)
    lane_r = _iota((r, LANES), 1)
    qpos = past + (row & (nq - 1))
    qi = _iota((r, 1), 0) & (nq - 1)
    dist_st = qi + wbuf - _iota((1, wbuf), 1)
    tnew = _iota((1, new_ref.shape[1]), 1)
    for bi, g in _batch_groups(q_ref.shape[0]):
        slope = _pow2_neg(NSA_HPG * g + (row >> (nq.bit_length() - 1)) + 1)
        q = q_ref[bi, g]
        q2 = jnp.concatenate([q, _query_aug(slope, qpos, lane_r).astype(BF16)], axis=1)
        st = st_ref[bi, g * LANES:(g + 1) * LANES, :].astype(BF16)
        nw = new_ref[bi, :, g * LANES:(g + 1) * LANES].astype(BF16)
        s1 = _dot(q2, jnp.concatenate([st, wtab_ref[...]], axis=0))
        s1 = jnp.where(dist_st <= WINDOW, s1, NEG_INF)
        s2 = _dot_nt(q, nw) - slope[:, :1] * (qi - tnew).astype(F32)
        s2 = jnp.where(tnew <= qi, s2, NEG_INF)
        mx = jnp.maximum(jnp.max(s1, axis=-1, keepdims=True), jnp.max(s2, axis=-1, keepdims=True))
        e1 = jnp.exp(s1 - mx)
        e2 = jnp.exp(s2 - mx)
        den = jnp.sum(e1, axis=-1, keepdims=True) + jnp.sum(e2, axis=-1, keepdims=True)
        o_ref[bi, g] = (_dot_nt(e1.astype(BF16), st) + _dot(e2.astype(BF16), nw)) / den


def _nsa_win_sample(q, state_t, wtab, win_new, *, nq, past):
    db, g, r, _ = q.shape
    bc = _decode_batch(db)
    return pl.pallas_call(
        functools.partial(_nsa_win_sample_kernel, nq=nq, past=past),
        grid=(db // bc,),
        in_specs=[pl.BlockSpec((bc,) + q.shape[1:], lambda b: (b, 0, 0, 0)),
                  pl.BlockSpec((bc,) + state_t.shape[1:], lambda b: (b, 0, 0)),
                  pl.BlockSpec(wtab.shape, lambda b: (0, 0)),
                  pl.BlockSpec((bc,) + win_new.shape[1:], lambda b: (b, 0, 0))],
        out_specs=pl.BlockSpec((bc, g, r, LANES), lambda b: (b, 0, 0, 0)),
        out_shape=jax.ShapeDtypeStruct((db, g, r, LANES), F32),
        compiler_params=_cparams(1),
    )(q, state_t, wtab, win_new)


def _gate_kernel(g_ref, a_ref, b_ref, c_ref, o_ref):
    o_ref[...] = (g_ref[0] * a_ref[...] + g_ref[1] * b_ref[...] + g_ref[2] * c_ref[...]).astype(o_ref.dtype)


def _nsa_gate_sample(gexp, o_cmp, o_sel, o_win):
    t, w = o_cmp.shape
    row = pl.BlockSpec((t, w), lambda i: (0, 0))
    return pl.pallas_call(
        _gate_kernel,
        grid=(1,),
        in_specs=[pl.BlockSpec((3, t, w), lambda i: (0, 0, 0)), row, row, row],
        out_specs=row,
        out_shape=jax.ShapeDtypeStruct((t, w), BF16),
        compiler_params=_cparams(1),
    )(gexp, o_cmp, o_sel, o_win)


def _xattn_sample_kernel(q_ref, kv_ref, o_ref, *, bc, n_mem):
    r = q_ref.shape[1]
    rows = 2 * X_HEADS * n_mem
    is_key = (_iota((X_HEADS * r, rows), 1) & (2 * X_HEADS - 1)) == (_iota((X_HEADS * r, rows), 0) >> (r.bit_length() - 1))
    for bi in range(bc):
        x = kv_ref[bi].astype(BF16)
        q = jnp.concatenate([q_ref[bi, :, h * X_DH:(h + 1) * X_DH] for h in range(X_HEADS)], axis=0)
        s = jnp.where(is_key, _dot_nt(q, x), NEG_INF)
        e = jnp.exp(s - jnp.max(s, axis=-1, keepdims=True))
        p = e / jnp.sum(e, axis=-1, keepdims=True)
        o = _dot(pltpu.roll(p, X_HEADS, 1).astype(BF16), x)
        for h in range(X_HEADS):
            o_ref[bi, :, h * X_DH:(h + 1) * X_DH] = o[h * r:(h + 1) * r].astype(o_ref.dtype)


def _xattn_sample(q, kv, *, n_mem):
    db, r, w = q.shape
    bc = _decode_batch(db)
    return pl.pallas_call(
        functools.partial(_xattn_sample_kernel, bc=bc, n_mem=n_mem),
        grid=(db // bc,),
        in_specs=[pl.BlockSpec((bc, r, w), lambda i: (i, 0, 0)),
                  pl.BlockSpec((bc,) + kv.shape[1:], lambda i: (i, 0, 0))],
        out_specs=pl.BlockSpec((bc, r, w), lambda i: (i, 0, 0)),
        out_shape=jax.ShapeDtypeStruct((db, r, w), BF16),
        compiler_params=_cparams(1),
    )(q, kv)


def _prep_weights(w_in, mla_w_uq, mla_w_uk, mla_w_uv, nsa_phi_k, nsa_phi_v):
    d = w_in.shape[0]
    z = lambda n: jnp.zeros((d, n), F32)
    o = 0
    cq, o = w_in[:, o:o + 384], o + 384
    ckv, o = w_in[:, o:o + 256], o + 256
    kr, o = w_in[:, o:o + 32], o + 32
    qn, o = w_in[:, o:o + 512], o + 512
    kvn, o = w_in[:, o:o + 768], o + 768
    gn, o = w_in[:, o:o + 24], o + 24
    mg = w_in[:, o:]
    qn_p = jnp.pad(qn.reshape(d, NSA_HEADS, NSA_DH), ((0, 0), (0, 0), (0, LANES - NSA_DH))).reshape(d, -1)
    w_in_p = jnp.concatenate([cq, ckv, z(64), kr, z(32), qn_p, gn[:, :12], z(116), gn[:, 12:], z(116), mg, kvn],
                             axis=1).astype(BF16)
    w_kv_t = jnp.transpose(kvn).astype(BF16)
    w_uq_p = jnp.pad(mla_w_uq, ((0, 0), (0, 0), (0, LANES - MLA_NOPE - MLA_ROPE))).reshape(MLA_Q_LORA, -1).astype(BF16)
    pad_t = lambda w: jnp.pad(jnp.transpose(w, (1, 2, 0)), ((0, 0), (0, LANES - w.shape[2]), (0, 0))).reshape(
        MLA_HEADS * LANES, MLA_KV_LORA).astype(BF16)
    w_uk_t, w_uv_t = pad_t(mla_w_uk), pad_t(mla_w_uv)
    rope_pass = np.zeros((LANES, 384), np.float32)
    rope_pass[MLA_NOPE:MLA_NOPE + MLA_ROPE, MLA_KV_LORA:MLA_KV_LORA + MLA_ROPE] = np.eye(MLA_ROPE)
    w_abs = jnp.pad(jnp.transpose(mla_w_uk, (1, 2, 0)), ((0, 0), (0, LANES - MLA_NOPE), (0, 384 - MLA_KV_LORA))) + rope_pass
    eye_h = np.eye(MLA_HEADS, dtype=np.float32)[:, None, :, None]
    w_ov = (eye_h * jnp.transpose(mla_w_uv, (1, 0, 2))[:, :, None, :]).reshape(MLA_HEADS * MLA_KV_LORA, MLA_HEADS * MLA_V)
    maps = jnp.stack([jnp.transpose(m) for g in range(NSA_G) for m in (nsa_phi_k[g], nsa_phi_v[g])])
    phi_t = (np.eye(4, dtype=np.float32)[:, None, :, None] * maps[:, :, None, :]).reshape(4 * NSA_DH, 4 * NSA_DH)
    return w_in_p, w_kv_t, w_uq_p, w_uk_t, w_uv_t, w_abs.astype(BF16), w_ov.astype(BF16), phi_t.astype(BF16)


def _rope_tables(pos):
    half = MLA_ROPE // 2
    inv = ROPE_BASE ** (-np.arange(half, dtype=np.float64) / half)
    ang = pos.astype(np.float64)[:, None] * inv[None, :]
    cos, sin = np.cos(ang).astype(np.float32), np.sin(ang).astype(np.float32)
    n = pos.shape[0]
    one, zero = np.ones((n, MLA_NOPE), np.float32), np.zeros((n, half), np.float32)
    tail = np.zeros((n, LANES - MLA_NOPE - MLA_ROPE), np.float32)
    rc = np.concatenate([one, cos, cos, tail], axis=1)
    rs1 = np.concatenate([0 * one, -sin, zero, tail], axis=1)
    rs2 = np.concatenate([0 * one, zero, sin, tail], axis=1)
    return rc, rs1, rs2


def _key_rows(pos):
    rows = np.arange(LANES)[:, None]
    blk = (pos >> 6)[None, :]
    t = np.where((rows == blk) & (rows < SEL_BLOCK), 1.0, 0.0)
    t = np.where(rows == A_BLK, blk, t)
    t = np.where(rows == A_OFF, (pos & (SEL_BLOCK - 1))[None, :], t)
    t = np.where((rows == A_ONE0) | (rows == A_ONE1), 1.0, t)
    return t.astype(BF16)


def _avg_matrix(n_keys, order):
    blk_of_key = np.tile(np.arange(n_keys) // CMP_BLOCK, 2)[:, None]
    return np.where(blk_of_key == order[None, :], 1.0 / CMP_BLOCK, 0.0).astype(BF16)


def _even_odd_order(n):
    return np.concatenate([np.arange(0, n, 2), np.arange(1, n, 2)])


def kernel(x_prompt, x_sample, mem_prompt, cache_mla_ckv, cache_mla_krope, cache_nsa_cmp_kv, cache_nsa_sel_kv,
           state_nsa_win_kv, cache_xattn_kv, page_table, ffn1_norm, ffn1_w_gate, ffn1_w_up, ffn1_w_down, mix_norm,
           w_in, mla_q_norm, mla_w_uq, mla_kv_norm, mla_w_uk, mla_w_uv, nsa_phi_k, nsa_phi_v, w_br_mla, w_br_nsa,
           w_out, xattn_norm, xattn_mem_norm, xattn_w_q, xattn_w_kv, xattn_w_o, ffn2_norm, ffn2_w_gate, ffn2_w_up,
           ffn2_w_down, final_norm):
    assert x_prompt.shape[2] == D_MODEL and ffn1_norm.shape[0] == 1
    b, s, d = x_prompt.shape
    db, nq, _ = x_sample.shape
    n_pages = page_table.shape[1]
    past = n_pages * PAGE
    n_mem = mem_prompt.shape[1]
    assert s % QTILE == 0 and past % SEL_BLOCK == 0 and nq <= 8 and NSA_G == 2
    g2 = lambda a: a.reshape(1, -1)
    bf = lambda a: a.astype(BF16)

    w_in_p, w_kv_t, w_uq_p, w_uk_t, w_uv_t, w_abs, w_ov, phi_t = _prep_weights(
        w_in[0], mla_w_uq[0], mla_w_uk[0], mla_w_uv[0], nsa_phi_k[0], nsa_phi_v[0])
    f1 = (g2(ffn1_norm[0]), bf(ffn1_w_gate[0]), bf(ffn1_w_up[0]), bf(ffn1_w_down[0]))
    f2 = (g2(ffn2_norm[0]), bf(ffn2_w_gate[0]), bf(ffn2_w_up[0]), bf(ffn2_w_down[0]))
    fg = g2(final_norm)
    w_a, w_b, w_o = bf(w_br_mla[0]), bf(w_br_nsa[0]), bf(w_out[0])
    w_xq, w_xo = bf(xattn_w_q[0]), bf(xattn_w_o[0])
    w_xkv = bf(xattn_w_kv[0].reshape(d, -1))
    mixg, qng, kvg = g2(mix_norm[0]), g2(mla_q_norm[0]), g2(mla_kv_norm[0])

    t = b * s
    nt = s // TILE
    hp = _ffn(x_prompt.reshape(t, d), *f1, fg, final=False)
    (q_mla, k_t, v_t, p_ckv, p_kr_t, q_nsa, p_cmp_t, p_sel_t, p_win_t, sk, sv, wk, wv, gates, merge) = (
        _mixer_project_prompt(hp, b, s, mixg, w_in_p[:, :C_KVN], w_kv_t, qng, w_uq_p, kvg, w_uk_t, w_uv_t,
                              *_rope_tables(np.arange(s))))
    o_a = _mla_prompt(q_mla.reshape(b, s, -1), k_t, v_t).reshape(t, -1)

    nc = s // CMP_BLOCK
    order = _even_odd_order(nc)
    ckcv_t = _compress_prompt(p_cmp_t, _avg_matrix(s, order), phi_t)
    ctab = _key_rows(order * CMP_BLOCK + (CMP_BLOCK - 1))
    ktab = np.transpose(_key_rows(np.arange(s)).reshape(LANES, nt, TILE), (1, 0, 2))
    o_b = _nsa_prompt(q_nsa.reshape(b, s, -1), ckcv_t, ctab, sk, sv, wk, wv, ktab, gates.reshape(b, s, -1)).reshape(t, -1)
    hp = _merge(o_a, o_b, merge, hp, w_a, w_b, w_o)

    kv_mem = _norm_matmul(mem_prompt.reshape(b * n_mem, d), g2(xattn_mem_norm[0]), w_xkv)
    xq = _norm_matmul(hp, g2(xattn_norm[0]), w_xq, scale=X_SCALE, out_dtype=BF16)
    xo = _xattn_prompt(xq.reshape(b, s, -1), bf(kv_mem).reshape(b, n_mem, -1))
    hp = _matmul_residual(xo.reshape(t, -1), w_xo, hp)
    y_prompt = _ffn(hp, *f2, fg, final=True).reshape(b, s, d)

    wlen = min(WINDOW, s)
    kv_out = lambda a: jnp.transpose(a.reshape(1, b, NSA_G, 2, NSA_DH, a.shape[-1]), (0, 1, 5, 2, 3, 4))
    prompt_caches = (p_ckv.reshape(1, b, s, -1), jnp.transpose(p_kr_t, (0, 2, 1))[None], kv_out(p_cmp_t),
                     kv_out(p_sel_t), kv_out(p_win_t[:, :, s - wlen:]),
                     kv_mem.reshape(1, b, n_mem, 2, X_HEADS, X_DH))

    ts = db * nq
    hs = _ffn(x_sample.reshape(ts, d), *f1, fg, final=False)
    pos_s = np.tile(past + np.arange(nq), db)
    (q_mla, s_ckv, s_krb, q_nsa, s_cmp, s_sel, s_win, gates, merge) = _mixer_project_sample(
        hs, mixg, w_in_p, qng, w_uq_p, kvg, *_rope_tables(pos_s))
    s_kr = s_krb[:, MLA_NOPE:MLA_NOPE + MLA_ROPE]
    pad_rows = lambda a: jnp.pad(a.reshape(db, nq, -1), ((0, 0), (0, 8 - nq), (0, 0)))
    pt_flat = jnp.transpose(page_table).reshape(-1)

    qa = _mla_absorb_q(q_mla, w_abs)
    qa = jnp.transpose(qa.reshape(MLA_HEADS, db, nq, -1), (1, 0, 2, 3)).reshape(db, MLA_HEADS * nq, -1)
    kr_pool_t = jnp.transpose(cache_mla_krope[0], (0, 2, 1))
    o_lat = _mla_sample(pt_flat, db, n_pages, qa, pad_rows(s_ckv), pad_rows(s_kr), cache_mla_ckv[0], kr_pool_t, nq=nq)
    o_lat = jnp.transpose(o_lat.reshape(db, MLA_HEADS, nq, -1), (0, 2, 1, 3)).reshape(ts, -1)
    o_a = _matmul(bf(o_lat), w_ov, out_dtype=BF16)

    fm_pool = lambda c: jnp.transpose(c[0], (0, 2, 3, 4, 1)).reshape(c.shape[1], -1, PAGE)
    cmp_pool_t, sel_pool_t = fm_pool(cache_nsa_cmp_kv), fm_pool(cache_nsa_sel_kv)
    keys_per_dot = min(past, 32 * PAGE)
    avg2_s = _avg_matrix(keys_per_dot, np.arange(keys_per_dot // CMP_BLOCK))
    ckcv_s = _compress_sample(pt_flat, db, n_pages, cmp_pool_t, avg2_s, jnp.transpose(phi_t))
    ncs = past // CMP_BLOCK
    ctab_s = np.transpose(_key_rows(np.arange(ncs) * CMP_BLOCK + (CMP_BLOCK - 1)))
    qn = jnp.transpose(q_nsa.reshape(db, nq, NSA_G, NSA_HPG, LANES), (0, 2, 3, 1, 4))
    q_rows = qn.reshape(db, NSA_G, NSA_HPG * nq, LANES)
    o_cmp, imp = _nsa_cmp_sample(q_rows, ckcv_s, ctab_s, nq=nq, past=past)
    n_sel = -(-(past + nq) // SEL_BLOCK)
    n_pick = min(SEL_TOPK, n_sel) - 2
    ids = _topk_sample(imp[:, :, :nq].reshape(db * NSA_G * nq, -1), n_pick=n_pick, last_blk=n_sel - 1)
    o_sel = _nsa_sel_sample(pt_flat, ids[:, :SEL_TOPK], db, q_rows.reshape(db * NSA_G, -1, LANES), pad_rows(s_sel),
                            sel_pool_t, nblk=n_pick + 1, nq=nq, past=past).reshape(db, NSA_G, -1, LANES)
    wbuf = state_nsa_win_kv.shape[2]
    state_t = jnp.transpose(state_nsa_win_kv[0], (0, 2, 3, 4, 1)).reshape(db, -1, wbuf)
    wtab = _key_rows(past - wbuf + np.arange(wbuf))
    o_win = _nsa_win_sample(q_rows, state_t, wtab, pad_rows(s_win), nq=nq, past=past)
    to_tok = lambda a: jnp.transpose(a[..., NSA_DH:].reshape(db, NSA_G, NSA_HPG, nq, NSA_DH), (0, 3, 1, 2, 4)).reshape(ts, -1)
    gts =jnp.stack([gates[:, :12], gates[:, LANES:LANES + 12]], axis=1).reshape(ts, NSA_HEADS, 3)
    gexp = jnp.transpose(jnp.broadcast_to(gts[:, :, :, None], (ts, NSA_HEADS, 3, NSA_DH)), (2, 0, 1, 3)).reshape(3, ts, -1)
    o_b = _nsa_gate_sample(gexp, to_tok(o_cmp), to_tok(o_sel), to_tok(o_win))
    hs = _merge(o_a, o_b, merge, hs, w_a, w_b, w_o)

    xq = _norm_matmul(hs, g2(xattn_norm[0]), w_xq, scale=X_SCALE, out_dtype=BF16)
    xo = _xattn_sample(pad_rows(xq), cache_xattn_kv[0].reshape(db, -1, X_DH), n_mem=n_mem)
    hs = _matmul_residual(xo[:, :nq].reshape(ts, -1), w_xo, hs)
    y_sample = _ffn(hs, *f2, fg, final=True).reshape(db, nq, d)

    kvs = (1, db, nq, NSA_G, 2, NSA_DH)
    win_upd = jnp.concatenate([state_nsa_win_kv[0], s_win.reshape(kvs[1:])], axis=1)[:, nq:]
    sample_caches = (s_ckv.reshape(1, db, nq, -1), s_kr.reshape(1, db, nq, -1), s_cmp.reshape(kvs),
                     s_sel.reshape(kvs), win_upd[None])
    return (y_prompt, y_sample) + prompt_caches + sample_caches
```

```python
import functools
import math

import jax
import jax.numpy as jnp
import numpy as np
from jax import lax
from jax.experimental import pallas as pl
from jax.experimental.pallas import tpu as pltpu

F32 = jnp.float32
BF16 = jnp.bfloat16
I32 = jnp.int32

D_MODEL = 1024
EPS = 1e-6
PAGE = 128
MLA_HEADS = 8
MLA_Q_LORA = 384
MLA_KV_LORA = 256
MLA_NOPE = 64
MLA_ROPE = 32
MLA_V = 64
MLA_SCALE = (MLA_NOPE + MLA_ROPE) ** -0.5
ROPE_BASE = 10000.0
NSA_HEADS = 8
NSA_G = 2
NSA_HPG = 4
NSA_DH = 64
NSA_SCALE = NSA_DH ** -0.5
CMP_BLOCK = 32
SEL_BLOCK = 64
SEL_TOPK = 16
WINDOW = 512
FORCE_BONUS = 4.0 * NSA_HPG
X_HEADS = 4
X_DH = 128
X_SCALE = X_DH ** -0.5
LANES = 128
LOG2E = math.log2(math.e)
NEG_INF = float("-inf")
MASK_BIG = 2.0 ** 60
VMEM_LIMIT = 56 * 1024 * 1024
TILE = 256
QTILE = 512

C_CQ = 0
C_CKV = 384
C_KR = 640
C_QN = 768
C_GT = 1792
C_MG = 2048
C_KVN = 4096
C_END = 4864
A_BLK, A_OFF, A_ONE0, A_ONE1 = 64, 65, 66, 67


def _cparams(n_axes):
    return pltpu.CompilerParams(dimension_semantics=("arbitrary",) * n_axes, vmem_limit_bytes=VMEM_LIMIT)


def _dot(a, b):
    return jnp.dot(a, b, preferred_element_type=F32)


def _dot_nt(a, b):
    return lax.dot_general(a, b, (((1,), (1,)), ((), ())), preferred_element_type=F32)


def _rms(x, g):
    ms = jnp.mean(x * x, axis=-1, keepdims=True)
    return x * lax.rsqrt(ms + EPS) * g


def _iota(shape, dim):
    return lax.broadcasted_iota(I32, shape, dim)


def _pow2_neg(e):
    return lax.bitcast_convert_type((127 - e) << 23, F32)


def _query_aug(slope, qpos, lane):
    s64 = slope * float(SEL_BLOCK)
    return jnp.where(lane == A_BLK, s64,
           jnp.where(lane == A_OFF, slope,
           jnp.where(lane == A_ONE0, -s64 * (qpos >> 6).astype(F32),
           jnp.where(lane == A_ONE1, -slope * (qpos & (SEL_BLOCK - 1)).astype(F32), 0.0))))


def _flash_step(s, v_t, m_ref, acc_ref, idx, exp_fn):
    m_prev = m_ref[idx]
    m_new = jnp.maximum(m_prev, jnp.max(s, axis=-1, keepdims=True))
    alpha = exp_fn(m_prev - m_new)
    p = exp_fn(s - jnp.concatenate([m_new] * (s.shape[1] // LANES), axis=1))
    acc_ref[idx] = alpha * acc_ref[idx] + _dot_nt(p.astype(BF16), v_t)
    m_ref[idx] = m_new


def _ffn_kernel(x_ref, g_ref, wg_ref, wu_ref, wd_ref, fg_ref, o_ref, h_ref, acc_ref, *, final):
    j = pl.program_id(1)

    @pl.when(j == 0)
    def _():
        h_ref[...] = _rms(x_ref[...], g_ref[...]).astype(BF16)
        acc_ref[...] = jnp.zeros(acc_ref.shape, F32)

    h = h_ref[...]
    a = _dot(h, wg_ref[...])
    u = _dot(h, wu_ref[...])
    act = (a * jax.nn.sigmoid(a)) * u
    acc_ref[...] += _dot(act.astype(BF16), wd_ref[...])

    @pl.when(j == pl.num_programs(1) - 1)
    def _():
        y = x_ref[...] + 0.5 * acc_ref[...]
        if final:
            y = _rms(y, fg_ref[...])
        o_ref[...] = y


def _ffn(x, g, wg, wu, wd, fg, *, final):
    t, d = x.shape
    ff = wg.shape[1]
    tm = min(t, 1024)
    tf = 256
    return pl.pallas_call(
        functools.partial(_ffn_kernel, final=final),
        grid=(t // tm, ff // tf),
        in_specs=[
            pl.BlockSpec((tm, d), lambda i, j: (i, 0)),
            pl.BlockSpec((1, d), lambda i, j: (0, 0)),
            pl.BlockSpec((d, tf), lambda i, j: (0, j)),
            pl.BlockSpec((d, tf), lambda i, j: (0, j)),
            pl.BlockSpec((tf, d), lambda i, j: (j, 0)),
            pl.BlockSpec((1, d), lambda i, j: (0, 0)),
        ],
        out_specs=pl.BlockSpec((tm, d), lambda i, j: (i, 0)),
        out_shape=jax.ShapeDtypeStruct((t, d), F32),
        scratch_shapes=[pltpu.VMEM((tm, d), BF16), pltpu.VMEM((tm, d), F32)],
        compiler_params=_cparams(2),
    )(x, g, wg, wu, wd, fg)


def _proj_common(x_ref, g_ref, win_ref, qn_ref, wuq_ref, kvn_ref, rc_ref, rs1_ref, rs2_ref,
                 qmla_ref, ckv_ref, qnsa_ref, gates_ref, merge_ref):
    n = _rms(x_ref[...], g_ref[...]).astype(BF16)

    def seg(a, b):
        return _dot(n, win_ref[:, a:b])

    rc, rs1, rs2 = rc_ref[...], rs1_ref[...], rs2_ref[...]

    def rope(blk):
        return blk * rc + pltpu.roll(blk, 112, 1) * rs1 + pltpu.roll(blk, 16, 1) * rs2

    cq = _rms(seg(C_CQ, C_CKV), qn_ref[...]).astype(BF16)
    q = _dot(cq, wuq_ref[...]) * (MLA_SCALE * LOG2E)
    for h in range(MLA_HEADS):
        qmla_ref[:, h * LANES:(h + 1) * LANES] = rope(q[:, h * LANES:(h + 1) * LANES]).astype(BF16)

    c_kv = _rms(seg(C_CKV, C_KR), kvn_ref[...])
    ckv_ref[...] = c_kv
    krb = rope(seg(C_KR, C_QN))
    qnsa_ref[...] = (seg(C_QN, C_GT) * NSA_SCALE).astype(BF16)
    gates_ref[...] = jax.nn.sigmoid(seg(C_GT, C_MG))
    merge_ref[...] = jax.nn.sigmoid(seg(C_MG, C_KVN))
    return n, seg, c_kv, krb


def _proj_prompt_kernel(x_ref, g_ref, win_ref, wkv_t_ref, qn_ref, wuq_ref, kvn_ref, wuk_t_ref, wuv_t_ref,
                        rc_ref, rs1_ref, rs2_ref,
                        qmla_ref, k_t_ref, v_t_ref, ckv_ref, kr_t_ref, qnsa_ref, cmp_t_ref, sel_t_ref, win_t_ref,
                        sk_ref, sv_ref, wk_ref, wv_ref, gates_ref, merge_ref):
    n, _, c_kv, krb = _proj_common(x_ref, g_ref, win_ref, qn_ref, wuq_ref, kvn_ref, rc_ref, rs1_ref, rs2_ref,
                                   qmla_ref, ckv_ref, qnsa_ref, gates_ref, merge_ref)
    tm = n.shape[0]
    cb = c_kv.astype(BF16)
    kr_t = jnp.transpose(krb)
    kr_t_ref[0] = kr_t[MLA_NOPE:MLA_NOPE + MLA_ROPE]
    rows = _iota((LANES, tm), 0)
    k_t = _dot_nt(wuk_t_ref[...], cb)
    v_t = _dot_nt(wuv_t_ref[...], cb)
    for h in range(MLA_HEADS):
        hs = slice(h * LANES, (h + 1) * LANES)
        k_t_ref[0, h, 0] = (k_t[hs] + kr_t).astype(BF16)
        v_t_ref[0, h, 0] = jnp.where(rows >= MLA_V, 1.0, v_t[hs]).astype(BF16)
    kv_t = _dot_nt(wkv_t_ref[...], n)
    cmp_t_ref[0] = kv_t[0:256]
    sel_t_ref[0] = kv_t[256:512]
    win_t_ref[0] = kv_t[512:768]
    for g in range(NSA_G):
        for base, k_out, v_out in ((256, sk_ref, sv_ref), (512, wk_ref, wv_ref)):
            blk = kv_t[base + g * LANES:base + (g + 1) * LANES]
            k_out[0, g, 0] = blk.astype(BF16)
            v_out[0, g, 0] = jnp.where(rows < NSA_DH, 1.0, blk).astype(BF16)


def _proj_sample_kernel(x_ref, g_ref, win_ref, qn_ref, wuq_ref, kvn_ref, rc_ref, rs1_ref, rs2_ref,
                        qmla_ref, ckv_ref, kr_ref, qnsa_ref, cmp_ref, sel_ref, win_o_ref, gates_ref, merge_ref):
    _, seg, _, krb = _proj_common(x_ref, g_ref, win_ref, qn_ref, wuq_ref, kvn_ref, rc_ref, rs1_ref, rs2_ref,
                                  qmla_ref, ckv_ref, qnsa_ref, gates_ref, merge_ref)
    kr_ref[...] = krb
    cmp_ref[...] = seg(C_KVN, C_KVN + 256)
    sel_ref[...] = seg(C_KVN + 256, C_KVN + 512)
    win_o_ref[...] = seg(C_KVN + 512, C_END)


def _mixer_project_prompt(x, b, s, g, w_in_p, w_kv_t, q_norm, w_uq_p, kv_norm, w_uk_t, w_uv_t, rc, rs1, rs2):
    t, d = x.shape
    tm = TILE
    nt = s // tm
    full = lambda a: pl.BlockSpec(a.shape, lambda i: (0,) * a.ndim)
    row = lambda w: pl.BlockSpec((tm, w), lambda i: (i, 0))
    tab = pl.BlockSpec((tm, LANES), lambda i: (i % nt, 0))
    fm = lambda r: pl.BlockSpec((1, r, tm), lambda i: (i // nt, 0, i % nt))
    tiles = lambda h: pl.BlockSpec((1, h, 1, LANES, tm), lambda i: (i // nt, 0, i % nt, 0, 0))
    tok = lambda w, dt: jax.ShapeDtypeStruct((t, w), dt)
    fms = lambda r: jax.ShapeDtypeStruct((b, r, s), F32)
    til = lambda h: jax.ShapeDtypeStruct((b, h, nt, LANES, tm), BF16)
    outs = [(row(1024), tok(1024, BF16)), (tiles(MLA_HEADS), til(MLA_HEADS)), (tiles(MLA_HEADS), til(MLA_HEADS)),
            (row(256), tok(256, F32)), (fm(MLA_ROPE), fms(MLA_ROPE)), (row(1024), tok(1024, BF16)),
            (fm(256), fms(256)), (fm(256), fms(256)), (fm(256), fms(256)),
            (tiles(NSA_G), til(NSA_G)), (tiles(NSA_G), til(NSA_G)), (tiles(NSA_G), til(NSA_G)), (tiles(NSA_G), til(NSA_G)),
            (row(256), tok(256, F32)), (row(2048), tok(2048, F32))]
    ins = (x, g, w_in_p, w_kv_t, q_norm, w_uq_p, kv_norm, w_uk_t, w_uv_t)
    return pl.pallas_call(
        _proj_prompt_kernel,
        grid=(t // tm,),
        in_specs=[row(d)] + [full(a) for a in ins[1:]] + [tab, tab, tab],
        out_specs=[o[0] for o in outs],
        out_shape=[o[1] for o in outs],
        compiler_params=_cparams(1),
    )(*ins, rc, rs1, rs2)


def _mixer_project_sample(x, g, w_in_p, q_norm, w_uq_p, kv_norm, rc, rs1, rs2):
    t, d = x.shape
    tm = min(t, TILE)
    full = lambda a: pl.BlockSpec(a.shape, lambda i: (0,) * a.ndim)
    row = lambda w: pl.BlockSpec((tm, w), lambda i: (i, 0))
    widths = [(1024, BF16), (256, F32), (128, F32), (1024, BF16), (256, F32), (256, F32), (256, F32), (256, F32),
              (2048, F32)]
    ins = (x, g, w_in_p, q_norm, w_uq_p, kv_norm)
    return pl.pallas_call(
        _proj_sample_kernel,
        grid=(t // tm,),
        in_specs=[row(d)] + [full(a) for a in ins[1:]] + [row(LANES)] * 3,
        out_specs=[row(w) for w, _ in widths],
        out_shape=[jax.ShapeDtypeStruct((t, w), dt) for w, dt in widths],
        compiler_params=_cparams(1),
    )(*ins, rc, rs1, rs2)


def _norm_mm_kernel(x_ref, g_ref, w_ref, o_ref, *, scale):
    y = _dot(_rms(x_ref[...], g_ref[...]).astype(BF16), w_ref[...])
    if scale != 1.0:
        y = y * scale
    o_ref[...] = y.astype(o_ref.dtype)


def _norm_matmul(x, g, w, *, scale=1.0, out_dtype=F32):
    t, d = x.shape
    n = w.shape[1]
    tm = min(t, 512)
    return pl.pallas_call(
        functools.partial(_norm_mm_kernel, scale=scale),
        grid=(t // tm,),
        in_specs=[pl.BlockSpec((tm, d), lambda i: (i, 0)), pl.BlockSpec((1, d), lambda i: (0, 0)),
                  pl.BlockSpec((d, n), lambda i: (0, 0))],
        out_specs=pl.BlockSpec((tm, n), lambda i: (i, 0)),
        out_shape=jax.ShapeDtypeStruct((t, n), out_dtype),
        compiler_params=_cparams(1),
    )(x, g, w)


def _mm_kernel(a_ref, w_ref, o_ref):
    o_ref[...] = _dot(a_ref[...], w_ref[...]).astype(o_ref.dtype)


def _matmul(a, w, *, out_dtype=F32):
    t, k = a.shape
    n = w.shape[1]
    tm = min(t, 512)
    return pl.pallas_call(
        _mm_kernel,
        grid=(t // tm,),
        in_specs=[pl.BlockSpec((tm, k), lambda i: (i, 0)), pl.BlockSpec((k, n), lambda i: (0, 0))],
        out_specs=pl.BlockSpec((tm, n), lambda i: (i, 0)),
        out_shape=jax.ShapeDtypeStruct((t, n), out_dtype),
        compiler_params=_cparams(1),
    )(a, w)


def _mm_res_kernel(a_ref, w_ref, x_ref, o_ref):
    o_ref[...] = x_ref[...] + _dot(a_ref[...], w_ref[...])


def _matmul_residual(a, w, x):
    t, k = a.shape
    n = w.shape[1]
    tm = min(t, 512)
    return pl.pallas_call(
        _mm_res_kernel,
        grid=(t // tm,),
        in_specs=[pl.BlockSpec((tm, k), lambda i: (i, 0)), pl.BlockSpec((k, n), lambda i: (0, 0)),
                  pl.BlockSpec((tm, n), lambda i: (i, 0))],
        out_specs=pl.BlockSpec((tm, n), lambda i: (i, 0)),
        out_shape=jax.ShapeDtypeStruct((t, n), F32),
        compiler_params=_cparams(1),
    )(a, w, x)


def _merge_kernel(oa_ref, ob_ref, mg_ref, x_ref, wa_ref, wb_ref, wo_ref, o_ref):
    d = x_ref.shape[1]
    mix = mg_ref[:, :d] * _dot(oa_ref[...], wa_ref[...]) + mg_ref[:, d:] * _dot(ob_ref[...], wb_ref[...])
    o_ref[...] = x_ref[...] + _dot(mix.astype(BF16), wo_ref[...])


def _merge(o_a, o_b, mg, x, w_a, w_b, w_o):
    t, d = x.shape
    tm = min(t, 512)
    row = lambda w: pl.BlockSpec((tm, w), lambda i: (i, 0))
    full = lambda a: pl.BlockSpec(a.shape, lambda i: (0, 0))
    return pl.pallas_call(
        _merge_kernel,
        grid=(t // tm,),
        in_specs=[row(o_a.shape[1]), row(o_b.shape[1]), row(2 * d), row(d), full(w_a), full(w_b), full(w_o)],
        out_specs=row(d),
        out_shape=jax.ShapeDtypeStruct((t, d), F32),
        compiler_params=_cparams(1),
    )(o_a, o_b, mg, x, w_a, w_b, w_o)


MLA_HPS = 8


def _mla_prompt_kernel(q_ref, k_ref, v_ref, o_ref, m_ref, acc_ref, *, tq, tk):
    i = pl.program_id(2)
    q = q_ref[0]
    qs = [q[:, h * LANES:(h + 1) * LANES] for h in range(MLA_HPS)]
    per_q = tq // tk
    m_ref[...] = jnp.full(m_ref.shape, NEG_INF, F32)
    acc_ref[...] = jnp.zeros(acc_ref.shape, F32)

    def tile(j, causal):
        if causal:
            ok = (j * tk + _iota((tq, tk), 1)) <= (i * tq + _iota((tq, tk), 0))
        for h in range(MLA_HPS):
            s = _dot(qs[h], k_ref[0, h, j])
            if causal:
                s = jnp.where(ok, s, -MASK_BIG)
            _flash_step(s, v_ref[0, h, j], m_ref, acc_ref, h, jnp.exp2)

    def body(j, c):
        tile(j, False)
        return c

    lax.fori_loop(0, per_q * i, body, 0)
    for d in range(per_q):
        tile(per_q * i + d, True)
    lane = _iota((1, LANES), 1)
    outs = []
    for h in range(MLA_HPS):
        a = acc_ref[h]
        outs.append(a / pltpu.roll(a, MLA_V, 1))
    for p in range(MLA_HPS // 2):
        pair = jnp.where(lane < MLA_V, outs[2 * p], pltpu.roll(outs[2 * p + 1], MLA_V, 1))
        o_ref[0, :, p * LANES:(p + 1) * LANES] = pair.astype(o_ref.dtype)


def _mla_prompt(q, k_t, v_t):
    b, s, _ = q.shape
    tq = QTILE
    nt = s // TILE
    kv_spec = pl.BlockSpec((1, MLA_HPS, nt, LANES, TILE), lambda bi, hq, i: (bi, hq, 0, 0, 0))
    return pl.pallas_call(
        functools.partial(_mla_prompt_kernel, tq=tq, tk=TILE),
        grid=(b, MLA_HEADS // MLA_HPS, s // tq),
        in_specs=[pl.BlockSpec((1, tq, MLA_HPS * LANES), lambda bi, hq, i: (bi, i, hq)), kv_spec, kv_spec],
        out_specs=pl.BlockSpec((1, tq, MLA_HPS * MLA_V), lambda bi, hq, i: (bi, i, hq)),
        out_shape=jax.ShapeDtypeStruct((b, s, MLA_HEADS * MLA_V), BF16),
        scratch_shapes=[pltpu.VMEM((MLA_HPS, tq, LANES), F32), pltpu.VMEM((MLA_HPS, tq, LANES), F32)],
        compiler_params=_cparams(3),
    )(q, k_t, v_t)


def _split_hi_lo(x):
    hi = x.astype(BF16)
    return hi, (x - hi.astype(F32)).astype(BF16)


def _compress_prompt_kernel(x_ref, a_ref, phi_ref, o_ref):
    hi, lo = _split_hi_lo(x_ref[0])
    means = _dot(jnp.concatenate([hi, lo], axis=1), a_ref[...])
    o_ref[0] = _dot(phi_ref[...], means.astype(BF16)).astype(o_ref.dtype)


def _compress_prompt(cmp_t, avg2, phi_t):
    b, w, s = cmp_t.shape
    nc = avg2.shape[1]
    return pl.pallas_call(
        _compress_prompt_kernel,
        grid=(b,),
        in_specs=[pl.BlockSpec((1, w, s), lambda i: (i, 0, 0)), pl.BlockSpec(avg2.shape, lambda i: (0, 0)),
                  pl.BlockSpec(phi_t.shape, lambda i: (0, 0))],
        out_specs=pl.BlockSpec((1, w, nc), lambda i: (i, 0, 0)),
        out_shape=jax.ShapeDtypeStruct((b, w, nc), BF16),
        compiler_params=_cparams(1),
    )(cmp_t, avg2, phi_t)


def _nsa_prompt_kernel(q_ref, c_ref, ctab_ref, sk_ref, sv_ref, wk_ref, wv_ref, ktab_ref, gt_ref, o_ref,
                       m_ref, acc_ref, *, tq, tk, n_sel):
    g = pl.program_id(1)
    i = pl.program_id(2)
    r = NSA_HPG * tq
    shift = tq.bit_length() - 1
    row = _iota((r, LANES), 0)
    lane_r = _iota((r, LANES), 1)
    qpos = i * tq + (row & (tq - 1))
    slope = _pow2_neg(NSA_HPG * g + (row >> shift) + 1)
    aq = _query_aug(slope, qpos, lane_r)
    q_st = jnp.concatenate([q_ref[0, :, h * LANES:(h + 1) * LANES] for h in range(NSA_HPG)], axis=0)
    q_plain = jnp.concatenate([q_st, aq.astype(BF16)], axis=1)

    kc = c_ref[0]
    nc = kc.shape[1]
    half = nc // 2
    lane_c = _iota((1, nc), 1)
    cblk = jnp.where(lane_c < half, 2 * lane_c, 2 * (lane_c - half) + 1)
    cmask = (cblk * CMP_BLOCK + (CMP_BLOCK - 1)) <= (i * tq + (_iota((r, nc), 0) & (tq - 1)))
    s = _dot(q_plain, jnp.concatenate([kc, ctab_ref[...]], axis=0))
    s = jnp.where(cmask, s, NEG_INF)
    mx = jnp.max(s, axis=-1, keepdims=True)
    mx = jnp.where(mx == NEG_INF, 0.0, mx)
    e = jnp.where(cmask, jnp.exp(s - mx), 0.0)
    p = e / jnp.maximum(jnp.sum(e, axis=-1, keepdims=True), 1e-30)
    o_cmp = _dot_nt(p.astype(BF16), kc)

    psum = p[0:tq] + p[tq:2 * tq] + p[2 * tq:3 * tq] + p[3 * tq:4 * tq]
    imp = psum + pltpu.roll(psum, half, 1)
    qp = i * tq + _iota((tq, 1), 0)
    forced = (lane_c == 0) | (lane_c == (qp >> 6))
    score = jnp.where(lane_c * SEL_BLOCK <= qp, imp + jnp.where(forced, FORCE_BONUS, 0.0), -1.0)
    st = jnp.transpose(score)[0:n_sel]
    jidx = _iota((n_sel, 1), 0)
    rank = jnp.zeros((n_sel, tq), F32)
    for ii in range(n_sel):
        ri = st[ii:ii + 1, :]
        first = jnp.where(jidx > ii, 1.0, 0.0)
        rank = rank + jnp.where(ri > st, 1.0, jnp.where(ri == st, first, 0.0))
    valid_t = (jidx * SEL_BLOCK) <= (i * tq + _iota((1, tq), 1))
    mt = jnp.where(valid_t, jnp.where(rank < float(min(SEL_TOPK, n_sel)), 1.0, 0.0), 0.0)
    if n_sel < LANES:
        mt = jnp.concatenate([mt, jnp.zeros((LANES - n_sel, tq), F32)], axis=0)
    msel = jnp.transpose(mt)
    mst = jnp.concatenate([msel] * NSA_HPG, axis=0)
    q_sel = jnp.concatenate([q_st, jnp.where(lane_r < n_sel, (mst - 1.0) * MASK_BIG, aq).astype(BF16)], axis=1)

    lane = _iota((1, LANES), 1)
    m_ref[...] = jnp.full(m_ref.shape, NEG_INF, F32)
    acc_ref[...] = jnp.zeros(acc_ref.shape, F32)
    per_q = tq // tk
    rel = (_iota((r, tk), 0) & (tq - 1)) - _iota((r, tk), 1)

    def dist_to(j):
        return rel + (i * tq - j * tk)

    def sel_tile(j, causal):
        sc = _dot(q_sel, jnp.concatenate([sk_ref[0, 0, j], ktab_ref[j]], axis=0))
        if causal:
            sc = jnp.where(dist_to(j) >= 0, sc, -MASK_BIG)
        _flash_step(sc, sv_ref[0, 0, j], m_ref, acc_ref, 0, jnp.exp)

    def sel_body(j, c):
        sel_tile(j, False)
        return c

    lax.fori_loop(0, per_q * i, sel_body, 0)
    for dd in range(per_q):
        sel_tile(per_q * i + dd, True)

    def win_body(j, c):
        in_window = lax.bitcast_convert_type(dist_to(j), jnp.uint32) <= jnp.uint32(WINDOW)
        sc = _dot(q_plain, jnp.concatenate([wk_ref[0, 0, j], ktab_ref[j]], axis=0))
        _flash_step(jnp.where(in_window, sc, -MASK_BIG), wv_ref[0, 0, j], m_ref, acc_ref, 1, jnp.exp)
        return c

    lax.fori_loop(jnp.maximum(per_q * i - WINDOW // tk, 0), per_q * (i + 1), win_body, 0)

    gt = gt_ref[0]
    heads = []
    for h in range(NSA_HPG):
        rs = slice(h * tq, (h + 1) * tq)
        a_sel, a_win = acc_ref[0, rs], acc_ref[1, rs]
        heads.append(gt[:, 3 * h:3 * h + 1] * o_cmp[rs]
                     + (gt[:, 3 * h + 1:3 * h + 2] / a_sel[:, 0:1]) * a_sel
                     + (gt[:, 3 * h + 2:3 * h + 3] / a_win[:, 0:1]) * a_win)
    for p2 in range(NSA_HPG // 2):
        pair = jnp.where(lane < NSA_DH, pltpu.roll(heads[2 * p2], NSA_DH, 1), heads[2 * p2 + 1])
        o_ref[0, :, p2 * LANES:(p2 + 1) * LANES] = pair.astype(o_ref.dtype)


def _nsa_prompt(q, ckcv_t, ctab, sk, sv, wk, wv, ktab, gates):
    b, s, _ = q.shape
    tq = QTILE
    nt = s // TILE
    nc = ckcv_t.shape[2]
    r = NSA_HPG * tq
    kv = pl.BlockSpec((1, 1, nt, LANES, TILE), lambda bi, gi, i: (bi, gi, 0, 0, 0))
    return pl.pallas_call(
        functools.partial(_nsa_prompt_kernel, tq=tq, tk=TILE, n_sel=s // SEL_BLOCK),
        grid=(b, NSA_G, s // tq),
        in_specs=[pl.BlockSpec((1, tq, NSA_HPG * LANES), lambda bi, gi, i: (bi, i, gi)),
                  pl.BlockSpec((1, LANES, nc), lambda bi, gi, i: (bi, gi, 0)),
                  pl.BlockSpec(ctab.shape, lambda bi, gi, i: (0, 0)),
                  kv, kv, kv, kv,
                  pl.BlockSpec(ktab.shape, lambda bi, gi, i: (0, 0, 0)),
                  pl.BlockSpec((1, tq, LANES), lambda bi, gi, i: (bi, i, gi))],
        out_specs=pl.BlockSpec((1, tq, NSA_HPG * NSA_DH), lambda bi, gi, i: (bi, i, gi)),
        out_shape=jax.ShapeDtypeStruct((b, s, NSA_HEADS * NSA_DH), BF16),
        scratch_shapes=[pltpu.VMEM((2, r, LANES), F32), pltpu.VMEM((2, r, LANES), F32)],
        compiler_params=_cparams(3),
    )(q, ckcv_t, ctab, sk, sv, wk, wv, ktab, gates)


def _xattn_prompt_kernel(q_ref, kv_ref, o_ref):
    hw = X_HEADS * X_DH
    for h in range(X_HEADS):
        cs = slice(h * X_DH, (h + 1) * X_DH)
        s = _dot_nt(q_ref[0, :, cs], kv_ref[0, :, cs])
        e = jnp.exp(s - jnp.max(s, axis=-1, keepdims=True))
        p = e / jnp.sum(e, axis=-1, keepdims=True)
        o_ref[0, :, cs] = _dot(p.astype(BF16), kv_ref[0, :, hw + h * X_DH:hw + (h + 1) * X_DH]).astype(o_ref.dtype)


def _xattn_prompt(q, kv):
    b, s, w = q.shape
    m = kv.shape[1]
    tq = min(s, 512)
    return pl.pallas_call(
        _xattn_prompt_kernel,
        grid=(b, s // tq),
        in_specs=[pl.BlockSpec((1, tq, w), lambda bi, i: (bi, i, 0)),
                  pl.BlockSpec((1, m, 2 * w), lambda bi, i: (bi, 0, 0))],
        out_specs=pl.BlockSpec((1, tq, w), lambda bi, i: (bi, i, 0)),
        out_shape=jax.ShapeDtypeStruct((b, s, w), BF16),
        compiler_params=_cparams(2),
    )(q, kv)


def _qabs_kernel(q_ref, w_ref, o_ref):
    o_ref[0] = _dot(q_ref[...], w_ref[0]).astype(o_ref.dtype)


def _mla_absorb_q(q_mla, w_abs):
    t = q_mla.shape[0]
    n = w_abs.shape[2]
    return pl.pallas_call(
        _qabs_kernel,
        grid=(MLA_HEADS,),
        in_specs=[pl.BlockSpec((t, LANES), lambda h: (0, h)), pl.BlockSpec((1, LANES, n), lambda h: (h, 0, 0))],
        out_specs=pl.BlockSpec((1, t, n), lambda h: (h, 0, 0)),
        out_shape=jax.ShapeDtypeStruct((MLA_HEADS, t, n), BF16),
        compiler_params=_cparams(1),
    )(q_mla, w_abs)


def _double_buffered(step, n_steps, copies, per_page=1):
    def start_all(cps):
        for k, cp in enumerate(cps):
            cp.start(priority=(k // per_page) % 2)

    @pl.when(step == 0)
    def _():
        start_all(copies(step, 0, True))

    @pl.when(step + 1 < n_steps)
    def _():
        start_all(copies(step + 1, (step + 1) & 1, True))

    slot = step & 1
    for cp in copies(step, slot, False):
        cp.wait()
    return slot


def _pages_on_lanes(buf, slot, first, n):
    return jnp.concatenate([buf[slot, first + p] for p in range(n)], axis=1)


def _mla_sample_kernel(pt_ref, qa_ref, cn_ref, krn_ref, c_hbm, kr_hbm, o_ref, cbuf, krbuf, csem, krsem,
                       *, n_pages, db, nq):
    def copies(bb, slot, lookup):
        out = []
        for p in range(n_pages):
            page = pt_ref[p * db + bb] if lookup else 0
            out.append(pltpu.make_async_copy(c_hbm.at[page], cbuf.at[slot, pl.ds(p * PAGE, PAGE), :], csem.at[slot]))
            out.append(pltpu.make_async_copy(kr_hbm.at[page], krbuf.at[slot, p], krsem.at[slot]))
        return out

    slot = _double_buffered(pl.program_id(0), pl.num_programs(0), copies, per_page=2)
    qa = qa_ref[0]
    qc, qr = qa[:, :MLA_KV_LORA], qa[:, MLA_KV_LORA:MLA_KV_LORA + MLA_ROPE]
    c = cbuf[slot].astype(BF16)
    kr_t = _pages_on_lanes(krbuf, slot, 0, n_pages).astype(BF16)
    s1 = _dot_nt(qc, c) + _dot(qr, kr_t)
    cn = cn_ref[0].astype(BF16)
    s2 = _dot_nt(qc, cn) + _dot_nt(qr, krn_ref[0].astype(BF16))
    qi = _iota((qa.shape[0], 1), 0) & (nq - 1)
    s2 = jnp.where(_iota((1, cn.shape[0]), 1) <= qi, s2, NEG_INF)
    mx = jnp.maximum(jnp.max(s1, axis=-1, keepdims=True), jnp.max(s2, axis=-1, keepdims=True))
    e1 = jnp.exp2(s1 - mx)
    e2 = jnp.exp2(s2 - mx)
    den = jnp.sum(e1, axis=-1, keepdims=True) + jnp.sum(e2, axis=-1, keepdims=True)
    o_ref[0] = (_dot(e1.astype(BF16), c) + _dot(e2.astype(BF16), cn)) / den


def _mla_sample(pt_flat, db, n_pages, qa, c_new, kr_new, ckv_pool, kr_pool_t, *, nq):
    r = qa.shape[1]
    past = n_pages * PAGE
    per_b = lambda a: pl.BlockSpec((1,) + a.shape[1:], lambda b, pt: (b, 0, 0))
    hbm = pl.BlockSpec(memory_space=pl.ANY)
    grid_spec = pltpu.PrefetchScalarGridSpec(
        num_scalar_prefetch=1,
        grid=(db,),
        in_specs=[per_b(qa), per_b(c_new), per_b(kr_new), hbm, hbm],
        out_specs=pl.BlockSpec((1, r, MLA_KV_LORA), lambda b, pt: (b, 0, 0)),
        scratch_shapes=[pltpu.VMEM((2, past, MLA_KV_LORA), F32), pltpu.VMEM((2, n_pages, MLA_ROPE, PAGE), F32),
                        pltpu.SemaphoreType.DMA((2,)), pltpu.SemaphoreType.DMA((2,))],
    )
    return pl.pallas_call(
        functools.partial(_mla_sample_kernel, n_pages=n_pages, db=db, nq=nq),
        grid_spec=grid_spec,
        out_shape=jax.ShapeDtypeStruct((db, r, MLA_KV_LORA), F32),
        compiler_params=_cparams(1),
    )(pt_flat, qa, c_new, kr_new, ckv_pool, kr_pool_t)


def _cmp_sample_kernel(pt_ref, a_ref, phi_ref, pool_hbm, o_ref, buf, sem, *, n_pages, db):
    def copies(bb, slot, lookup):
        return [pltpu.make_async_copy(pool_hbm.at[pt_ref[p * db + bb] if lookup else 0], buf.at[slot, p], sem.at[slot])
                for p in range(n_pages)]

    slot = _double_buffered(pl.program_id(0), pl.num_programs(0), copies)
    pages = a_ref.shape[0] // (2 * PAGE)
    blocks = a_ref.shape[1]
    for c in range(n_pages // pages):
        hi, lo = _split_hi_lo(_pages_on_lanes(buf, slot, c * pages, pages))
        lhs = jnp.concatenate([hi, lo], axis=1)
        half = lhs.shape[0] // 2
        means = jnp.concatenate([_dot(lhs[:half], a_ref[...]), _dot(lhs[half:], a_ref[...])], axis=0)
        means = jnp.transpose(means).astype(BF16)
        o_ref[0, c * blocks:(c + 1) * blocks, :] = _dot(means, phi_ref[...]).astype(o_ref.dtype)


def _compress_sample(pt_flat, db, n_pages, cmp_pool_t, avg2, phi):
    w = cmp_pool_t.shape[1]
    past = n_pages * PAGE
    grid_spec = pltpu.PrefetchScalarGridSpec(
        num_scalar_prefetch=1,
        grid=(db,),
        in_specs=[pl.BlockSpec(avg2.shape, lambda b, pt: (0, 0)), pl.BlockSpec(phi.shape, lambda b, pt: (0, 0)),
                  pl.BlockSpec(memory_space=pl.ANY)],
        out_specs=pl.BlockSpec((1, past // CMP_BLOCK, w), lambda b, pt: (b, 0, 0)),
        scratch_shapes=[pltpu.VMEM((2, n_pages, w, PAGE), F32), pltpu.SemaphoreType.DMA((2,))],
    )
    return pl.pallas_call(
        functools.partial(_cmp_sample_kernel, n_pages=n_pages, db=db),
        grid_spec=grid_spec,
        out_shape=jax.ShapeDtypeStruct((db, past // CMP_BLOCK, w), BF16),
        compiler_params=_cparams(1),
    )(pt_flat, avg2, phi, cmp_pool_t)


def _nsa_cmp_sample_kernel(q_ref, c_ref, ctab_ref, o_ref, imp_ref, *, nq, past):
    r = q_ref.shape[2]
    nc = c_ref.shape[1]
    row = _iota((r, LANES), 0)
    lane_r = _iota((r, LANES), 1)
    qpos = past + (row & (nq - 1))
    cmask = (_iota((1, nc), 1) * CMP_BLOCK + (CMP_BLOCK - 1)) <= (past + (_iota((r, nc), 0) & (nq - 1)))
    for bi, g in _batch_groups(q_ref.shape[0]):
        slope = _pow2_neg(NSA_HPG * g + (row >> (nq.bit_length() - 1)) + 1)
        q2 = jnp.concatenate([q_ref[bi, g], _query_aug(slope, qpos, lane_r).astype(BF16)], axis=1)
        kc = c_ref[bi, :, g * LANES:(g + 1) * LANES]
        s = _dot_nt(q2, jnp.concatenate([kc, ctab_ref[...]], axis=1))
        s = jnp.where(cmask, s, NEG_INF)
        mx = jnp.max(s, axis=-1, keepdims=True)
        mx = jnp.where(mx == NEG_INF, 0.0, mx)
        e = jnp.where(cmask, jnp.exp(s - mx), 0.0)
        p = e / jnp.maximum(jnp.sum(e, axis=-1, keepdims=True), 1e-30)
        o_ref[bi, g] = _dot(p.astype(BF16), kc)
        ps = p
        for h in range(1, NSA_HPG):
            ps = ps + pltpu.roll(p, h * nq, 0)
        imp_ref[bi, g] = ps + pltpu.roll(ps, nc - 1, 1)


def _batch_groups(bc):
    return [(bi, g) for bi in range(bc) for g in range(NSA_G)]


def _decode_batch(db):
    return 4 if db % 4 == 0 else 1


def _nsa_cmp_sample(q, ckcv, ctab, *, nq, past):
    db, g, r, _ = q.shape
    nc = ckcv.shape[1]
    bc = _decode_batch(db)
    return pl.pallas_call(
        functools.partial(_nsa_cmp_sample_kernel, nq=nq, past=past),
        grid=(db // bc,),
        in_specs=[pl.BlockSpec((bc,) + q.shape[1:], lambda b: (b, 0, 0, 0)),
                  pl.BlockSpec((bc,) + ckcv.shape[1:], lambda b: (b, 0, 0)),
                  pl.BlockSpec(ctab.shape, lambda b: (0, 0))],
        out_specs=[pl.BlockSpec((bc, g, r, LANES), lambda b: (b, 0, 0, 0)),
                   pl.BlockSpec((bc, g, r, nc), lambda b: (b, 0, 0, 0))],
        out_shape=[jax.ShapeDtypeStruct((db, g, r, LANES), F32), jax.ShapeDtypeStruct((db, g, r, nc), F32)],
        compiler_params=_cparams(1),
    )(q, ckcv, ctab)


def _topk_sample_kernel(imp_ref, o_ref, *, n_pick, last_blk):
    s = imp_ref[...]
    lane_i = _iota(s.shape, 1)
    lane = lane_i.astype(F32)
    s = jnp.where(((lane_i & 1) == 0) & (lane_i > 0), s, -1.0)
    olane = _iota(o_ref.shape, 1)
    out = jnp.where(olane == n_pick + 1, float(last_blk), 0.0)
    for it in range(n_pick):
        mx = jnp.max(s, axis=-1, keepdims=True)
        idx = jnp.min(jnp.where(s == mx, lane, float(s.shape[1])), axis=-1, keepdims=True)
        s = jnp.where(lane == idx, -1.0, s)
        out = jnp.where(olane == it, idx * 0.5, out)
    o_ref[...] = out.astype(I32)


def _topk_sample(imp, *, n_pick, last_blk):
    rows = imp.shape[0]
    return pl.pallas_call(
        functools.partial(_topk_sample_kernel, n_pick=n_pick, last_blk=last_blk),
        grid=(1,),
        in_specs=[pl.BlockSpec(imp.shape, lambda i: (0, 0))],
        out_specs=pl.BlockSpec((rows, LANES), lambda i: (0, 0)),
        out_shape=jax.ShapeDtypeStruct((rows, LANES), I32),
        compiler_params=_cparams(1),
    )(imp)


def _nsa_sel_sample_kernel(pt_ref, ids_ref, q_ref, new_ref, pool_hbm, o_ref, buf, sem, *, nblk, nq, db, past):
    def block_id(t, qq, c):
        return ids_ref[(t * nq + qq) * SEL_TOPK + c]

    def copies(t, slot, lookup):
        b = t >> 1
        rows = pl.ds(pl.multiple_of((t & 1) * LANES, LANES), LANES)
        out = []
        for qq in range(nq):
            for c in range(nblk if qq == 0 else nblk - 1):
                page = pt_ref[(block_id(t, qq, c) >> 1) * db + b] if lookup else 0
                out.append(pltpu.make_async_copy(pool_hbm.at[page, rows, :], buf.at[slot, qq * nblk + c], sem.at[slot]))
        return out

    t = pl.program_id(0)
    slot = _double_buffered(t, pl.num_programs(0), copies)
    g = t & 1
    q = q_ref[0]
    r = q.shape[0]
    pages = [buf[slot, (0 if c == nblk - 1 else qq) * nblk + c] for qq in range(nq) for c in range(nblk)]
    kv_t = jnp.concatenate(pages, axis=1).astype(BF16)
    lane = _iota((1, PAGE), 1)
    kpos, keep, owner = [], [], []
    for qq in range(nq):
        for c in range(nblk):
            blk = block_id(t, qq, c)
            kpos.append((blk >> 1) * PAGE + lane)
            keep.append((lane >> 6) == (blk & 1))
        owner.append(jnp.full((1, nblk * PAGE), qq, I32))
    kpos = jnp.concatenate(kpos, axis=1)
    keep = jnp.concatenate(keep, axis=1)
    owner = jnp.concatenate(owner, axis=1)
    row = _iota((r, 1), 0)
    qi = row & (nq - 1)
    slope = _pow2_neg(NSA_HPG * g + (row >> (nq.bit_length() - 1)) + 1)
    s1 = _dot(q, kv_t) - slope * (past + qi - kpos).astype(F32)
    s1 = jnp.where(owner == qi, jnp.where(keep, s1, NEG_INF), NEG_INF)
    nw = new_ref[0].astype(BF16)
    tnew = _iota((1, nw.shape[0]), 1)
    s2 = _dot_nt(q, nw) - slope * (qi - tnew).astype(F32)
    s2 = jnp.where(tnew <= qi, s2, NEG_INF)
    mx = jnp.maximum(jnp.max(s1, axis=-1, keepdims=True), jnp.max(s2, axis=-1, keepdims=True))
    e1 = jnp.exp(s1 - mx)
    e2 = jnp.exp(s2 - mx)
    den = jnp.sum(e1, axis=-1, keepdims=True) + jnp.sum(e2, axis=-1, keepdims=True)
    o_ref[0] = (_dot_nt(e1.astype(BF16), kv_t) + _dot(e2.astype(BF16), nw)) / den


def _nsa_sel_sample(pt_flat, ids, db, q, sel_new, sel_pool_t, *, nblk, nq, past):
    steps, r, _ = q.shape
    grid_spec = pltpu.PrefetchScalarGridSpec(
        num_scalar_prefetch=2,
        grid=(steps,),
        in_specs=[pl.BlockSpec((1, r, LANES), lambda t, pt, idr: (t, 0, 0)),
                  pl.BlockSpec((1, sel_new.shape[1], LANES), lambda t, pt, idr: (t // NSA_G, 0, t % NSA_G)),
                  pl.BlockSpec(memory_space=pl.ANY)],
        out_specs=pl.BlockSpec((1, r, LANES), lambda t, pt, idr: (t, 0, 0)),
        scratch_shapes=[pltpu.VMEM((2, nq * nblk, LANES, PAGE), F32), pltpu.SemaphoreType.DMA((2,))],
    )
    return pl.pallas_call(
        functools.partial(_nsa_sel_sample_kernel, nblk=nblk, nq=nq, db=db, past=past),
        grid_spec=grid_spec,
        out_shape=jax.ShapeDtypeStruct((steps, r, LANES), F32),
        compiler_params=_cparams(1),
    )(pt_flat, ids.reshape(-1), q, sel_new, sel_pool_t)


def _nsa_win_sample_kernel(q_ref, st_ref, wtab_ref, new_ref, o_ref, *, nq, past):
    r = q_ref.shape[2]
    wbuf = st_ref.shape[2]
    row = _iota((r, LANES), 0)
    lane_r = _iota((r, LANES), 1)
    qpos = past + (row & (nq - 1))
    qi = _iota((r, 1), 0) & (nq - 1)
    dist_st = qi + wbuf - _iota((1, wbuf), 1)
    tnew = _iota((1, new_ref.shape[1]), 1)
    for bi, g in _batch_groups(q_ref.shape[0]):
        slope = _pow2_neg(NSA_HPG * g + (row >> (nq.bit_length() - 1)) + 1)
        q = q_ref[bi, g]
        q2 = jnp.concatenate([q, _query_aug(slope, qpos, lane_r).astype(BF16)], axis=1)
        st = st_ref[bi, g * LANES:(g + 1) * LANES, :].astype(BF16)
        nw = new_ref[bi, :, g * LANES:(g + 1) * LANES].astype(BF16)
        s1 = _dot(q2, jnp.concatenate([st, wtab_ref[...]], axis=0))
        s1 = jnp.where(dist_st <= WINDOW, s1, NEG_INF)
        s2 = _dot_nt(q, nw) - slope[:, :1] * (qi - tnew).astype(F32)
        s2 = jnp.where(tnew <= qi, s2, NEG_INF)
        mx = jnp.maximum(jnp.max(s1, axis=-1, keepdims=True), jnp.max(s2, axis=-1, keepdims=True))
        e1 = jnp.exp(s1 - mx)
        e2 = jnp.exp(s2 - mx)
        den = jnp.sum(e1, axis=-1, keepdims=True) + jnp.sum(e2, axis=-1, keepdims=True)
        o_ref[bi, g] = (_dot_nt(e1.astype(BF16), st) + _dot(e2.astype(BF16), nw)) / den


def _nsa_win_sample(q, state_t, wtab, win_new, *, nq, past):
    db, g, r, _ = q.shape
    bc = _decode_batch(db)
    return pl.pallas_call(
        functools.partial(_nsa_win_sample_kernel, nq=nq, past=past),
        grid=(db // bc,),
        in_specs=[pl.BlockSpec((bc,) + q.shape[1:], lambda b: (b, 0, 0, 0)),
                  pl.BlockSpec((bc,) + state_t.shape[1:], lambda b: (b, 0, 0)),
                  pl.BlockSpec(wtab.shape, lambda b: (0, 0)),
                  pl.BlockSpec((bc,) + win_new.shape[1:], lambda b: (b, 0, 0))],
        out_specs=pl.BlockSpec((bc, g, r, LANES), lambda b: (b, 0, 0, 0)),
        out_shape=jax.ShapeDtypeStruct((db, g, r, LANES), F32),
        compiler_params=_cparams(1),
    )(q, state_t, wtab, win_new)


def _gate_kernel(g_ref, a_ref, b_ref, c_ref, o_ref):
    o_ref[...] = (g_ref[0] * a_ref[...] + g_ref[1] * b_ref[...] + g_ref[2] * c_ref[...]).astype(o_ref.dtype)


def _nsa_gate_sample(gexp, o_cmp, o_sel, o_win):
    t, w = o_cmp.shape
    row = pl.BlockSpec((t, w), lambda i: (0, 0))
    return pl.pallas_call(
        _gate_kernel,
        grid=(1,),
        in_specs=[pl.BlockSpec((3, t, w), lambda i: (0, 0, 0)), row, row, row],
        out_specs=row,
        out_shape=jax.ShapeDtypeStruct((t, w), BF16),
        compiler_params=_cparams(1),
    )(gexp, o_cmp, o_sel, o_win)


def _xattn_sample_kernel(q_ref, kv_ref, o_ref, *, bc, n_mem):
    r = q_ref.shape[1]
    rows = 2 * X_HEADS * n_mem
    is_key = (_iota((X_HEADS * r, rows), 1) & (2 * X_HEADS - 1)) == (_iota((X_HEADS * r, rows), 0) >> (r.bit_length() - 1))
    for bi in range(bc):
        x = kv_ref[bi].astype(BF16)
        q = jnp.concatenate([q_ref[bi, :, h * X_DH:(h + 1) * X_DH] for h in range(X_HEADS)], axis=0)
        s = jnp.where(is_key, _dot_nt(q, x), NEG_INF)
        e = jnp.exp(s - jnp.max(s, axis=-1, keepdims=True))
        p = e / jnp.sum(e, axis=-1, keepdims=True)
        o = _dot(pltpu.roll(p, X_HEADS, 1).astype(BF16), x)
        for h in range(X_HEADS):
            o_ref[bi, :, h * X_DH:(h + 1) * X_DH] = o[h * r:(h + 1) * r].astype(o_ref.dtype)


def _xattn_sample(q, kv, *, n_mem):
    db, r, w = q.shape
    bc = _decode_batch(db)
    return pl.pallas_call(
        functools.partial(_xattn_sample_kernel, bc=bc, n_mem=n_mem),
        grid=(db // bc,),
        in_specs=[pl.BlockSpec((bc, r, w), lambda i: (i, 0, 0)),
                  pl.BlockSpec((bc,) + kv.shape[1:], lambda i: (i, 0, 0))],
        out_specs=pl.BlockSpec((bc, r, w), lambda i: (i, 0, 0)),
        out_shape=jax.ShapeDtypeStruct((db, r, w), BF16),
        compiler_params=_cparams(1),
    )(q, kv)


def _prep_weights(w_in, mla_w_uq, mla_w_uk, mla_w_uv, nsa_phi_k, nsa_phi_v):
    d = w_in.shape[0]
    z = lambda n: jnp.zeros((d, n), F32)
    o = 0
    cq, o = w_in[:, o:o + 384], o + 384
    ckv, o = w_in[:, o:o + 256], o + 256
    kr, o = w_in[:, o:o + 32], o + 32
    qn, o = w_in[:, o:o + 512], o + 512
    kvn, o = w_in[:, o:o + 768], o + 768
    gn, o = w_in[:, o:o + 24], o + 24
    mg = w_in[:, o:]
    qn_p = jnp.pad(qn.reshape(d, NSA_HEADS, NSA_DH), ((0, 0), (0, 0), (0, LANES - NSA_DH))).reshape(d, -1)
    w_in_p = jnp.concatenate([cq, ckv, z(64), kr, z(32), qn_p, gn[:, :12], z(116), gn[:, 12:], z(116), mg, kvn],
                             axis=1).astype(BF16)
    w_kv_t = jnp.transpose(kvn).astype(BF16)
    w_uq_p = jnp.pad(mla_w_uq, ((0, 0), (0, 0), (0, LANES - MLA_NOPE - MLA_ROPE))).reshape(MLA_Q_LORA, -1).astype(BF16)
    pad_t = lambda w: jnp.pad(jnp.transpose(w, (1, 2, 0)), ((0, 0), (0, LANES - w.shape[2]), (0, 0))).reshape(
        MLA_HEADS * LANES, MLA_KV_LORA).astype(BF16)
    w_uk_t, w_uv_t = pad_t(mla_w_uk), pad_t(mla_w_uv)
    rope_pass = np.zeros((LANES, 384), np.float32)
    rope_pass[MLA_NOPE:MLA_NOPE + MLA_ROPE, MLA_KV_LORA:MLA_KV_LORA + MLA_ROPE] = np.eye(MLA_ROPE)
    w_abs = jnp.pad(jnp.transpose(mla_w_uk, (1, 2, 0)), ((0, 0), (0, LANES - MLA_NOPE), (0, 384 - MLA_KV_LORA))) + rope_pass
    eye_h = np.eye(MLA_HEADS, dtype=np.float32)[:, None, :, None]
    w_ov = (eye_h * jnp.transpose(mla_w_uv, (1, 0, 2))[:, :, None, :]).reshape(MLA_HEADS * MLA_KV_LORA, MLA_HEADS * MLA_V)
    maps = jnp.stack([jnp.transpose(m) for g in range(NSA_G) for m in (nsa_phi_k[g], nsa_phi_v[g])])
    phi_t = (np.eye(4, dtype=np.float32)[:, None, :, None] * maps[:, :, None, :]).reshape(4 * NSA_DH, 4 * NSA_DH)
    return w_in_p, w_kv_t, w_uq_p, w_uk_t, w_uv_t, w_abs.astype(BF16), w_ov.astype(BF16), phi_t.astype(BF16)


def _rope_tables(pos):
    half = MLA_ROPE // 2
    inv = ROPE_BASE ** (-np.arange(half, dtype=np.float64) / half)
    ang = pos.astype(np.float64)[:, None] * inv[None, :]
    cos, sin = np.cos(ang).astype(np.float32), np.sin(ang).astype(np.float32)
    n = pos.shape[0]
    one, zero = np.ones((n, MLA_NOPE), np.float32), np.zeros((n, half), np.float32)
    tail = np.zeros((n, LANES - MLA_NOPE - MLA_ROPE), np.float32)
    rc = np.concatenate([one, cos, cos, tail], axis=1)
    rs1 = np.concatenate([0 * one, -sin, zero, tail], axis=1)
    rs2 = np.concatenate([0 * one, zero, sin, tail], axis=1)
    return rc, rs1, rs2


def _key_rows(pos):
    rows = np.arange(LANES)[:, None]
    blk = (pos >> 6)[None, :]
    t = np.where((rows == blk) & (rows < SEL_BLOCK), 1.0, 0.0)
    t = np.where(rows == A_BLK, blk, t)
    t = np.where(rows == A_OFF, (pos & (SEL_BLOCK - 1))[None, :], t)
    t = np.where((rows == A_ONE0) | (rows == A_ONE1), 1.0, t)
    return t.astype(BF16)


def _avg_matrix(n_keys, order):
    blk_of_key = np.tile(np.arange(n_keys) // CMP_BLOCK, 2)[:, None]
    return np.where(blk_of_key == order[None, :], 1.0 / CMP_BLOCK, 0.0).astype(BF16)


def _even_odd_order(n):
    return np.concatenate([np.arange(0, n, 2), np.arange(1, n, 2)])


def kernel(x_prompt, x_sample, mem_prompt, cache_mla_ckv, cache_mla_krope, cache_nsa_cmp_kv, cache_nsa_sel_kv,
           state_nsa_win_kv, cache_xattn_kv, page_table, ffn1_norm, ffn1_w_gate, ffn1_w_up, ffn1_w_down, mix_norm,
           w_in, mla_q_norm, mla_w_uq, mla_kv_norm, mla_w_uk, mla_w_uv, nsa_phi_k, nsa_phi_v, w_br_mla, w_br_nsa,
           w_out, xattn_norm, xattn_mem_norm, xattn_w_q, xattn_w_kv, xattn_w_o, ffn2_norm, ffn2_w_gate, ffn2_w_up,
           ffn2_w_down, final_norm):
    assert x_prompt.shape[2] == D_MODEL and ffn1_norm.shape[0] == 1
    b, s, d = x_prompt.shape
    db, nq, _ = x_sample.shape
    n_pages = page_table.shape[1]
    past = n_pages * PAGE
    n_mem = mem_prompt.shape[1]
    assert s % QTILE == 0 and past % SEL_BLOCK == 0 and nq <= 8 and NSA_G == 2
    g2 = lambda a: a.reshape(1, -1)
    bf = lambda a: a.astype(BF16)

    w_in_p, w_kv_t, w_uq_p, w_uk_t, w_uv_t, w_abs, w_ov, phi_t = _prep_weights(
        w_in[0], mla_w_uq[0], mla_w_uk[0], mla_w_uv[0], nsa_phi_k[0], nsa_phi_v[0])
    f1 = (g2(ffn1_norm[0]), bf(ffn1_w_gate[0]), bf(ffn1_w_up[0]), bf(ffn1_w_down[0]))
    f2 = (g2(ffn2_norm[0]), bf(ffn2_w_gate[0]), bf(ffn2_w_up[0]), bf(ffn2_w_down[0]))
    fg = g2(final_norm)
    w_a, w_b, w_o = bf(w_br_mla[0]), bf(w_br_nsa[0]), bf(w_out[0])
    w_xq, w_xo = bf(xattn_w_q[0]), bf(xattn_w_o[0])
    w_xkv = bf(xattn_w_kv[0].reshape(d, -1))
    mixg, qng, kvg = g2(mix_norm[0]), g2(mla_q_norm[0]), g2(mla_kv_norm[0])

    t = b * s
    nt = s // TILE
    hp = _ffn(x_prompt.reshape(t, d), *f1, fg, final=False)
    (q_mla, k_t, v_t, p_ckv, p_kr_t, q_nsa, p_cmp_t, p_sel_t, p_win_t, sk, sv, wk, wv, gates, merge) = (
        _mixer_project_prompt(hp, b, s, mixg, w_in_p[:, :C_KVN], w_kv_t, qng, w_uq_p, kvg, w_uk_t, w_uv_t,
                              *_rope_tables(np.arange(s))))
    o_a = _mla_prompt(q_mla.reshape(b, s, -1), k_t, v_t).reshape(t, -1)

    nc = s // CMP_BLOCK
    order = _even_odd_order(nc)
    ckcv_t = _compress_prompt(p_cmp_t, _avg_matrix(s, order), phi_t)
    ctab = _key_rows(order * CMP_BLOCK + (CMP_BLOCK - 1))
    ktab = np.transpose(_key_rows(np.arange(s)).reshape(LANES, nt, TILE), (1, 0, 2))
    o_b = _nsa_prompt(q_nsa.reshape(b, s, -1), ckcv_t, ctab, sk, sv, wk, wv, ktab, gates.reshape(b, s, -1)).reshape(t, -1)
    hp = _merge(o_a, o_b, merge, hp, w_a, w_b, w_o)

    kv_mem = _norm_matmul(mem_prompt.reshape(b * n_mem, d), g2(xattn_mem_norm[0]), w_xkv)
    xq = _norm_matmul(hp, g2(xattn_norm[0]), w_xq, scale=X_SCALE, out_dtype=BF16)
    xo = _xattn_prompt(xq.reshape(b, s, -1), bf(kv_mem).reshape(b, n_mem, -1))
    hp = _matmul_residual(xo.reshape(t, -1), w_xo, hp)
    y_prompt = _ffn(hp, *f2, fg, final=True).reshape(b, s, d)

    wlen = min(WINDOW, s)
    kv_out = lambda a: jnp.transpose(a.reshape(1, b, NSA_G, 2, NSA_DH, a.shape[-1]), (0, 1, 5, 2, 3, 4))
    prompt_caches = (p_ckv.reshape(1, b, s, -1), jnp.transpose(p_kr_t, (0, 2, 1))[None], kv_out(p_cmp_t),
                     kv_out(p_sel_t), kv_out(p_win_t[:, :, s - wlen:]),
                     kv_mem.reshape(1, b, n_mem, 2, X_HEADS, X_DH))

    ts = db * nq
    hs = _ffn(x_sample.reshape(ts, d), *f1, fg, final=False)
    pos_s = np.tile(past + np.arange(nq), db)
    (q_mla, s_ckv, s_krb, q_nsa, s_cmp, s_sel, s_win, gates, merge) = _mixer_project_sample(
        hs, mixg, w_in_p, qng, w_uq_p, kvg, *_rope_tables(pos_s))
    s_kr = s_krb[:, MLA_NOPE:MLA_NOPE + MLA_ROPE]
    pad_rows = lambda a: jnp.pad(a.reshape(db, nq, -1), ((0, 0), (0, 8 - nq), (0, 0)))
    pt_flat = jnp.transpose(page_table).reshape(-1)

    qa = _mla_absorb_q(q_mla, w_abs)
    qa = jnp.transpose(qa.reshape(MLA_HEADS, db, nq, -1), (1, 0, 2, 3)).reshape(db, MLA_HEADS * nq, -1)
    kr_pool_t = jnp.transpose(cache_mla_krope[0], (0, 2, 1))
    o_lat = _mla_sample(pt_flat, db, n_pages, qa, pad_rows(s_ckv), pad_rows(s_kr), cache_mla_ckv[0], kr_pool_t, nq=nq)
    o_lat = jnp.transpose(o_lat.reshape(db, MLA_HEADS, nq, -1), (0, 2, 1, 3)).reshape(ts, -1)
    o_a = _matmul(bf(o_lat), w_ov, out_dtype=BF16)

    fm_pool = lambda c: jnp.transpose(c[0], (0, 2, 3, 4, 1)).reshape(c.shape[1], -1, PAGE)
    cmp_pool_t, sel_pool_t = fm_pool(cache_nsa_cmp_kv), fm_pool(cache_nsa_sel_kv)
    keys_per_dot = min(past, 32 * PAGE)
    avg2_s = _avg_matrix(keys_per_dot, np.arange(keys_per_dot // CMP_BLOCK))
    ckcv_s = _compress_sample(pt_flat, db, n_pages, cmp_pool_t, avg2_s, jnp.transpose(phi_t))
    ncs = past // CMP_BLOCK
    ctab_s = np.transpose(_key_rows(np.arange(ncs) * CMP_BLOCK + (CMP_BLOCK - 1)))
    qn = jnp.transpose(q_nsa.reshape(db, nq, NSA_G, NSA_HPG, LANES), (0, 2, 3, 1, 4))
    q_rows = qn.reshape(db, NSA_G, NSA_HPG * nq, LANES)
    o_cmp, imp = _nsa_cmp_sample(q_rows, ckcv_s, ctab_s, nq=nq, past=past)
    n_sel = -(-(past + nq) // SEL_BLOCK)
    n_pick = min(SEL_TOPK, n_sel) - 2
    ids = _topk_sample(imp[:, :, :nq].reshape(db * NSA_G * nq, -1), n_pick=n_pick, last_blk=n_sel - 1)
    o_sel = _nsa_sel_sample(pt_flat, ids[:, :SEL_TOPK], db, q_rows.reshape(db * NSA_G, -1, LANES), pad_rows(s_sel),
                            sel_pool_t, nblk=n_pick + 1, nq=nq, past=past).reshape(db, NSA_G, -1, LANES)
    wbuf = state_nsa_win_kv.shape[2]
    state_t = jnp.transpose(state_nsa_win_kv[0], (0, 2, 3, 4, 1)).reshape(db, -1, wbuf)
    wtab = _key_rows(past - wbuf + np.arange(wbuf))
    o_win = _nsa_win_sample(q_rows, state_t, wtab, pad_rows(s_win), nq=nq, past=past)
    to_tok = lambda a: jnp.transpose(a[..., NSA_DH:].reshape(db, NSA_G, NSA_HPG, nq, NSA_DH), (0, 3, 1, 2, 4)).reshape(ts, -1)
    gts =jnp.stack([gates[:, :12], gates[:, LANES:LANES + 12]], axis=1).reshape(ts, NSA_HEADS, 3)
    gexp = jnp.transpose(jnp.broadcast_to(gts[:, :, :, None], (ts, NSA_HEADS, 3, NSA_DH)), (2, 0, 1, 3)).reshape(3, ts, -1)
    o_b = _nsa_gate_sample(gexp, to_tok(o_cmp), to_tok(o_sel), to_tok(o_win))
    hs = _merge(o_a, o_b, merge, hs, w_a, w_b, w_o)

    xq = _norm_matmul(hs, g2(xattn_norm[0]), w_xq, scale=X_SCALE, out_dtype=BF16)
    xo = _xattn_sample(pad_rows(xq), cache_xattn_kv[0].reshape(db, -1, X_DH), n_mem=n_mem)
    hs = _matmul_residual(xo[:, :nq].reshape(ts, -1), w_xo, hs)
    y_sample = _ffn(hs, *f2, fg, final=True).reshape(db, nq, d)

    kvs = (1, db, nq, NSA_G, 2, NSA_DH)
    win_upd = jnp.concatenate([state_nsa_win_kv[0], s_win.reshape(kvs[1:])], axis=1)[:, nq:]
    sample_caches = (s_ckv.reshape(1, db, nq, -1), s_kr.reshape(1, db, nq, -1), s_cmp.reshape(kvs),
                     s_sel.reshape(kvs), win_upd[None])
    return (y_prompt, y_sample) + prompt_caches + sample_caches
```
